```python
import jax
import jax.numpy as jnp
from jax import lax
import numpy as np

D_MODEL = 4096
BATCH = 1
SEQ = 16384
DEPTH = 4

GRID_W = 64
CTX_LEN = 256
EPS = 1e-6
GLA_HEADS = 8
GLA_DK = 64
GLA_DV = 128
GLA_CHUNK = 64
DECAY_RANK = 16
GLA_TAU = 16.0
SG_GROUPS = 8
SG_GROUP_CH = 128
SG_CHUNK = 128
N_EXPERTS = 16
EXPERT_FF = 256
EC_CAPACITY = 2
ADA_INIT = 0.5

QK_DIM = GLA_HEADS * GLA_DK
V_DIM = GLA_HEADS * GLA_DV
SG_DIM = SG_GROUPS * SG_GROUP_CH
OFF_K = QK_DIM
OFF_V = 2 * QK_DIM
OFF_DF = OFF_V + V_DIM
OFF_DB = OFF_DF + DECAY_RANK
SCAN_COLS = OFF_DB + DECAY_RANK
OFF_U = SCAN_COLS + V_DIM
OFF_VS = OFF_U + SG_DIM
OFF_GA = OFF_VS + SG_DIM
OFF_GB = OFF_GA + D_MODEL
N_IN = OFF_GB + D_MODEL

kernel_name = 'hybrid_gla_sgu_ecmoe_dit_trunk'


def rms_norm(x, g):
    xf = x.astype(jnp.float32)
    y = xf * lax.rsqrt(jnp.mean(xf * xf, axis=-1, keepdims=True) + EPS)
    return (y * g.astype(jnp.float32)).astype(x.dtype)


def layer_norm(x, g, b):
    xf = x.astype(jnp.float32)
    xc = xf - jnp.mean(xf, axis=-1, keepdims=True)
    y = xc * lax.rsqrt(jnp.mean(xc * xc, axis=-1, keepdims=True) + EPS)
    return (y * g.astype(jnp.float32) + b.astype(jnp.float32)).astype(x.dtype)


def adaln(cond, w, b):
    m = jnp.einsum('bd,de->be', jax.nn.silu(cond), w) + b
    return jnp.split(m[:, None, :], 6, axis=-1)


def to_heads(t):
    b, l, _ = t.shape
    return t.reshape(b, l, GLA_HEADS, -1).transpose(0, 2, 1, 3).astype(jnp.float32)


def gla_inputs(p, w_dec_up, b_dec):
    q = to_heads(p[..., :OFF_K]) * (GLA_DK ** -0.5)
    k = to_heads(p[..., OFF_K:OFF_V])
    v = to_heads(p[..., OFF_V:OFF_DF])

    def log_decay(z, w, b):
        zf = jnp.einsum('blr,rk->blk', z.astype(jnp.float32), w.astype(jnp.float32)) + b.astype(jnp.float32)
        return to_heads(jax.nn.log_sigmoid(zf) / GLA_TAU)

    g_f = log_decay(p[..., OFF_DF:OFF_DB], w_dec_up[0], b_dec[0])
    g_b = log_decay(p[..., OFF_DB:SCAN_COLS], w_dec_up[1], b_dec[1])
    return q, k, v, g_f, g_b


def gla_chunk_scan(q, k, v, g, s0):
    b, h, l, dk = q.shape
    n = l // GLA_CHUNK

    def chunks(t):
        return t.reshape(b, h, n, GLA_CHUNK, t.shape[-1]).transpose(2, 0, 1, 3, 4)

    tri = jnp.tril(jnp.ones((GLA_CHUNK, GLA_CHUNK), dtype=bool))[None, None, :, :, None]

    def step(s, inp):
        qc, kc, vc, gc = inp
        bc = jnp.cumsum(gc, axis=-2)
        o_inter = jnp.einsum('bhik,bhkv->bhiv', qc * jnp.exp(bc), s)
        diff = bc[:, :, :, None, :] - bc[:, :, None, :, :]
        dec = jnp.exp(jnp.where(tri, diff, -jnp.inf))
        a = jnp.einsum('bhik,bhjk,bhijk->bhij', qc, kc, dec)
        o = o_inter + jnp.einsum('bhij,bhjv->bhiv', a, vc)
        b_last = bc[:, :, -1:, :]
        s_new = jnp.exp(b_last[:, :, 0, :])[..., None] * s + jnp.einsum(
            'bhjk,bhjv->bhkv', kc * jnp.exp(b_last - bc), vc)
        return s_new, o

    s_fin, o = lax.scan(step, s0, (chunks(q), chunks(k), chunks(v), chunks(g)))
    o = o.transpose(1, 2, 0, 3, 4).reshape(b, h, l, v.shape[-1])
    return o, s_fin


def gla_direction(ctx_in, lat_in, flip):
    rev = (lambda t: jnp.flip(t, axis=2)) if flip else (lambda t: t)
    qc, kc, vc, gc = (rev(t) for t in ctx_in)
    b, h, _, dk = qc.shape
    s0 = jnp.zeros((b, h, dk, vc.shape[-1]), jnp.float32)
    o_c, s_c = gla_chunk_scan(qc, kc, vc, gc, s0)
    qx, kx, vx, gx = (rev(t) for t in lat_in)
    o_x, _ = gla_chunk_scan(qx, kx, vx, gx, s_c)
    return rev(o_c), rev(o_x)


def gla_output(o, r, g):
    o = o * lax.rsqrt(jnp.mean(o * o, axis=-1, keepdims=True) + EPS)
    b, h, l, dv = o.shape
    o = o.transpose(0, 2, 1, 3).reshape(b, l, h * dv) * g.astype(jnp.float32)
    return (o * jax.nn.silu(r.astype(jnp.float32))).astype(r.dtype)


def spatial_gating(u, vs, ln_g, ln_b, w_s, b_s, n_chunks):
    u = jax.nn.gelu(u)
    vs = layer_norm(jax.nn.gelu(vs), ln_g, ln_b)
    b, l, _ = vs.shape
    vc = vs.reshape(b, n_chunks, l // n_chunks, SG_GROUPS, SG_GROUP_CH)
    mixed = jnp.einsum('gij,bnjgc->bnigc', w_s, vc) + b_s.T[None, None, :, :, None]
    return u * mixed.reshape(b, l, SG_DIM)


def merge_branches(p, o_gla, gla_norm_g, sg_ln_g, sg_ln_b, sg_w, sg_b,
                   w_branch_a, w_branch_b, w_out, n_chunks):
    a = gla_output(o_gla, p[..., SCAN_COLS:OFF_U], gla_norm_g)
    s = spatial_gating(p[..., OFF_U:OFF_VS], p[..., OFF_VS:OFF_GA], sg_ln_g, sg_ln_b, sg_w, sg_b, n_chunks)
    y_a = jnp.einsum('blk,kd->bld', a, w_branch_a)
    y_b = jnp.einsum('blk,kd->bld', s, w_branch_b)
    m = jax.nn.sigmoid(p[..., OFF_GA:OFF_GB]) * y_a + jax.nn.sigmoid(p[..., OFF_GB:]) * y_b
    return jnp.einsum('bld,de->ble', m, w_out)


def expert_choice_ffn(h, w_router, w1, w3, w2):
    b, n, d = h.shape
    cap = EC_CAPACITY * n // N_EXPERTS
    aff = jax.nn.softmax(jnp.einsum('bnd,de->bne', h.astype(jnp.float32),
                                    w_router.astype(jnp.float32)), axis=-1)
    gate, idx = lax.top_k(aff.transpose(0, 2, 1), cap)
    xs = jax.vmap(lambda hb, ib: hb[ib])(h, idx)
    hid = jax.nn.silu(jnp.einsum('becd,edf->becf', xs, w1)) * jnp.einsum('becd,edf->becf', xs, w3)
    y = jnp.einsum('becf,efd->becd', hid, w2) * gate[..., None].astype(h.dtype)
    return jax.vmap(lambda ib, yb: jnp.zeros((n, d), yb.dtype).at[ib.reshape(-1)].add(
        yb.reshape(-1, d)))(idx, y)


def setup_inputs(seed: int = 0) -> dict:
    key = jax.random.key(seed)
    ks = jax.random.split(key, 24)

    def nrm(k, shape, scale):
        return jax.random.normal(k, shape, jnp.float32) * scale

    L = DEPTH
    return {
        'x': nrm(ks[0], (BATCH, SEQ, D_MODEL), 1.0),
        'c': nrm(ks[1], (BATCH, D_MODEL), 1.0),
        'ctx': nrm(ks[2], (BATCH, CTX_LEN, D_MODEL), 1.0),
        'c_ctx': nrm(ks[3], (D_MODEL,), 1.0),
        'ada_w': nrm(ks[4], (L, D_MODEL, 6 * D_MODEL), ADA_INIT * D_MODEL ** -0.5),
        'ada_b': nrm(ks[5], (L, 6 * D_MODEL), 0.02),
        'norm1_g': 1.0 + nrm(ks[6], (L, D_MODEL), 0.1),
        'norm2_g': 1.0 + nrm(ks[7], (L, D_MODEL), 0.1),
        'w_in': nrm(ks[8], (L, D_MODEL, N_IN), D_MODEL ** -0.5),
        'w_dec_up': nrm(ks[9], (L, 2, DECAY_RANK, QK_DIM), DECAY_RANK ** -0.5),
        'b_dec': nrm(ks[10], (L, 2, QK_DIM), 0.5),
        'gla_norm_g': 1.0 + nrm(ks[11], (L, V_DIM), 0.1),
        'sg_ln_g': 1.0 + nrm(ks[12], (L, SG_DIM), 0.1),
        'sg_ln_b': nrm(ks[13], (L, SG_DIM), 0.02),
        'sg_w': nrm(ks[14], (L, SG_GROUPS, SG_CHUNK, SG_CHUNK), 0.5 * SG_CHUNK ** -0.5),
        'sg_b': 1.0 + nrm(ks[15], (L, SG_GROUPS, SG_CHUNK), 0.1),
        'w_branch_a': nrm(ks[16], (L, V_DIM, D_MODEL), V_DIM ** -0.5),
        'w_branch_b': nrm(ks[17], (L, SG_DIM, D_MODEL), SG_DIM ** -0.5),
        'w_out': nrm(ks[18], (L, D_MODEL, D_MODEL), D_MODEL ** -0.5),
        'w_router': nrm(ks[19], (L, D_MODEL, N_EXPERTS), D_MODEL ** -0.5),
        'w_exp1': nrm(ks[20], (L, N_EXPERTS, D_MODEL, EXPERT_FF), D_MODEL ** -0.5),
        'w_exp3': nrm(ks[21], (L, N_EXPERTS, D_MODEL, EXPERT_FF), D_MODEL ** -0.5),
        'w_exp2': nrm(ks[22], (L, N_EXPERTS, EXPERT_FF, D_MODEL), EXPERT_FF ** -0.5),
        'final_g': 1.0 + nrm(ks[23], (D_MODEL,), 0.1),
    }


def reference(x, c, ctx, c_ctx, ada_w, ada_b, norm1_g, norm2_g, w_in, w_dec_up, b_dec,
              gla_norm_g, sg_ln_g, sg_ln_b, sg_w, sg_b, w_branch_a, w_branch_b, w_out,
              w_router, w_exp1, w_exp3, w_exp2, final_g):
    rows = x.shape[1] // GRID_W
    lat_chunks = rows * GRID_W // SG_CHUNK
    ctx_chunks = ctx.shape[1] // SG_CHUNK
    for l in range(DEPTH):
        last = l == DEPTH - 1
        sh1, sc1, g1, sh2, sc2, g2 = adaln(c, ada_w[l], ada_b[l])
        csh1, csc1, cg1, csh2, csc2, cg2 = adaln(c_ctx[None], ada_w[l], ada_b[l])

        h_x = rms_norm(x, norm1_g[l]) * (1 + sc1) + sh1
        h_c = rms_norm(ctx, norm1_g[l]) * (1 + csc1) + csh1
        p_x = jnp.einsum('bld,dn->bln', h_x, w_in[l])
        w_in_c = w_in[l][:, :SCAN_COLS] if last else w_in[l]
        p_c = jnp.einsum('bld,dn->bln', h_c, w_in_c)

        qx, kx, vx, gfx, gbx = gla_inputs(p_x, w_dec_up[l], b_dec[l])
        qc, kc, vc, gfc, gbc = gla_inputs(p_c, w_dec_up[l], b_dec[l])
        o_cf, o_xf = gla_direction((qc, kc, vc, gfc), (qx, kx, vx, gfx), False)
        o_cb, o_xb = gla_direction((qc, kc, vc, gbc), (qx, kx, vx, gbx), True)

        x = x + g1 * merge_branches(p_x, o_xf + o_xb, gla_norm_g[l], sg_ln_g[l], sg_ln_b[l],
                                    sg_w[l], sg_b[l], w_branch_a[l], w_branch_b[l], w_out[l],
                                    lat_chunks)
        h2 = rms_norm(x, norm2_g[l]) * (1 + sc2) + sh2
        x = x + g2 * expert_choice_ffn(h2, w_router[l], w_exp1[l], w_exp3[l], w_exp2[l])

        if not last:
            ctx = ctx + cg1 * merge_branches(p_c, o_cf + o_cb, gla_norm_g[l], sg_ln_g[l], sg_ln_b[l],
                                             sg_w[l], sg_b[l], w_branch_a[l], w_branch_b[l],
                                             w_out[l], ctx_chunks)
            hc2 = rms_norm(ctx, norm2_g[l]) * (1 + csc2) + csh2
            ctx = ctx + cg2 * expert_choice_ffn(hc2, w_router[l], w_exp1[l], w_exp3[l], w_exp2[l])
    return rms_norm(x, final_g)
```

```python
import functools

import jax
import jax.numpy as jnp
from jax import lax
from jax.experimental import pallas as pl
from jax.experimental.pallas import tpu as pltpu

EPS = 1e-6
GLA_HEADS = 8
GLA_DK = 64
GLA_DV = 128
DECAY_RANK = 16
GLA_TAU = 16.0
SG_GROUPS = 8
SG_GROUP_CH = 128
SG_CHUNK = 128
N_EXPERTS = 16
EXPERT_FF = 256
EC_CAPACITY = 2

QK_DIM = GLA_HEADS * GLA_DK
V_DIM = GLA_HEADS * GLA_DV
SG_DIM = SG_GROUPS * SG_GROUP_CH
OFF_K = QK_DIM
OFF_V = 2 * QK_DIM
OFF_DF = OFF_V + V_DIM
OFF_DB = OFF_DF + DECAY_RANK
SCAN_COLS = OFF_DB + DECAY_RANK
OFF_U = SCAN_COLS + V_DIM
OFF_VS = OFF_U + SG_DIM
OFF_GA = OFF_VS + SG_DIM

MAIN_COL_R = OFF_DF // V_DIM
MAIN_COL_U = MAIN_COL_R + 1
MAIN_COL_VS = MAIN_COL_R + 2

LANES = 128
ROW_BLOCK = 256
HEAD_PAIR = 2 * GLA_DK
EXP_CLAMP = 80.0
VMEM_LIMIT_BYTES = 52 * 1024 * 1024

F32 = jnp.float32
BF16 = jnp.bfloat16
NT_DIMS = (((1,), (1,)), ((), ()))
TN_DIMS = (((0,), (0,)), ((), ()))


def _pick(n, candidates):
    for c in candidates:
        if n % c == 0:
            return c
    raise ValueError(f"no tile of {candidates} divides {n}")


def _params(*semantics):
    return pltpu.CompilerParams(dimension_semantics=semantics,
                                vmem_limit_bytes=VMEM_LIMIT_BYTES)


def _stream_of_block(tm, ctx_rows):
    return lambda i: (i * tm >= ctx_rows).astype(jnp.int32)


def _adaln_kernel(cond_ref, w_ref, b_ref, out_ref, *, rows):
    d = cond_ref.shape[0]
    tn = out_ref.shape[-1]

    def body(i, acc):
        r0 = pl.multiple_of(i * rows, rows)
        cond = cond_ref[pl.ds(r0, rows), :]
        s = cond * jax.nn.sigmoid(cond)
        w = w_ref[0, pl.ds(r0, rows), :]
        a0 = jnp.sum(w * s[:, 0:1], axis=0, keepdims=True)
        a1 = jnp.sum(w * s[:, 1:2], axis=0, keepdims=True)
        return acc + jnp.concatenate([a0, a1], axis=0)

    acc = lax.fori_loop(0, d // rows, body, jnp.zeros((2, tn), F32))
    out_ref[0] = acc + b_ref[0]


def _adaln(cond2, ada_w, ada_b):
    nl, d, n6 = ada_w.shape
    tn = _pick(n6, (1024, 512, 256, 128))
    return pl.pallas_call(
        functools.partial(_adaln_kernel, rows=_pick(d, (256, 128, 8))),
        out_shape=jax.ShapeDtypeStruct((nl, 2, n6), F32),
        grid=(nl, n6 // tn),
        in_specs=[pl.BlockSpec((d, 2), lambda l, j: (0, 0)),
                  pl.BlockSpec((1, d, tn), lambda l, j: (l, 0, j)),
                  pl.BlockSpec((1, 1, tn), lambda l, j: (l, 0, j))],
        out_specs=pl.BlockSpec((1, 2, tn), lambda l, j: (l, 0, j)),
        compiler_params=_params("parallel", "parallel"),
        name="adaln",
    )(cond2, ada_w, ada_b.reshape(nl, 1, n6))


def _modulated_norm(x, g, mod, shift_row, scale_row):
    ms = jnp.mean(x * x, axis=-1, keepdims=True)
    y = x * lax.rsqrt(ms + EPS) * g
    return y * (1.0 + mod[scale_row:scale_row + 1, :]) + mod[shift_row:shift_row + 1, :]


def _prenorm_kernel(x_ref, g_ref, mod_ref, h_ref, *, shift_row, scale_row):
    h = _modulated_norm(x_ref[...], g_ref[...], mod_ref[0], shift_row, scale_row)
    h_ref[...] = h.astype(BF16)


def _prenorm_router_kernel(x_ref, g_ref, mod_ref, wr_ref, h_ref, logit_ref, *, shift_row, scale_row):
    h = _modulated_norm(x_ref[...], g_ref[...], mod_ref[0], shift_row, scale_row)
    h_ref[...] = h.astype(BF16)
    logit_ref[...] = jnp.dot(h, wr_ref[...], precision=lax.Precision.HIGHEST,
                             preferred_element_type=F32)


def _prenorm(x, g, mods, ctx_rows, shift_row, scale_row, w_router=None):
    m, d = x.shape
    tm = ROW_BLOCK
    stream = _stream_of_block(tm, ctx_rows)
    in_specs = [pl.BlockSpec((tm, d), lambda i: (i, 0)),
                pl.BlockSpec((1, d), lambda i: (0, 0)),
                pl.BlockSpec((1, 8, d), lambda i: (stream(i), 0, 0))]
    h_spec = pl.BlockSpec((tm, d), lambda i: (i, 0))
    h_shape = jax.ShapeDtypeStruct((m, d), BF16)
    if w_router is None:
        return pl.pallas_call(
            functools.partial(_prenorm_kernel, shift_row=shift_row, scale_row=scale_row),
            out_shape=h_shape, grid=(m // tm,), in_specs=in_specs, out_specs=h_spec,
            compiler_params=_params("parallel"), name="prenorm",
        )(x, g.reshape(1, d), mods)
    return pl.pallas_call(
        functools.partial(_prenorm_router_kernel, shift_row=shift_row, scale_row=scale_row),
        out_shape=(h_shape, jax.ShapeDtypeStruct((m, LANES), F32)),
        grid=(m // tm,),
        in_specs=in_specs + [pl.BlockSpec((d, LANES), lambda i: (0, 0))],
        out_specs=(h_spec, pl.BlockSpec((tm, LANES), lambda i: (i, 0))),
        compiler_params=_params("parallel"), name="prenorm_router",
    )(x, g.reshape(1, d), mods, w_router)


def _final_norm_kernel(x_ref, g_ref, o_ref):
    x = x_ref[...]
    ms = jnp.mean(x * x, axis=-1, keepdims=True)
    o_ref[...] = x * lax.rsqrt(ms + EPS) * g_ref[...]


def _final_norm(x_all, g, ctx_rows):
    m, d = x_all.shape
    tm = ROW_BLOCK
    skip = ctx_rows // tm
    return pl.pallas_call(
        _final_norm_kernel,
        out_shape=jax.ShapeDtypeStruct((m - ctx_rows, d), F32),
        grid=((m - ctx_rows) // tm,),
        in_specs=[pl.BlockSpec((tm, d), lambda i: (i + skip, 0)),
                  pl.BlockSpec((1, d), lambda i: (0, 0))],
        out_specs=pl.BlockSpec((tm, d), lambda i: (i, 0)),
        compiler_params=_params("parallel"), name="final_norm",
    )(x_all, g.reshape(1, d))


def _mm_kernel(a_ref, w_ref, o_ref):
    o_ref[...] = jnp.dot(a_ref[...], w_ref[...], preferred_element_type=F32).astype(o_ref.dtype)


def _matmul(a, w, out_dtype=F32):
    m, k = a.shape
    n = w.shape[1]
    tm = _pick(m, (1280, 1024, 768, 512, 256))
    tn = _pick(n, (512, 256, 128))
    return pl.pallas_call(
        _mm_kernel,
        out_shape=jax.ShapeDtypeStruct((m, n), out_dtype),
        grid=(m // tm, n // tn),
        in_specs=[pl.BlockSpec((tm, k), lambda i, j: (i, 0)),
                  pl.BlockSpec((k, tn), lambda i, j: (0, j))],
        out_specs=pl.BlockSpec((tm, tn), lambda i, j: (i, j)),
        compiler_params=_params("parallel", "arbitrary"), name="matmul",
    )(a, w)


def _mm_resid_kernel(a_ref, w_ref, x_ref, mod_ref, o_ref, *, gate_row, ctx_rows):
    tm = a_ref.shape[0]
    acc = jnp.dot(a_ref[...], w_ref[...], preferred_element_type=F32)
    row = pl.program_id(0) * tm + lax.broadcasted_iota(jnp.int32, (tm, 1), 0)
    gate = jnp.where(row < ctx_rows, mod_ref[0, gate_row:gate_row + 1, :],
                     mod_ref[1, gate_row:gate_row + 1, :])
    o_ref[...] = x_ref[...] + gate * acc


def _matmul_gated_residual(a, w, x, mods, ctx_rows, gate_row):
    m, k = a.shape
    n = w.shape[1]
    tm = _pick(m, (1280, 1024, 768, 512, 256))
    tn = _pick(n, (512, 256, 128))
    return pl.pallas_call(
        functools.partial(_mm_resid_kernel, gate_row=gate_row, ctx_rows=ctx_rows),
        out_shape=jax.ShapeDtypeStruct((m, n), F32),
        grid=(m // tm, n // tn),
        in_specs=[pl.BlockSpec((tm, k), lambda i, j: (i, 0)),
                  pl.BlockSpec((k, tn), lambda i, j: (0, j)),
                  pl.BlockSpec((tm, tn), lambda i, j: (i, j)),
                  pl.BlockSpec((2, 8, tn), lambda i, j: (0, 0, j))],
        out_specs=pl.BlockSpec((tm, tn), lambda i, j: (i, j)),
        compiler_params=_params("parallel", "arbitrary"), name="out_proj",
    )(a, w, x, mods)


def _merge_kernel(h_ref, a_ref, s_ref, wga_ref, wgb_ref, wa_ref, wb_ref, m_ref):
    h = h_ref[...]
    ga = jnp.dot(h, wga_ref[...], preferred_element_type=F32)
    gb = jnp.dot(h, wgb_ref[...], preferred_element_type=F32)
    ya = jnp.dot(a_ref[...], wa_ref[...], preferred_element_type=F32)
    yb = jnp.dot(s_ref[...], wb_ref[...], preferred_element_type=F32)
    m_ref[...] = (jax.nn.sigmoid(ga) * ya + jax.nn.sigmoid(gb) * yb).astype(BF16)


def _merge(h, a, s, w_gates, w_a, w_b):
    m, d = h.shape
    ka = a.shape[1]
    kb = s.shape[1]
    tm = _pick(m, (1280, 1024, 768, 512, 256))
    tn = _pick(d, (256, 128))
    nb = d // tn
    return pl.pallas_call(
        _merge_kernel,
        out_shape=jax.ShapeDtypeStruct((m, d), BF16),
        grid=(m // tm, nb),
        in_specs=[pl.BlockSpec((tm, d), lambda i, j: (i, 0)),
                  pl.BlockSpec((tm, ka), lambda i, j: (i, 0)),
                  pl.BlockSpec((tm, kb), lambda i, j: (i, 0)),
                  pl.BlockSpec((d, tn), lambda i, j: (0, j)),
                  pl.BlockSpec((d, tn), lambda i, j: (0, j + nb)),
                  pl.BlockSpec((ka, tn), lambda i, j: (0, j)),
                  pl.BlockSpec((kb, tn), lambda i, j: (0, j))],
        out_specs=pl.BlockSpec((tm, tn), lambda i, j: (i, j)),
        compiler_params=_params("parallel", "arbitrary"), name="merge",
    )(h, a, s, w_gates, w_gates, w_a, w_b)


def _sgu_kernel(u_ref, vs_ref, lng_ref, lnb_ref, ws_ref, bs_ref, s_ref):
    tm = u_ref.shape[0]
    u = jax.nn.gelu(u_ref[...])
    v = jax.nn.gelu(vs_ref[...])
    vc = v - jnp.mean(v, axis=-1, keepdims=True)
    vn = vc * lax.rsqrt(jnp.mean(vc * vc, axis=-1, keepdims=True) + EPS)
    vn = (vn * lng_ref[...] + lnb_ref[...]).astype(BF16)
    for ch in range(tm // SG_CHUNK):
        rows = slice(ch * SG_CHUNK, (ch + 1) * SG_CHUNK)
        for g in range(SG_GROUPS):
            cols = slice(g * SG_GROUP_CH, (g + 1) * SG_GROUP_CH)
            mixed = jnp.dot(ws_ref[g], vn[rows, cols], preferred_element_type=F32) + bs_ref[g]
            s_ref[rows, cols] = (u[rows, cols] * mixed).astype(BF16)


def _sgu(p_main, ln_g, ln_b, w_s, b_s_bcast):
    m = p_main.shape[0]
    tm = ROW_BLOCK
    return pl.pallas_call(
        _sgu_kernel,
        out_shape=jax.ShapeDtypeStruct((m, SG_DIM), BF16),
        grid=(m // tm,),
        in_specs=[pl.BlockSpec((tm, SG_DIM), lambda i: (i, MAIN_COL_U)),
                  pl.BlockSpec((tm, SG_DIM), lambda i: (i, MAIN_COL_VS)),
                  pl.BlockSpec((1, SG_DIM), lambda i: (0, 0)),
                  pl.BlockSpec((1, SG_DIM), lambda i: (0, 0)),
                  pl.BlockSpec((SG_GROUPS, SG_CHUNK, SG_CHUNK), lambda i: (0, 0, 0)),
                  pl.BlockSpec((SG_GROUPS, SG_CHUNK, SG_GROUP_CH), lambda i: (0, 0, 0))],
        out_specs=pl.BlockSpec((tm, SG_DIM), lambda i: (i, 0)),
        compiler_params=_params("parallel"), name="sgu",
    )(p_main, p_main, ln_g.reshape(1, SG_DIM), ln_b.reshape(1, SG_DIM), w_s, b_s_bcast)


def _split_bf16(x):
    hi = x.astype(BF16)
    lo = (x - hi.astype(F32)).astype(BF16)
    return hi, lo


def _gla_prep_kernel(q_ref, k_ref, v_ref, dl_ref, wup_ref, bdec_ref,
                     qmf_ref, kmf_ref, qef_ref, qmb_ref, kmb_ref, qeb_ref,
                     vb_ref, utf_ref, utb_ref, dtot_ref):
    c = q_ref.shape[0]
    half = c // 2
    z = jnp.dot(dl_ref[...], wup_ref[...], precision=lax.Precision.HIGHEST,
                preferred_element_type=F32) + bdec_ref[...]
    g = -(jnp.maximum(-z, 0.0) + jnp.log1p(jnp.exp(-jnp.abs(z)))) * (1.0 / GLA_TAU)
    ri = lax.broadcasted_iota(jnp.int32, (c, c), 0)
    ci = lax.broadcasted_iota(jnp.int32, (c, c), 1)
    lane = lax.broadcasted_iota(jnp.int32, (1, HEAD_PAIR), 1)
    q = q_ref[...] * (GLA_DK ** -0.5)
    k = k_ref[...]
    v = v_ref[...]
    vb = v.astype(BF16)
    vb_ref[...] = vb

    def direction(gd, tri, mid_row, end_row, qm_ref, km_ref, qe_ref, ut_ref):
        hi, lo = _split_bf16(gd)
        tri = tri.astype(BF16)
        bc = (jnp.dot(tri, hi, preferred_element_type=F32)
              + jnp.dot(tri, lo, preferred_element_type=F32))
        rho = bc[mid_row:mid_row + 1, :]
        tot = bc[end_row:end_row + 1, :]
        e1 = bc - rho
        qm_ref[...] = (q * jnp.exp(jnp.minimum(e1, EXP_CLAMP))).astype(BF16)
        km_ref[...] = (k * jnp.exp(jnp.minimum(-e1, EXP_CLAMP))).astype(BF16)
        qe_ref[...] = (q * jnp.exp(bc)).astype(BF16)
        k2 = (k * jnp.exp(tot - bc)).astype(BF16)
        for p in range(GLA_HEADS // 2):
            kp = k2[:, p * HEAD_PAIR:(p + 1) * HEAD_PAIR]
            ut = jnp.zeros((GLA_DV, HEAD_PAIR), F32)
            for hh in range(2):
                h = 2 * p + hh
                keep = (lane < GLA_DK) if hh == 0 else (lane >= GLA_DK)
                kh = jnp.where(keep, kp, jnp.zeros_like(kp))
                ut = ut + lax.dot_general(vb[:, h * GLA_DV:(h + 1) * GLA_DV], kh, TN_DIMS,
                                          preferred_element_type=F32)
            ut_ref[0, :, p * HEAD_PAIR:(p + 1) * HEAD_PAIR] = ut
        return jnp.exp(tot)

    df = direction(g[:, :QK_DIM], ci <= ri, half - 1, c - 1, qmf_ref, kmf_ref, qef_ref, utf_ref)
    db = direction(g[:, QK_DIM:], ci >= ri, half, 0, qmb_ref, kmb_ref, qeb_ref, utb_ref)
    dtot_ref[0] = jnp.concatenate([df, db, jnp.zeros((6, QK_DIM), F32)], axis=0)


def _gla_prep(p_main, dlow, wup, bdec):
    qkv = p_main
    m = qkv.shape[0]
    c = ROW_BLOCK
    nb = m // c
    qk_spec = pl.BlockSpec((c, QK_DIM), lambda i: (i, 0))
    bf_qk = jax.ShapeDtypeStruct((m, QK_DIM), BF16)
    ut_shape = jax.ShapeDtypeStruct((nb, GLA_DV, QK_DIM), F32)
    ut_spec = pl.BlockSpec((1, GLA_DV, QK_DIM), lambda i: (i, 0, 0))
    return pl.pallas_call(
        _gla_prep_kernel,
        out_shape=(bf_qk,) * 6 + (jax.ShapeDtypeStruct((m, V_DIM), BF16), ut_shape, ut_shape,
                                  jax.ShapeDtypeStruct((nb, 8, QK_DIM), F32)),
        grid=(nb,),
        in_specs=[pl.BlockSpec((c, QK_DIM), lambda i: (i, 0)),
                  pl.BlockSpec((c, QK_DIM), lambda i: (i, 1)),
                  pl.BlockSpec((c, V_DIM), lambda i: (i, 1)),
                  pl.BlockSpec((c, LANES), lambda i: (i, 0)),
                  pl.BlockSpec((LANES, 2 * QK_DIM), lambda i: (0, 0)),
                  pl.BlockSpec((1, 2 * QK_DIM), lambda i: (0, 0))],
        out_specs=(qk_spec,) * 6 + (pl.BlockSpec((c, V_DIM), lambda i: (i, 0)), ut_spec, ut_spec,
                                    pl.BlockSpec((1, 8, QK_DIM), lambda i: (i, 0, 0))),
        compiler_params=_params("parallel"), name="gla_prep",
    )(qkv, qkv, qkv, dlow, wup, bdec)


def _gla_scan_kernel(utf_ref, dtf_ref, utb_ref, dtb_ref, stf_ref, stb_ref, sf_ref, sb_ref):
    @pl.when(pl.program_id(0) == 0)
    def _():
        sf_ref[...] = jnp.zeros_like(sf_ref)
        sb_ref[...] = jnp.zeros_like(sb_ref)

    sf = sf_ref[...]
    stf_ref[0] = sf.astype(BF16)
    sf_ref[...] = dtf_ref[0, 0:1, :] * sf + utf_ref[0]
    sb = sb_ref[...]
    stb_ref[0] = sb.astype(BF16)
    sb_ref[...] = dtb_ref[0, 1:2, :] * sb + utb_ref[0]


def _gla_scan(utf, utb, dtot, ctx_blocks):
    nb = utf.shape[0]

    def bwd(t):
        return jnp.where(t < ctx_blocks, ctx_blocks - 1 - t, nb + ctx_blocks - 1 - t)

    st_shape = jax.ShapeDtypeStruct((nb, GLA_DV, QK_DIM), BF16)
    blk = (1, GLA_DV, QK_DIM)
    return pl.pallas_call(
        _gla_scan_kernel,
        out_shape=(st_shape, st_shape),
        grid=(nb,),
        in_specs=[pl.BlockSpec(blk, lambda t: (t, 0, 0)),
                  pl.BlockSpec((1, 8, QK_DIM), lambda t: (t, 0, 0)),
                  pl.BlockSpec(blk, lambda t: (bwd(t), 0, 0)),
                  pl.BlockSpec((1, 8, QK_DIM), lambda t: (bwd(t), 0, 0))],
        out_specs=(pl.BlockSpec(blk, lambda t: (t, 0, 0)),
                   pl.BlockSpec(blk, lambda t: (bwd(t), 0, 0))),
        scratch_shapes=[pltpu.VMEM((GLA_DV, QK_DIM), F32), pltpu.VMEM((GLA_DV, QK_DIM), F32)],
        compiler_params=_params("arbitrary"), name="gla_scan",
    )(utf, dtot, utb, dtot)


def _gla_out_kernel(qmf_ref, kmf_ref, qef_ref, qmb_ref, kmb_ref, qeb_ref, vb_ref,
                    stf_ref, stb_ref, r_ref, gain_ref, a_ref):
    c = qmf_ref.shape[0]
    ri = lax.broadcasted_iota(jnp.int32, (c, c), 0)
    ci = lax.broadcasted_iota(jnp.int32, (c, c), 1)
    lower = ci <= ri
    upper = ci >= ri
    lane = lax.broadcasted_iota(jnp.int32, (1, HEAD_PAIR), 1)
    for p in range(GLA_HEADS // 2):
        pair = slice(p * HEAD_PAIR, (p + 1) * HEAD_PAIR)
        qmf, kmf, qef = qmf_ref[:, pair], kmf_ref[:, pair], qef_ref[:, pair]
        qmb, kmb, qeb = qmb_ref[:, pair], kmb_ref[:, pair], qeb_ref[:, pair]
        stf = stf_ref[0, :, pair]
        stb = stb_ref[0, :, pair]
        for hh in range(2):
            h = 2 * p + hh
            keep = (lane < GLA_DK) if hh == 0 else (lane >= GLA_DK)
            zero = jnp.zeros_like(kmf)
            af = lax.dot_general(qmf, jnp.where(keep, kmf, zero), NT_DIMS, preferred_element_type=F32)
            ab = lax.dot_general(qmb, jnp.where(keep, kmb, zero), NT_DIMS, preferred_element_type=F32)
            att = (jnp.where(lower, af, 0.0) + jnp.where(upper, ab, 0.0)).astype(BF16)
            cols = slice(h * GLA_DV, (h + 1) * GLA_DV)
            o = jnp.dot(att, vb_ref[:, cols], preferred_element_type=F32)
            o = o + lax.dot_general(jnp.where(keep, qef, zero), stf, NT_DIMS, preferred_element_type=F32)
            o = o + lax.dot_general(jnp.where(keep, qeb, zero), stb, NT_DIMS, preferred_element_type=F32)
            o = o * lax.rsqrt(jnp.mean(o * o, axis=-1, keepdims=True) + EPS)
            r = r_ref[:, cols]
            a_ref[:, cols] = (o * gain_ref[:, cols] * (r * jax.nn.sigmoid(r))).astype(BF16)


def _gla_out(prep, stf, stb, p_main, gain):
    qmf, kmf, qef, qmb, kmb, qeb, vb = prep
    m = vb.shape[0]
    c = ROW_BLOCK
    qk_spec = pl.BlockSpec((c, QK_DIM), lambda i: (i, 0))
    st_spec = pl.BlockSpec((1, GLA_DV, QK_DIM), lambda i: (i, 0, 0))
    v_spec = pl.BlockSpec((c, V_DIM), lambda i: (i, 0))
    return pl.pallas_call(
        _gla_out_kernel,
        out_shape=jax.ShapeDtypeStruct((m, V_DIM), BF16),
        grid=(m // c,),
        in_specs=[qk_spec] * 6 + [v_spec, st_spec, st_spec,
                                  pl.BlockSpec((c, V_DIM), lambda i: (i, MAIN_COL_R)),
                                  pl.BlockSpec((1, V_DIM), lambda i: (0, 0))],
        out_specs=v_spec,
        compiler_params=_params("parallel"), name="gla_out",
    )(qmf, kmf, qef, qmb, kmb, qeb, vb, stf, stb, p_main, gain.reshape(1, V_DIM))


def _expert_kernel(xs_ref, w1_ref, w3_ref, w2_ref, gate_ref, y_ref):
    x = xs_ref[0]
    h1 = jnp.dot(x, w1_ref[0], preferred_element_type=F32)
    h3 = jnp.dot(x, w3_ref[0], preferred_element_type=F32)
    hid = (h1 * jax.nn.sigmoid(h1) * h3).astype(BF16)
    y = jnp.dot(hid, w2_ref[0], preferred_element_type=F32)
    y_ref[0] = y * gate_ref[0]


def _expert_ffn(xs, gates, w1, w3, w2):
    e, cap, d = xs.shape
    f = w1.shape[-1]
    tc = _pick(cap, (528, 512, 256, 128, 64, 32, 16))
    return pl.pallas_call(
        _expert_kernel,
        out_shape=jax.ShapeDtypeStruct((e, cap, d), F32),
        grid=(e, cap // tc),
        in_specs=[pl.BlockSpec((1, tc, d), lambda i, j: (i, j, 0)),
                  pl.BlockSpec((1, d, f), lambda i, j: (i, 0, 0)),
                  pl.BlockSpec((1, d, f), lambda i, j: (i, 0, 0)),
                  pl.BlockSpec((1, f, d), lambda i, j: (i, 0, 0)),
                  pl.BlockSpec((1, tc, 1), lambda i, j: (i, j, 0))],
        out_specs=pl.BlockSpec((1, tc, d), lambda i, j: (i, j, 0)),
        compiler_params=_params("parallel", "arbitrary"), name="expert_ffn",
    )(xs, w1, w3, w2, gates)


def _route(logits, ctx_rows):
    aff = jax.nn.softmax(logits[:, :N_EXPERTS], axis=-1)
    m = aff.shape[0]
    idx_parts, gate_parts = [], []
    for lo, hi in ((0, ctx_rows), (ctx_rows, m)):
        cap = EC_CAPACITY * (hi - lo) // N_EXPERTS
        gate, idx = lax.top_k(aff[lo:hi].T, cap)
        idx_parts.append(idx + lo)
        gate_parts.append(gate)
    return jnp.concatenate(idx_parts, axis=1), jnp.concatenate(gate_parts, axis=1)


def kernel(x, c, ctx, c_ctx, ada_w, ada_b, norm1_g, norm2_g, w_in, w_dec_up, b_dec, gla_norm_g,
           sg_ln_g, sg_ln_b, sg_w, sg_b, w_branch_a, w_branch_b, w_out, w_router, w_exp1, w_exp3,
           w_exp2, final_g):
    batch, seq, d = x.shape
    ctx_rows = ctx.shape[1]
    depth = ada_w.shape[0]
    assert batch == 1 and seq % ROW_BLOCK == 0 and ctx_rows % ROW_BLOCK == 0
    m = ctx_rows + seq
    off_gb = OFF_GA + d

    xa = jnp.concatenate([ctx[0], x[0]], axis=0)
    cond2 = jnp.stack([c_ctx, c[0]], axis=1)
    mods_all = _adaln(cond2, ada_w, ada_b).reshape(depth, 2, 6, d)
    mods_all = jnp.pad(mods_all, ((0, 0), (0, 0), (0, 2), (0, 0)))

    for l in range(depth):
        mods = mods_all[l]
        wl = w_in[l]
        w_main = jnp.concatenate([wl[:, :OFF_DF], wl[:, SCAN_COLS:OFF_GA]], axis=1).astype(BF16)
        w_dlow = jnp.pad(wl[:, OFF_DF:SCAN_COLS], ((0, 0), (0, LANES - 2 * DECAY_RANK))).astype(BF16)
        w_gates = wl[:, OFF_GA:].astype(BF16)
        wup = jnp.zeros((LANES, 2 * QK_DIM), F32)
        wup = wup.at[:DECAY_RANK, :QK_DIM].set(w_dec_up[l, 0])
        wup = wup.at[DECAY_RANK:2 * DECAY_RANK, QK_DIM:].set(w_dec_up[l, 1])
        bdec = b_dec[l].reshape(1, 2 * QK_DIM)
        b_s_bcast = jnp.broadcast_to(sg_b[l][:, :, None], (SG_GROUPS, SG_CHUNK, SG_GROUP_CH))
        w_r = jnp.pad(w_router[l], ((0, 0), (0, LANES - N_EXPERTS)))

        h = _prenorm(xa, norm1_g[l], mods, ctx_rows, shift_row=0, scale_row=1)
        p_main = _matmul(h, w_main)
        dlow = _matmul(h, w_dlow)

        prep = _gla_prep(p_main, dlow, wup, bdec)
        stf, stb = _gla_scan(prep[7], prep[8], prep[9], ctx_rows // ROW_BLOCK)
        a = _gla_out(prep[:7], stf, stb, p_main, gla_norm_g[l])
        s = _sgu(p_main, sg_ln_g[l], sg_ln_b[l], sg_w[l].astype(BF16), b_s_bcast)
        mix = _merge(h, a, s, w_gates, w_branch_a[l].astype(BF16), w_branch_b[l].astype(BF16))
        xa = _matmul_gated_residual(mix, w_out[l].astype(BF16), xa, mods, ctx_rows, gate_row=2)

        h2, logits = _prenorm(xa, norm2_g[l], mods, ctx_rows, shift_row=3, scale_row=4, w_router=w_r)
        idx, gate = _route(logits, ctx_rows)
        cap = idx.shape[1]
        cap_pad = -(-cap // 32) * 32
        idx_p = jnp.pad(idx, ((0, 0), (0, cap_pad - cap)))
        gate_p = jnp.pad(gate, ((0, 0), (0, cap_pad - cap)))
        xs = h2[idx_p]
        ys = _expert_ffn(xs, gate_p[:, :, None], w_exp1[l].astype(BF16), w_exp3[l].astype(BF16),
                         w_exp2[l].astype(BF16))
        moe = jnp.zeros((m, d), F32).at[idx_p.reshape(-1)].add(ys.reshape(-1, d))
        row_gate = jnp.where(jnp.arange(m)[:, None] < ctx_rows, mods[0, 5][None, :], mods[1, 5][None, :])
        xa = xa + row_gate * moe

    return _final_norm(xa, final_g, ctx_rows)[None]
```

```python
import functools

import jax
import jax.numpy as jnp
from jax import lax
from jax.experimental import pallas as pl
from jax.experimental.pallas import tpu as pltpu

EPS = 1e-6
GLA_HEADS = 8
GLA_DK = 64
GLA_DV = 128
DECAY_RANK = 16
GLA_TAU = 16.0
SG_GROUPS = 8
SG_GROUP_CH = 128
SG_CHUNK = 128
N_EXPERTS = 16
EXPERT_FF = 256
EC_CAPACITY = 2

QK_DIM = GLA_HEADS * GLA_DK
V_DIM = GLA_HEADS * GLA_DV
SG_DIM = SG_GROUPS * SG_GROUP_CH
OFF_K = QK_DIM
OFF_V = 2 * QK_DIM
OFF_DF = OFF_V + V_DIM
OFF_DB = OFF_DF + DECAY_RANK
SCAN_COLS = OFF_DB + DECAY_RANK
OFF_U = SCAN_COLS + V_DIM
OFF_VS = OFF_U + SG_DIM
OFF_GA = OFF_VS + SG_DIM

MAIN_COL_R = OFF_DF // V_DIM
MAIN_COL_U = MAIN_COL_R + 1
MAIN_COL_VS = MAIN_COL_R + 2

LANES = 128
ROW_BLOCK = 256
HEAD_PAIR = 2 * GLA_DK
EXP_CLAMP = 80.0
VMEM_LIMIT_BYTES = 52 * 1024 * 1024

F32 = jnp.float32
BF16 = jnp.bfloat16
NT_DIMS = (((1,), (1,)), ((), ()))
TN_DIMS = (((0,), (0,)), ((), ()))


def _pick(n, candidates):
    for c in candidates:
        if n % c == 0:
            return c
    raise ValueError(f"no tile of {candidates} divides {n}")


def _params(*semantics):
    return pltpu.CompilerParams(dimension_semantics=semantics,
                                vmem_limit_bytes=VMEM_LIMIT_BYTES)


def _stream_of_block(tm, ctx_rows):
    return lambda i: (i * tm >= ctx_rows).astype(jnp.int32)


def _adaln_kernel(cond_ref, w_ref, b_ref, out_ref, *, rows):
    d = cond_ref.shape[0]
    tn = out_ref.shape[-1]

    def body(i, acc):
        r0 = pl.multiple_of(i * rows, rows)
        cond = cond_ref[pl.ds(r0, rows), :]
        s = cond * jax.nn.sigmoid(cond)
        w = w_ref[0, pl.ds(r0, rows), :]
        a0 = jnp.sum(w * s[:, 0:1], axis=0, keepdims=True)
        a1 = jnp.sum(w * s[:, 1:2], axis=0, keepdims=True)
        return acc + jnp.concatenate([a0, a1], axis=0)

    acc = lax.fori_loop(0, d // rows, body, jnp.zeros((2, tn), F32))
    out_ref[0] = acc + b_ref[0]


def _adaln(cond2, ada_w, ada_b):
    nl, d, n6 = ada_w.shape
    tn = _pick(n6, (1024, 512, 256, 128))
    return pl.pallas_call(
        functools.partial(_adaln_kernel, rows=_pick(d, (256, 128, 8))),
        out_shape=jax.ShapeDtypeStruct((nl, 2, n6), F32),
        grid=(nl, n6 // tn),
        in_specs=[pl.BlockSpec((d, 2), lambda l, j: (0, 0)),
                  pl.BlockSpec((1, d, tn), lambda l, j: (l, 0, j)),
                  pl.BlockSpec((1, 1, tn), lambda l, j: (l, 0, j))],
        out_specs=pl.BlockSpec((1, 2, tn), lambda l, j: (l, 0, j)),
        compiler_params=_params("parallel", "parallel"),
        name="adaln",
    )(cond2, ada_w, ada_b.reshape(nl, 1, n6))


def _modulated_norm(x, g, mod, shift_row, scale_row):
    ms = jnp.mean(x * x, axis=-1, keepdims=True)
    y = x * lax.rsqrt(ms + EPS) * g
    return y * (1.0 + mod[scale_row:scale_row + 1, :]) + mod[shift_row:shift_row + 1, :]


def _prenorm_kernel(x_ref, g_ref, mod_ref, h_ref, *, shift_row, scale_row):
    h = _modulated_norm(x_ref[...], g_ref[...], mod_ref[0], shift_row, scale_row)
    h_ref[...] = h.astype(BF16)


def _prenorm_router_kernel(x_ref, g_ref, mod_ref, wr_hi_ref, wr_lo_ref, h_ref, logit_ref, *,
                           shift_row, scale_row):
    h = _modulated_norm(x_ref[...], g_ref[...], mod_ref[0], shift_row, scale_row)
    h_hi, h_lo = _split_bf16(h)
    h_ref[...] = h_hi
    logit_ref[...] = (jnp.dot(h_hi, wr_hi_ref[...], preferred_element_type=F32)
                      + jnp.dot(h_lo, wr_hi_ref[...], preferred_element_type=F32)
                      + jnp.dot(h_hi, wr_lo_ref[...], preferred_element_type=F32))


def _prenorm(x, g, mods, ctx_rows, shift_row, scale_row, w_router=None):
    m, d = x.shape
    tm = ROW_BLOCK
    stream = _stream_of_block(tm, ctx_rows)
    in_specs = [pl.BlockSpec((tm, d), lambda i: (i, 0)),
                pl.BlockSpec((1, d), lambda i: (0, 0)),
                pl.BlockSpec((1, 8, d), lambda i: (stream(i), 0, 0))]
    h_spec = pl.BlockSpec((tm, d), lambda i: (i, 0))
    h_shape = jax.ShapeDtypeStruct((m, d), BF16)
    if w_router is None:
        return pl.pallas_call(
            functools.partial(_prenorm_kernel, shift_row=shift_row, scale_row=scale_row),
            out_shape=h_shape, grid=(m // tm,), in_specs=in_specs, out_specs=h_spec,
            compiler_params=_params("parallel"), name="prenorm",
        )(x, g.reshape(1, d), mods)
    return pl.pallas_call(
        functools.partial(_prenorm_router_kernel, shift_row=shift_row, scale_row=scale_row),
        out_shape=(h_shape, jax.ShapeDtypeStruct((m, LANES), F32)),
        grid=(m // tm,),
        in_specs=in_specs + [pl.BlockSpec((d, LANES), lambda i: (0, 0))] * 2,
        out_specs=(h_spec, pl.BlockSpec((tm, LANES), lambda i: (i, 0))),
        compiler_params=_params("parallel"), name="prenorm_router",
    )(x, g.reshape(1, d), mods, *_split_bf16(w_router))


def _final_norm_kernel(x_ref, g_ref, o_ref):
    x = x_ref[...]
    ms = jnp.mean(x * x, axis=-1, keepdims=True)
    o_ref[...] = x * lax.rsqrt(ms + EPS) * g_ref[...]


def _final_norm(x_all, g, ctx_rows):
    m, d = x_all.shape
    tm = ROW_BLOCK
    skip = ctx_rows // tm
    return pl.pallas_call(
        _final_norm_kernel,
        out_shape=jax.ShapeDtypeStruct((m - ctx_rows, d), F32),
        grid=((m - ctx_rows) // tm,),
        in_specs=[pl.BlockSpec((tm, d), lambda i: (i + skip, 0)),
                  pl.BlockSpec((1, d), lambda i: (0, 0))],
        out_specs=pl.BlockSpec((tm, d), lambda i: (i, 0)),
        compiler_params=_params("parallel"), name="final_norm",
    )(x_all, g.reshape(1, d))


def _mm_kernel(a_ref, w_ref, o_ref):
    o_ref[...] = jnp.dot(a_ref[...], w_ref[...], preferred_element_type=F32).astype(o_ref.dtype)


def _matmul(a, w, out_dtype=F32):
    m, k = a.shape
    n = w.shape[1]
    tm = _pick(m, (1280, 1024, 768, 512, 256))
    tn = _pick(n, (512, 256, 128))
    return pl.pallas_call(
        _mm_kernel,
        out_shape=jax.ShapeDtypeStruct((m, n), out_dtype),
        grid=(m // tm, n // tn),
        in_specs=[pl.BlockSpec((tm, k), lambda i, j: (i, 0)),
                  pl.BlockSpec((k, tn), lambda i, j: (0, j))],
        out_specs=pl.BlockSpec((tm, tn), lambda i, j: (i, j)),
        compiler_params=_params("parallel", "arbitrary"), name="matmul",
    )(a, w)


def _mm_resid_kernel(a_ref, w_ref, x_ref, mod_ref, o_ref, *, gate_row, ctx_rows):
    tm = a_ref.shape[0]
    acc = jnp.dot(a_ref[...], w_ref[...], preferred_element_type=F32)
    row = pl.program_id(0) * tm + lax.broadcasted_iota(jnp.int32, (tm, 1), 0)
    gate = jnp.where(row < ctx_rows, mod_ref[0, gate_row:gate_row + 1, :],
                     mod_ref[1, gate_row:gate_row + 1, :])
    o_ref[...] = x_ref[...] + gate * acc


def _matmul_gated_residual(a, w, x, mods, ctx_rows, gate_row):
    m, k = a.shape
    n = w.shape[1]
    tm = _pick(m, (1280, 1024, 768, 512, 256))
    tn = _pick(n, (512, 256, 128))
    return pl.pallas_call(
        functools.partial(_mm_resid_kernel, gate_row=gate_row, ctx_rows=ctx_rows),
        out_shape=jax.ShapeDtypeStruct((m, n), F32),
        grid=(m // tm, n // tn),
        in_specs=[pl.BlockSpec((tm, k), lambda i, j: (i, 0)),
                  pl.BlockSpec((k, tn), lambda i, j: (0, j)),
                  pl.BlockSpec((tm, tn), lambda i, j: (i, j)),
                  pl.BlockSpec((2, 8, tn), lambda i, j: (0, 0, j))],
        out_specs=pl.BlockSpec((tm, tn), lambda i, j: (i, j)),
        compiler_params=_params("parallel", "arbitrary"), name="out_proj",
    )(a, w, x, mods)


def _merge_kernel(h_ref, a_ref, s_ref, wga_ref, wgb_ref, wa_ref, wb_ref, m_ref):
    h = h_ref[...]
    ga = jnp.dot(h, wga_ref[...], preferred_element_type=F32)
    gb = jnp.dot(h, wgb_ref[...], preferred_element_type=F32)
    ya = jnp.dot(a_ref[...], wa_ref[...], preferred_element_type=F32)
    yb = jnp.dot(s_ref[...], wb_ref[...], preferred_element_type=F32)
    m_ref[...] = (jax.nn.sigmoid(ga) * ya + jax.nn.sigmoid(gb) * yb).astype(BF16)


def _merge(h, a, s, w_gates, w_a, w_b):
    m, d = h.shape
    ka = a.shape[1]
    kb = s.shape[1]
    tm = _pick(m, (1280, 1024, 768, 512, 256))
    tn = _pick(d, (256, 128))
    nb = d // tn
    return pl.pallas_call(
        _merge_kernel,
        out_shape=jax.ShapeDtypeStruct((m, d), BF16),
        grid=(m // tm, nb),
        in_specs=[pl.BlockSpec((tm, d), lambda i, j: (i, 0)),
                  pl.BlockSpec((tm, ka), lambda i, j: (i, 0)),
                  pl.BlockSpec((tm, kb), lambda i, j: (i, 0)),
                  pl.BlockSpec((d, tn), lambda i, j: (0, j)),
                  pl.BlockSpec((d, tn), lambda i, j: (0, j + nb)),
                  pl.BlockSpec((ka, tn), lambda i, j: (0, j)),
                  pl.BlockSpec((kb, tn), lambda i, j: (0, j))],
        out_specs=pl.BlockSpec((tm, tn), lambda i, j: (i, j)),
        compiler_params=_params("parallel", "arbitrary"), name="merge",
    )(h, a, s, w_gates, w_gates, w_a, w_b)


def _sgu_kernel(u_ref, vs_ref, lng_ref, lnb_ref, ws_ref, bs_ref, s_ref):
    tm = u_ref.shape[0]
    u = jax.nn.gelu(u_ref[...])
    v = jax.nn.gelu(vs_ref[...])
    vc = v - jnp.mean(v, axis=-1, keepdims=True)
    vn = vc * lax.rsqrt(jnp.mean(vc * vc, axis=-1, keepdims=True) + EPS)
    vn = (vn * lng_ref[...] + lnb_ref[...]).astype(BF16)
    for ch in range(tm // SG_CHUNK):
        rows = slice(ch * SG_CHUNK, (ch + 1) * SG_CHUNK)
        for g in range(SG_GROUPS):
            cols = slice(g * SG_GROUP_CH, (g + 1) * SG_GROUP_CH)
            mixed = jnp.dot(ws_ref[g], vn[rows, cols], preferred_element_type=F32) + bs_ref[g]
            s_ref[rows, cols] = (u[rows, cols] * mixed).astype(BF16)


def _sgu(p_main, ln_g, ln_b, w_s, b_s_bcast):
    m = p_main.shape[0]
    tm = ROW_BLOCK
    return pl.pallas_call(
        _sgu_kernel,
        out_shape=jax.ShapeDtypeStruct((m, SG_DIM), BF16),
        grid=(m // tm,),
        in_specs=[pl.BlockSpec((tm, SG_DIM), lambda i: (i, MAIN_COL_U)),
                  pl.BlockSpec((tm, SG_DIM), lambda i: (i, MAIN_COL_VS)),
                  pl.BlockSpec((1, SG_DIM), lambda i: (0, 0)),
                  pl.BlockSpec((1, SG_DIM), lambda i: (0, 0)),
                  pl.BlockSpec((SG_GROUPS, SG_CHUNK, SG_CHUNK), lambda i: (0, 0, 0)),
                  pl.BlockSpec((SG_GROUPS, SG_CHUNK, SG_GROUP_CH), lambda i: (0, 0, 0))],
        out_specs=pl.BlockSpec((tm, SG_DIM), lambda i: (i, 0)),
        compiler_params=_params("parallel"), name="sgu",
    )(p_main, p_main, ln_g.reshape(1, SG_DIM), ln_b.reshape(1, SG_DIM), w_s, b_s_bcast)


def _split_bf16(x):
    hi = x.astype(BF16)
    lo = (x - hi.astype(F32)).astype(BF16)
    return hi, lo


def _gla_prep_kernel(q_ref, k_ref, v_ref, dl_ref, wup_ref, bdec_ref,
                     qmf_ref, kmf_ref, qef_ref, qmb_ref, kmb_ref, qeb_ref,
                     vb_ref, utf_ref, utb_ref, dtot_ref):
    c = q_ref.shape[0]
    half = c // 2
    z = jnp.dot(dl_ref[...], wup_ref[...], precision=lax.Precision.HIGHEST,
                preferred_element_type=F32) + bdec_ref[...]
    g = -(jnp.maximum(-z, 0.0) + jnp.log1p(jnp.exp(-jnp.abs(z)))) * (1.0 / GLA_TAU)
    ri = lax.broadcasted_iota(jnp.int32, (c, c), 0)
    ci = lax.broadcasted_iota(jnp.int32, (c, c), 1)
    lane = lax.broadcasted_iota(jnp.int32, (1, HEAD_PAIR), 1)
    q = q_ref[...] * (GLA_DK ** -0.5)
    k = k_ref[...]
    v = v_ref[...]
    vb = v.astype(BF16)
    vb_ref[...] = vb

    def direction(gd, tri, mid_row, end_row, qm_ref, km_ref, qe_ref, ut_ref):
        hi, lo = _split_bf16(gd)
        tri = tri.astype(BF16)
        bc = (jnp.dot(tri, hi, preferred_element_type=F32)
              + jnp.dot(tri, lo, preferred_element_type=F32))
        rho = bc[mid_row:mid_row + 1, :]
        tot = bc[end_row:end_row + 1, :]
        e1 = bc - rho
        qm_ref[...] = (q * jnp.exp(jnp.minimum(e1, EXP_CLAMP))).astype(BF16)
        km_ref[...] = (k * jnp.exp(jnp.minimum(-e1, EXP_CLAMP))).astype(BF16)
        qe_ref[...] = (q * jnp.exp(bc)).astype(BF16)
        k2 = (k * jnp.exp(tot - bc)).astype(BF16)
        for p in range(GLA_HEADS // 2):
            kp = k2[:, p * HEAD_PAIR:(p + 1) * HEAD_PAIR]
            ut = jnp.zeros((GLA_DV, HEAD_PAIR), F32)
            for hh in range(2):
                h = 2 * p + hh
                keep = (lane < GLA_DK) if hh == 0 else (lane >= GLA_DK)
                kh = jnp.where(keep, kp, jnp.zeros_like(kp))
                ut = ut + lax.dot_general(vb[:, h * GLA_DV:(h + 1) * GLA_DV], kh, TN_DIMS,
                                          preferred_element_type=F32)
            ut_ref[0, :, p * HEAD_PAIR:(p + 1) * HEAD_PAIR] = ut
        return jnp.exp(tot)

    df = direction(g[:, :QK_DIM], ci <= ri, half - 1, c - 1, qmf_ref, kmf_ref, qef_ref, utf_ref)
    db = direction(g[:, QK_DIM:], ci >= ri, half, 0, qmb_ref, kmb_ref, qeb_ref, utb_ref)
    dtot_ref[0] = jnp.concatenate([df, db, jnp.zeros((6, QK_DIM), F32)], axis=0)


def _gla_prep(p_main, dlow, wup, bdec):
    qkv = p_main
    m = qkv.shape[0]
    c = ROW_BLOCK
    nb = m // c
    qk_spec = pl.BlockSpec((c, QK_DIM), lambda i: (i, 0))
    bf_qk = jax.ShapeDtypeStruct((m, QK_DIM), BF16)
    ut_shape = jax.ShapeDtypeStruct((nb, GLA_DV, QK_DIM), F32)
    ut_spec = pl.BlockSpec((1, GLA_DV, QK_DIM), lambda i: (i, 0, 0))
    return pl.pallas_call(
        _gla_prep_kernel,
        out_shape=(bf_qk,) * 6 + (jax.ShapeDtypeStruct((m, V_DIM), BF16), ut_shape, ut_shape,
                                  jax.ShapeDtypeStruct((nb, 8, QK_DIM), F32)),
        grid=(nb,),
        in_specs=[pl.BlockSpec((c, QK_DIM), lambda i: (i, 0)),
                  pl.BlockSpec((c, QK_DIM), lambda i: (i, 1)),
                  pl.BlockSpec((c, V_DIM), lambda i: (i, 1)),
                  pl.BlockSpec((c, LANES), lambda i: (i, 0)),
                  pl.BlockSpec((LANES, 2 * QK_DIM), lambda i: (0, 0)),
                  pl.BlockSpec((1, 2 * QK_DIM), lambda i: (0, 0))],
        out_specs=(qk_spec,) * 6 + (pl.BlockSpec((c, V_DIM), lambda i: (i, 0)), ut_spec, ut_spec,
                                    pl.BlockSpec((1, 8, QK_DIM), lambda i: (i, 0, 0))),
        compiler_params=_params("parallel"), name="gla_prep",
    )(qkv, qkv, qkv, dlow, wup, bdec)


def _gla_scan_kernel(utf_ref, dtf_ref, utb_ref, dtb_ref, stf_ref, stb_ref, sf_ref, sb_ref):
    @pl.when(pl.program_id(0) == 0)
    def _():
        sf_ref[...] = jnp.zeros_like(sf_ref)
        sb_ref[...] = jnp.zeros_like(sb_ref)

    sf = sf_ref[...]
    stf_ref[0] = sf.astype(BF16)
    sf_ref[...] = dtf_ref[0, 0:1, :] * sf + utf_ref[0]
    sb = sb_ref[...]
    stb_ref[0] = sb.astype(BF16)
    sb_ref[...] = dtb_ref[0, 1:2, :] * sb + utb_ref[0]


def _gla_scan(utf, utb, dtot, ctx_blocks):
    nb = utf.shape[0]

    def bwd(t):
        return jnp.where(t < ctx_blocks, ctx_blocks - 1 - t, nb + ctx_blocks - 1 - t)

    st_shape = jax.ShapeDtypeStruct((nb, GLA_DV, QK_DIM), BF16)
    blk = (1, GLA_DV, QK_DIM)
    return pl.pallas_call(
        _gla_scan_kernel,
        out_shape=(st_shape, st_shape),
        grid=(nb,),
        in_specs=[pl.BlockSpec(blk, lambda t: (t, 0, 0)),
                  pl.BlockSpec((1, 8, QK_DIM), lambda t: (t, 0, 0)),
                  pl.BlockSpec(blk, lambda t: (bwd(t), 0, 0)),
                  pl.BlockSpec((1, 8, QK_DIM), lambda t: (bwd(t), 0, 0))],
        out_specs=(pl.BlockSpec(blk, lambda t: (t, 0, 0)),
                   pl.BlockSpec(blk, lambda t: (bwd(t), 0, 0))),
        scratch_shapes=[pltpu.VMEM((GLA_DV, QK_DIM), F32), pltpu.VMEM((GLA_DV, QK_DIM), F32)],
        compiler_params=_params("arbitrary"), name="gla_scan",
    )(utf, dtot, utb, dtot)


def _gla_out_kernel(qmf_ref, kmf_ref, qef_ref, qmb_ref, kmb_ref, qeb_ref, vb_ref,
                    stf_ref, stb_ref, r_ref, gain_ref, a_ref):
    c = qmf_ref.shape[0]
    ri = lax.broadcasted_iota(jnp.int32, (c, c), 0)
    ci = lax.broadcasted_iota(jnp.int32, (c, c), 1)
    lower = ci <= ri
    upper = ci >= ri
    lane = lax.broadcasted_iota(jnp.int32, (1, HEAD_PAIR), 1)
    for p in range(GLA_HEADS // 2):
        pair = slice(p * HEAD_PAIR, (p + 1) * HEAD_PAIR)
        qmf, kmf, qef = qmf_ref[:, pair], kmf_ref[:, pair], qef_ref[:, pair]
        qmb, kmb, qeb = qmb_ref[:, pair], kmb_ref[:, pair], qeb_ref[:, pair]
        stf = stf_ref[0, :, pair]
        stb = stb_ref[0, :, pair]
        for hh in range(2):
            h = 2 * p + hh
            keep = (lane < GLA_DK) if hh == 0 else (lane >= GLA_DK)
            zero = jnp.zeros_like(kmf)
            af = lax.dot_general(qmf, jnp.where(keep, kmf, zero), NT_DIMS, preferred_element_type=F32)
            ab = lax.dot_general(qmb, jnp.where(keep, kmb, zero), NT_DIMS, preferred_element_type=F32)
            att = (jnp.where(lower, af, 0.0) + jnp.where(upper, ab, 0.0)).astype(BF16)
            cols = slice(h * GLA_DV, (h + 1) * GLA_DV)
            o = jnp.dot(att, vb_ref[:, cols], preferred_element_type=F32)
            o = o + lax.dot_general(jnp.where(keep, qef, zero), stf, NT_DIMS, preferred_element_type=F32)
            o = o + lax.dot_general(jnp.where(keep, qeb, zero), stb, NT_DIMS, preferred_element_type=F32)
            o = o * lax.rsqrt(jnp.mean(o * o, axis=-1, keepdims=True) + EPS)
            r = r_ref[:, cols]
            a_ref[:, cols] = (o * gain_ref[:, cols] * (r * jax.nn.sigmoid(r))).astype(BF16)


def _gla_out(prep, stf, stb, p_main, gain):
    qmf, kmf, qef, qmb, kmb, qeb, vb = prep
    m = vb.shape[0]
    c = ROW_BLOCK
    qk_spec = pl.BlockSpec((c, QK_DIM), lambda i: (i, 0))
    st_spec = pl.BlockSpec((1, GLA_DV, QK_DIM), lambda i: (i, 0, 0))
    v_spec = pl.BlockSpec((c, V_DIM), lambda i: (i, 0))
    return pl.pallas_call(
        _gla_out_kernel,
        out_shape=jax.ShapeDtypeStruct((m, V_DIM), BF16),
        grid=(m // c,),
        in_specs=[qk_spec] * 6 + [v_spec, st_spec, st_spec,
                                  pl.BlockSpec((c, V_DIM), lambda i: (i, MAIN_COL_R)),
                                  pl.BlockSpec((1, V_DIM), lambda i: (0, 0))],
        out_specs=v_spec,
        compiler_params=_params("parallel"), name="gla_out",
    )(qmf, kmf, qef, qmb, kmb, qeb, vb, stf, stb, p_main, gain.reshape(1, V_DIM))


ONE_F32_BITS = 0x3F800000
SEARCH_STEPS = 31


def _route_kernel(logit_ref, g_ref, *, ctx_rows):
    m = logit_ref.shape[0]
    rb = ROW_BLOCK
    lane = lax.broadcasted_iota(jnp.int32, (1, LANES), 1)
    valid = lane < N_EXPERTS

    def softmax_block(i, carry):
        r0 = pl.multiple_of(i * rb, rb)
        lg = jnp.where(valid, logit_ref[pl.ds(r0, rb), :], -jnp.inf)
        ex = jnp.exp(lg - jnp.max(lg, axis=-1, keepdims=True))
        g_ref[pl.ds(r0, rb), :] = ex / jnp.sum(ex, axis=-1, keepdims=True)
        return carry

    lax.fori_loop(0, m // rb, softmax_block, 0)

    def bits_of(r0):
        return lax.bitcast_convert_type(g_ref[pl.ds(r0, rb), :], jnp.int32)

    def count_ge(blk0, nblk, t):
        def body(i, acc):
            r0 = pl.multiple_of((blk0 + i) * rb, rb)
            hit = jnp.where(bits_of(r0) >= t, 1, 0)
            return acc + jnp.sum(hit.reshape(rb // 8, 8, LANES), axis=0)

        acc = lax.fori_loop(0, nblk, body, jnp.zeros((8, LANES), jnp.int32))
        return jnp.sum(acc, axis=0, keepdims=True)

    def route_set(blk0, nblk):
        cap = EC_CAPACITY * nblk * rb // N_EXPERTS

        def halve(_, lohi):
            lo, hi = lohi
            mid = lo + ((hi - lo) >> 1)
            ok = count_ge(blk0, nblk, mid) >= cap
            return jnp.where(ok, mid, lo), jnp.where(ok, hi, mid)

        thr, _ = lax.fori_loop(0, SEARCH_STEPS, halve,
                               (jnp.zeros((1, LANES), jnp.int32),
                                jnp.full((1, LANES), ONE_F32_BITS + 1, jnp.int32)))
        need = (cap - count_ge(blk0, nblk, thr + 1)).astype(F32)
        ri = lax.broadcasted_iota(jnp.int32, (rb, rb), 0)
        ci = lax.broadcasted_iota(jnp.int32, (rb, rb), 1)
        earlier = jnp.where(ci < ri, 1.0, 0.0).astype(BF16)

        def finalize(i, seen):
            r0 = pl.multiple_of((blk0 + i) * rb, rb)
            aff = g_ref[pl.ds(r0, rb), :]
            bits = lax.bitcast_convert_type(aff, jnp.int32)
            tie = bits == thr
            tie_f = jnp.where(tie, 1.0, 0.0)
            rank = seen + jnp.dot(earlier, tie_f.astype(BF16), preferred_element_type=F32)
            g_ref[pl.ds(r0, rb), :] = jnp.where(tie, jnp.where(rank < need, aff, 0.0),
                                                jnp.where(bits > thr, aff, 0.0))
            return seen + jnp.sum(tie_f, axis=0, keepdims=True)

        lax.fori_loop(0, nblk, finalize, jnp.zeros((1, LANES), F32))

    route_set(0, ctx_rows // rb)
    route_set(ctx_rows // rb, (m - ctx_rows) // rb)


def _route(logits, ctx_rows):
    m = logits.shape[0]
    return pl.pallas_call(
        functools.partial(_route_kernel, ctx_rows=ctx_rows),
        out_shape=jax.ShapeDtypeStruct((m, LANES), F32),
        grid=(1,),
        in_specs=[pl.BlockSpec((m, LANES), lambda i: (0, 0))],
        out_specs=pl.BlockSpec((m, LANES), lambda i: (0, 0)),
        compiler_params=_params("arbitrary"), name="route",
    )(logits)


def _moe_hidden_kernel(h_ref, w1_ref, w3_ref, g_ref, o_ref):
    e = pl.program_id(1)
    lane = lax.broadcasted_iota(jnp.int32, (1, LANES), 1)
    gate = jnp.sum(jnp.where(lane == e, g_ref[...], 0.0), axis=-1, keepdims=True)
    h = h_ref[...]
    h1 = jnp.dot(h, w1_ref[0], preferred_element_type=F32)
    h3 = jnp.dot(h, w3_ref[0], preferred_element_type=F32)
    o_ref[...] = (h1 * jax.nn.sigmoid(h1) * h3 * gate).astype(BF16)


def _moe_hidden(h, gates, w1, w3):
    m, d = h.shape
    e, _, f = w1.shape
    tm = _pick(m, (1280, 1024, 768, 512, 256))
    return pl.pallas_call(
        _moe_hidden_kernel,
        out_shape=jax.ShapeDtypeStruct((m, e * f), BF16),
        grid=(m // tm, e),
        in_specs=[pl.BlockSpec((tm, d), lambda i, j: (i, 0)),
                  pl.BlockSpec((1, d, f), lambda i, j: (j, 0, 0)),
                  pl.BlockSpec((1, d, f), lambda i, j: (j, 0, 0)),
                  pl.BlockSpec((tm, LANES), lambda i, j: (i, 0))],
        out_specs=pl.BlockSpec((tm, f), lambda i, j: (i, j)),
        compiler_params=_params("parallel", "arbitrary"), name="moe_hidden",
    )(h, w1, w3, gates)


def kernel(x, c, ctx, c_ctx, ada_w, ada_b, norm1_g, norm2_g, w_in, w_dec_up, b_dec, gla_norm_g,
           sg_ln_g, sg_ln_b, sg_w, sg_b, w_branch_a, w_branch_b, w_out, w_router, w_exp1, w_exp3,
           w_exp2, final_g):
    batch, seq, d = x.shape
    ctx_rows = ctx.shape[1]
    depth = ada_w.shape[0]
    assert batch == 1 and seq % ROW_BLOCK == 0 and ctx_rows % ROW_BLOCK == 0

    xa = jnp.concatenate([ctx[0], x[0]], axis=0)
    cond2 = jnp.stack([c_ctx, c[0]], axis=1)
    mods_all = _adaln(cond2, ada_w, ada_b).reshape(depth, 2, 6, d)
    mods_all = jnp.pad(mods_all, ((0, 0), (0, 0), (0, 2), (0, 0)))

    for l in range(depth):
        mods = mods_all[l]
        wl = w_in[l]
        w_main = jnp.concatenate([wl[:, :OFF_DF], wl[:, SCAN_COLS:OFF_GA]], axis=1).astype(BF16)
        w_dlow = jnp.pad(wl[:, OFF_DF:SCAN_COLS], ((0, 0), (0, LANES - 2 * DECAY_RANK))).astype(BF16)
        w_gates = wl[:, OFF_GA:].astype(BF16)
        wup = jnp.zeros((LANES, 2 * QK_DIM), F32)
        wup = wup.at[:DECAY_RANK, :QK_DIM].set(w_dec_up[l, 0])
        wup = wup.at[DECAY_RANK:2 * DECAY_RANK, QK_DIM:].set(w_dec_up[l, 1])
        bdec = b_dec[l].reshape(1, 2 * QK_DIM)
        b_s_bcast = jnp.broadcast_to(sg_b[l][:, :, None], (SG_GROUPS, SG_CHUNK, SG_GROUP_CH))
        w_r = jnp.pad(w_router[l], ((0, 0), (0, LANES - N_EXPERTS)))

        h = _prenorm(xa, norm1_g[l], mods, ctx_rows, shift_row=0, scale_row=1)
        p_main = _matmul(h, w_main)
        dlow = _matmul(h, w_dlow)

        prep = _gla_prep(p_main, dlow, wup, bdec)
        stf, stb = _gla_scan(prep[7], prep[8], prep[9], ctx_rows // ROW_BLOCK)
        a = _gla_out(prep[:7], stf, stb, p_main, gla_norm_g[l])
        s = _sgu(p_main, sg_ln_g[l], sg_ln_b[l], sg_w[l].astype(BF16), b_s_bcast)
        mix = _merge(h, a, s, w_gates, w_branch_a[l].astype(BF16), w_branch_b[l].astype(BF16))
        xa = _matmul_gated_residual(mix, w_out[l].astype(BF16), xa, mods, ctx_rows, gate_row=2)

        h2, logits = _prenorm(xa, norm2_g[l], mods, ctx_rows, shift_row=3, scale_row=4, w_router=w_r)
        gates = _route(logits, ctx_rows)
        hid = _moe_hidden(h2, gates, w_exp1[l].astype(BF16), w_exp3[l].astype(BF16))
        w2_all = w_exp2[l].reshape(N_EXPERTS * EXPERT_FF, d).astype(BF16)
        xa = _matmul_gated_residual(hid, w2_all, xa, mods, ctx_rows, gate_row=5)

    return _final_norm(xa, final_g, ctx_rows)[None]
```

```python
import functools

import jax
import jax.numpy as jnp
from jax import lax
from jax.experimental import pallas as pl
from jax.experimental.pallas import tpu as pltpu

EPS = 1e-6
GLA_HEADS = 8
GLA_DK = 64
GLA_DV = 128
DECAY_RANK = 16
GLA_TAU = 16.0
SG_GROUPS = 8
SG_GROUP_CH = 128
SG_CHUNK = 128
N_EXPERTS = 16
EXPERT_FF = 256
EC_CAPACITY = 2

QK_DIM = GLA_HEADS * GLA_DK
V_DIM = GLA_HEADS * GLA_DV
SG_DIM = SG_GROUPS * SG_GROUP_CH
OFF_K = QK_DIM
OFF_V = 2 * QK_DIM
OFF_DF = OFF_V + V_DIM
OFF_DB = OFF_DF + DECAY_RANK
SCAN_COLS = OFF_DB + DECAY_RANK
OFF_U = SCAN_COLS + V_DIM
OFF_VS = OFF_U + SG_DIM
OFF_GA = OFF_VS + SG_DIM

MAIN_COL_R = OFF_DF // V_DIM
MAIN_COL_U = MAIN_COL_R + 1
MAIN_COL_VS = MAIN_COL_R + 2

LANES = 128
ROW_BLOCK = 256
HEAD_PAIR = 2 * GLA_DK
EXP_CLAMP = 80.0
VMEM_LIMIT_BYTES = 52 * 1024 * 1024

F32 = jnp.float32
BF16 = jnp.bfloat16
NT_DIMS = (((1,), (1,)), ((), ()))
TN_DIMS = (((0,), (0,)), ((), ()))


def _pick(n, candidates):
    for c in candidates:
        if n % c == 0:
            return c
    raise ValueError(f"no tile of {candidates} divides {n}")


def _params(*semantics):
    return pltpu.CompilerParams(dimension_semantics=semantics,
                                vmem_limit_bytes=VMEM_LIMIT_BYTES)


def _stream_of_block(tm, ctx_rows):
    return lambda i: (i * tm >= ctx_rows).astype(jnp.int32)


def _adaln_kernel(cond_ref, w_ref, b_ref, out_ref, *, rows):
    d = cond_ref.shape[0]
    tn = out_ref.shape[-1]

    def body(i, acc):
        r0 = pl.multiple_of(i * rows, rows)
        cond = cond_ref[pl.ds(r0, rows), :]
        s = cond * jax.nn.sigmoid(cond)
        w = w_ref[0, pl.ds(r0, rows), :]
        a0 = jnp.sum(w * s[:, 0:1], axis=0, keepdims=True)
        a1 = jnp.sum(w * s[:, 1:2], axis=0, keepdims=True)
        return acc + jnp.concatenate([a0, a1], axis=0)

    acc = lax.fori_loop(0, d // rows, body, jnp.zeros((2, tn), F32))
    out_ref[0] = acc + b_ref[0]


def _adaln(cond2, ada_w, ada_b):
    nl, d, n6 = ada_w.shape
    tn = _pick(n6, (1024, 512, 256, 128))
    return pl.pallas_call(
        functools.partial(_adaln_kernel, rows=_pick(d, (256, 128, 8))),
        out_shape=jax.ShapeDtypeStruct((nl, 2, n6), F32),
        grid=(nl, n6 // tn),
        in_specs=[pl.BlockSpec((d, 2), lambda l, j: (0, 0)),
                  pl.BlockSpec((1, d, tn), lambda l, j: (l, 0, j)),
                  pl.BlockSpec((1, 1, tn), lambda l, j: (l, 0, j))],
        out_specs=pl.BlockSpec((1, 2, tn), lambda l, j: (l, 0, j)),
        compiler_params=_params("parallel", "parallel"),
        name="adaln",
    )(cond2, ada_w, ada_b.reshape(nl, 1, n6))


def _modulated_norm(x, g, mod, shift_row, scale_row):
    ms = jnp.mean(x * x, axis=-1, keepdims=True)
    y = x * lax.rsqrt(ms + EPS) * g
    return y * (1.0 + mod[scale_row:scale_row + 1, :]) + mod[shift_row:shift_row + 1, :]


def _prenorm_kernel(x_ref, g_ref, mod_ref, h_ref, *, shift_row, scale_row):
    h = _modulated_norm(x_ref[...], g_ref[...], mod_ref[0], shift_row, scale_row)
    h_ref[...] = h.astype(BF16)


def _prenorm_router_kernel(x_ref, g_ref, mod_ref, wr_hi_ref, wr_lo_ref, h_ref, logit_ref, *,
                           shift_row, scale_row):
    h = _modulated_norm(x_ref[...], g_ref[...], mod_ref[0], shift_row, scale_row)
    h_hi, h_lo = _split_bf16(h)
    h_ref[...] = h_hi
    logit_ref[...] = (jnp.dot(h_hi, wr_hi_ref[...], preferred_element_type=F32)
                      + jnp.dot(h_lo, wr_hi_ref[...], preferred_element_type=F32)
                      + jnp.dot(h_hi, wr_lo_ref[...], preferred_element_type=F32))


def _prenorm(x, g, mods, ctx_rows, shift_row, scale_row, w_router=None):
    m, d = x.shape
    tm = ROW_BLOCK
    stream = _stream_of_block(tm, ctx_rows)
    in_specs = [pl.BlockSpec((tm, d), lambda i: (i, 0)),
                pl.BlockSpec((1, d), lambda i: (0, 0)),
                pl.BlockSpec((1, 8, d), lambda i: (stream(i), 0, 0))]
    h_spec = pl.BlockSpec((tm, d), lambda i: (i, 0))
    h_shape = jax.ShapeDtypeStruct((m, d), BF16)
    if w_router is None:
        return pl.pallas_call(
            functools.partial(_prenorm_kernel, shift_row=shift_row, scale_row=scale_row),
            out_shape=h_shape, grid=(m // tm,), in_specs=in_specs, out_specs=h_spec,
            compiler_params=_params("parallel"), name="prenorm",
        )(x, g.reshape(1, d), mods)
    return pl.pallas_call(
        functools.partial(_prenorm_router_kernel, shift_row=shift_row, scale_row=scale_row),
        out_shape=(h_shape, jax.ShapeDtypeStruct((m, LANES), F32)),
        grid=(m // tm,),
        in_specs=in_specs + [pl.BlockSpec((d, LANES), lambda i: (0, 0))] * 2,
        out_specs=(h_spec, pl.BlockSpec((tm, LANES), lambda i: (i, 0))),
        compiler_params=_params("parallel"), name="prenorm_router",
    )(x, g.reshape(1, d), mods, *_split_bf16(w_router))


def _final_norm_kernel(x_ref, g_ref, o_ref):
    x = x_ref[...]
    ms = jnp.mean(x * x, axis=-1, keepdims=True)
    o_ref[...] = x * lax.rsqrt(ms + EPS) * g_ref[...]


def _final_norm(x_all, g, ctx_rows):
    m, d = x_all.shape
    tm = ROW_BLOCK
    skip = ctx_rows // tm
    return pl.pallas_call(
        _final_norm_kernel,
        out_shape=jax.ShapeDtypeStruct((m - ctx_rows, d), F32),
        grid=((m - ctx_rows) // tm,),
        in_specs=[pl.BlockSpec((tm, d), lambda i: (i + skip, 0)),
                  pl.BlockSpec((1, d), lambda i: (0, 0))],
        out_specs=pl.BlockSpec((tm, d), lambda i: (i, 0)),
        compiler_params=_params("parallel"), name="final_norm",
    )(x_all, g.reshape(1, d))


def _mm_kernel(a_ref, w_ref, o_ref):
    o_ref[...] = jnp.dot(a_ref[...], w_ref[...], preferred_element_type=F32).astype(o_ref.dtype)


def _matmul(a, w, out_dtype=F32):
    m, k = a.shape
    n = w.shape[1]
    tm = _pick(m, (1280, 1024, 768, 512, 256))
    tn = _pick(n, (512, 256, 128))
    return pl.pallas_call(
        _mm_kernel,
        out_shape=jax.ShapeDtypeStruct((m, n), out_dtype),
        grid=(m // tm, n // tn),
        in_specs=[pl.BlockSpec((tm, k), lambda i, j: (i, 0)),
                  pl.BlockSpec((k, tn), lambda i, j: (0, j))],
        out_specs=pl.BlockSpec((tm, tn), lambda i, j: (i, j)),
        compiler_params=_params("parallel", "arbitrary"), name="matmul",
    )(a, w)


def _mm_resid_kernel(a_ref, w_ref, x_ref, mod_ref, o_ref, *, gate_row, ctx_rows):
    tm = a_ref.shape[0]
    acc = jnp.dot(a_ref[...], w_ref[...], preferred_element_type=F32)
    row = pl.program_id(0) * tm + lax.broadcasted_iota(jnp.int32, (tm, 1), 0)
    gate = jnp.where(row < ctx_rows, mod_ref[0, gate_row:gate_row + 1, :],
                     mod_ref[1, gate_row:gate_row + 1, :])
    o_ref[...] = x_ref[...] + gate * acc


def _matmul_gated_residual(a, w, x, mods, ctx_rows, gate_row):
    m, k = a.shape
    n = w.shape[1]
    tm = _pick(m, (1280, 1024, 768, 512, 256))
    tn = _pick(n, (512, 256, 128))
    return pl.pallas_call(
        functools.partial(_mm_resid_kernel, gate_row=gate_row, ctx_rows=ctx_rows),
        out_shape=jax.ShapeDtypeStruct((m, n), F32),
        grid=(m // tm, n // tn),
        in_specs=[pl.BlockSpec((tm, k), lambda i, j: (i, 0)),
                  pl.BlockSpec((k, tn), lambda i, j: (0, j)),
                  pl.BlockSpec((tm, tn), lambda i, j: (i, j)),
                  pl.BlockSpec((2, 8, tn), lambda i, j: (0, 0, j))],
        out_specs=pl.BlockSpec((tm, tn), lambda i, j: (i, j)),
        compiler_params=_params("parallel", "arbitrary"), name="out_proj",
    )(a, w, x, mods)


def _merge_kernel(h_ref, a_ref, s_ref, wga_ref, wgb_ref, wa_ref, wb_ref, m_ref):
    h = h_ref[...]
    ga = jnp.dot(h, wga_ref[...], preferred_element_type=F32)
    gb = jnp.dot(h, wgb_ref[...], preferred_element_type=F32)
    ya = jnp.dot(a_ref[...], wa_ref[...], preferred_element_type=F32)
    yb = jnp.dot(s_ref[...], wb_ref[...], preferred_element_type=F32)
    m_ref[...] = (jax.nn.sigmoid(ga) * ya + jax.nn.sigmoid(gb) * yb).astype(BF16)


def _merge(h, a, s, w_gates, w_a, w_b):
    m, d = h.shape
    ka = a.shape[1]
    kb = s.shape[1]
    tm = _pick(m, (1280, 1024, 768, 512, 256))
    tn = _pick(d, (256, 128))
    nb = d // tn
    return pl.pallas_call(
        _merge_kernel,
        out_shape=jax.ShapeDtypeStruct((m, d), BF16),
        grid=(m // tm, nb),
        in_specs=[pl.BlockSpec((tm, d), lambda i, j: (i, 0)),
                  pl.BlockSpec((tm, ka), lambda i, j: (i, 0)),
                  pl.BlockSpec((tm, kb), lambda i, j: (i, 0)),
                  pl.BlockSpec((d, tn), lambda i, j: (0, j)),
                  pl.BlockSpec((d, tn), lambda i, j: (0, j + nb)),
                  pl.BlockSpec((ka, tn), lambda i, j: (0, j)),
                  pl.BlockSpec((kb, tn), lambda i, j: (0, j))],
        out_specs=pl.BlockSpec((tm, tn), lambda i, j: (i, j)),
        compiler_params=_params("parallel", "arbitrary"), name="merge",
    )(h, a, s, w_gates, w_gates, w_a, w_b)


def _sgu_kernel(u_ref, vs_ref, lng_ref, lnb_ref, ws_ref, bs_ref, s_ref):
    tm = u_ref.shape[0]
    u = jax.nn.gelu(u_ref[...])
    v = jax.nn.gelu(vs_ref[...])
    vc = v - jnp.mean(v, axis=-1, keepdims=True)
    vn = vc * lax.rsqrt(jnp.mean(vc * vc, axis=-1, keepdims=True) + EPS)
    vn = (vn * lng_ref[...] + lnb_ref[...]).astype(BF16)
    for ch in range(tm // SG_CHUNK):
        rows = slice(ch * SG_CHUNK, (ch + 1) * SG_CHUNK)
        for g in range(SG_GROUPS):
            cols = slice(g * SG_GROUP_CH, (g + 1) * SG_GROUP_CH)
            mixed = jnp.dot(ws_ref[g], vn[rows, cols], preferred_element_type=F32) + bs_ref[g]
            s_ref[rows, cols] = (u[rows, cols] * mixed).astype(BF16)


def _sgu(p_main, ln_g, ln_b, w_s, b_s_bcast):
    m = p_main.shape[0]
    tm = ROW_BLOCK
    return pl.pallas_call(
        _sgu_kernel,
        out_shape=jax.ShapeDtypeStruct((m, SG_DIM), BF16),
        grid=(m // tm,),
        in_specs=[pl.BlockSpec((tm, SG_DIM), lambda i: (i, MAIN_COL_U)),
                  pl.BlockSpec((tm, SG_DIM), lambda i: (i, MAIN_COL_VS)),
                  pl.BlockSpec((1, SG_DIM), lambda i: (0, 0)),
                  pl.BlockSpec((1, SG_DIM), lambda i: (0, 0)),
                  pl.BlockSpec((SG_GROUPS, SG_CHUNK, SG_CHUNK), lambda i: (0, 0, 0)),
                  pl.BlockSpec((SG_GROUPS, SG_CHUNK, SG_GROUP_CH), lambda i: (0, 0, 0))],
        out_specs=pl.BlockSpec((tm, SG_DIM), lambda i: (i, 0)),
        compiler_params=_params("parallel"), name="sgu",
    )(p_main, p_main, ln_g.reshape(1, SG_DIM), ln_b.reshape(1, SG_DIM), w_s, b_s_bcast)


def _split_bf16(x):
    hi = x.astype(BF16)
    lo = (x - hi.astype(F32)).astype(BF16)
    return hi, lo


def _gla_prep_kernel(q_ref, k_ref, v_ref, dl_ref, wup_ref, bdec_ref,
                     qmf_ref, kmf_ref, qef_ref, qmb_ref, kmb_ref, qeb_ref,
                     vb_ref, utf_ref, utb_ref, dtot_ref):
    c = q_ref.shape[0]
    half = c // 2
    z = jnp.dot(dl_ref[...], wup_ref[...], precision=lax.Precision.HIGHEST,
                preferred_element_type=F32) + bdec_ref[...]
    g = -(jnp.maximum(-z, 0.0) + jnp.log1p(jnp.exp(-jnp.abs(z)))) * (1.0 / GLA_TAU)
    ri = lax.broadcasted_iota(jnp.int32, (c, c), 0)
    ci = lax.broadcasted_iota(jnp.int32, (c, c), 1)
    lane = lax.broadcasted_iota(jnp.int32, (1, HEAD_PAIR), 1)
    q = q_ref[...] * (GLA_DK ** -0.5)
    k = k_ref[...]
    v = v_ref[...]
    vb = v.astype(BF16)
    vb_ref[...] = vb

    def direction(gd, tri, mid_row, end_row, qm_ref, km_ref, qe_ref, ut_ref):
        hi, lo = _split_bf16(gd)
        tri = tri.astype(BF16)
        bc = (jnp.dot(tri, hi, preferred_element_type=F32)
              + jnp.dot(tri, lo, preferred_element_type=F32))
        rho = bc[mid_row:mid_row + 1, :]
        tot = bc[end_row:end_row + 1, :]
        e1 = bc - rho
        qm_ref[...] = (q * jnp.exp(jnp.minimum(e1, EXP_CLAMP))).astype(BF16)
        km_ref[...] = (k * jnp.exp(jnp.minimum(-e1, EXP_CLAMP))).astype(BF16)
        qe_ref[...] = (q * jnp.exp(bc)).astype(BF16)
        k2 = (k * jnp.exp(tot - bc)).astype(BF16)
        for p in range(GLA_HEADS // 2):
            kp = k2[:, p * HEAD_PAIR:(p + 1) * HEAD_PAIR]
            ut = jnp.zeros((GLA_DV, HEAD_PAIR), F32)
            for hh in range(2):
                h = 2 * p + hh
                keep = (lane < GLA_DK) if hh == 0 else (lane >= GLA_DK)
                kh = jnp.where(keep, kp, jnp.zeros_like(kp))
                ut = ut + lax.dot_general(vb[:, h * GLA_DV:(h + 1) * GLA_DV], kh, TN_DIMS,
                                          preferred_element_type=F32)
            ut_ref[0, :, p * HEAD_PAIR:(p + 1) * HEAD_PAIR] = ut
        return jnp.exp(tot)

    df = direction(g[:, :QK_DIM], ci <= ri, half - 1, c - 1, qmf_ref, kmf_ref, qef_ref, utf_ref)
    db = direction(g[:, QK_DIM:], ci >= ri, half, 0, qmb_ref, kmb_ref, qeb_ref, utb_ref)
    dtot_ref[0] = jnp.concatenate([df, db, jnp.zeros((6, QK_DIM), F32)], axis=0)


def _gla_prep(p_main, dlow, wup, bdec):
    qkv = p_main
    m = qkv.shape[0]
    c = ROW_BLOCK
    nb = m // c
    qk_spec = pl.BlockSpec((c, QK_DIM), lambda i: (i, 0))
    bf_qk = jax.ShapeDtypeStruct((m, QK_DIM), BF16)
    ut_shape = jax.ShapeDtypeStruct((nb, GLA_DV, QK_DIM), F32)
    ut_spec = pl.BlockSpec((1, GLA_DV, QK_DIM), lambda i: (i, 0, 0))
    return pl.pallas_call(
        _gla_prep_kernel,
        out_shape=(bf_qk,) * 6 + (jax.ShapeDtypeStruct((m, V_DIM), BF16), ut_shape, ut_shape,
                                  jax.ShapeDtypeStruct((nb, 8, QK_DIM), F32)),
        grid=(nb,),
        in_specs=[pl.BlockSpec((c, QK_DIM), lambda i: (i, 0)),
                  pl.BlockSpec((c, QK_DIM), lambda i: (i, 1)),
                  pl.BlockSpec((c, V_DIM), lambda i: (i, 1)),
                  pl.BlockSpec((c, LANES), lambda i: (i, 0)),
                  pl.BlockSpec((LANES, 2 * QK_DIM), lambda i: (0, 0)),
                  pl.BlockSpec((1, 2 * QK_DIM), lambda i: (0, 0))],
        out_specs=(qk_spec,) * 6 + (pl.BlockSpec((c, V_DIM), lambda i: (i, 0)), ut_spec, ut_spec,
                                    pl.BlockSpec((1, 8, QK_DIM), lambda i: (i, 0, 0))),
        compiler_params=_params("parallel"), name="gla_prep",
    )(qkv, qkv, qkv, dlow, wup, bdec)


def _gla_scan_kernel(utf_ref, dtf_ref, utb_ref, dtb_ref, stf_ref, stb_ref, sf_ref, sb_ref):
    @pl.when(pl.program_id(0) == 0)
    def _():
        sf_ref[...] = jnp.zeros_like(sf_ref)
        sb_ref[...] = jnp.zeros_like(sb_ref)

    sf = sf_ref[...]
    stf_ref[0] = sf.astype(BF16)
    sf_ref[...] = dtf_ref[0, 0:1, :] * sf + utf_ref[0]
    sb = sb_ref[...]
    stb_ref[0] = sb.astype(BF16)
    sb_ref[...] = dtb_ref[0, 1:2, :] * sb + utb_ref[0]


def _gla_scan(utf, utb, dtot, ctx_blocks):
    nb = utf.shape[0]

    def bwd(t):
        return jnp.where(t < ctx_blocks, ctx_blocks - 1 - t, nb + ctx_blocks - 1 - t)

    st_shape = jax.ShapeDtypeStruct((nb, GLA_DV, QK_DIM), BF16)
    blk = (1, GLA_DV, QK_DIM)
    return pl.pallas_call(
        _gla_scan_kernel,
        out_shape=(st_shape, st_shape),
        grid=(nb,),
        in_specs=[pl.BlockSpec(blk, lambda t: (t, 0, 0)),
                  pl.BlockSpec((1, 8, QK_DIM), lambda t: (t, 0, 0)),
                  pl.BlockSpec(blk, lambda t: (bwd(t), 0, 0)),
                  pl.BlockSpec((1, 8, QK_DIM), lambda t: (bwd(t), 0, 0))],
        out_specs=(pl.BlockSpec(blk, lambda t: (t, 0, 0)),
                   pl.BlockSpec(blk, lambda t: (bwd(t), 0, 0))),
        scratch_shapes=[pltpu.VMEM((GLA_DV, QK_DIM), F32), pltpu.VMEM((GLA_DV, QK_DIM), F32)],
        compiler_params=_params("arbitrary"), name="gla_scan",
    )(utf, dtot, utb, dtot)


def _gla_out_kernel(qmf_ref, kmf_ref, qef_ref, qmb_ref, kmb_ref, qeb_ref, vb_ref,
                    stf_ref, stb_ref, r_ref, gain_ref, a_ref):
    c = qmf_ref.shape[0]
    ri = lax.broadcasted_iota(jnp.int32, (c, c), 0)
    ci = lax.broadcasted_iota(jnp.int32, (c, c), 1)
    lower = ci <= ri
    upper = ci >= ri
    lane = lax.broadcasted_iota(jnp.int32, (1, HEAD_PAIR), 1)
    for p in range(GLA_HEADS // 2):
        pair = slice(p * HEAD_PAIR, (p + 1) * HEAD_PAIR)
        qmf, kmf, qef = qmf_ref[:, pair], kmf_ref[:, pair], qef_ref[:, pair]
        qmb, kmb, qeb = qmb_ref[:, pair], kmb_ref[:, pair], qeb_ref[:, pair]
        stf = stf_ref[0, :, pair]
        stb = stb_ref[0, :, pair]
        for hh in range(2):
            h = 2 * p + hh
            keep = (lane < GLA_DK) if hh == 0 else (lane >= GLA_DK)
            zero = jnp.zeros_like(kmf)
            af = lax.dot_general(qmf, jnp.where(keep, kmf, zero), NT_DIMS, preferred_element_type=F32)
            ab = lax.dot_general(qmb, jnp.where(keep, kmb, zero), NT_DIMS, preferred_element_type=F32)
            att = (jnp.where(lower, af, 0.0) + jnp.where(upper, ab, 0.0)).astype(BF16)
            cols = slice(h * GLA_DV, (h + 1) * GLA_DV)
            o = jnp.dot(att, vb_ref[:, cols], preferred_element_type=F32)
            o = o + lax.dot_general(jnp.where(keep, qef, zero), stf, NT_DIMS, preferred_element_type=F32)
            o = o + lax.dot_general(jnp.where(keep, qeb, zero), stb, NT_DIMS, preferred_element_type=F32)
            o = o * lax.rsqrt(jnp.mean(o * o, axis=-1, keepdims=True) + EPS)
            r = r_ref[:, cols]
            a_ref[:, cols] = (o * gain_ref[:, cols] * (r * jax.nn.sigmoid(r))).astype(BF16)


def _gla_out(prep, stf, stb, p_main, gain):
    qmf, kmf, qef, qmb, kmb, qeb, vb = prep
    m = vb.shape[0]
    c = ROW_BLOCK
    qk_spec = pl.BlockSpec((c, QK_DIM), lambda i: (i, 0))
    st_spec = pl.BlockSpec((1, GLA_DV, QK_DIM), lambda i: (i, 0, 0))
    v_spec = pl.BlockSpec((c, V_DIM), lambda i: (i, 0))
    return pl.pallas_call(
        _gla_out_kernel,
        out_shape=jax.ShapeDtypeStruct((m, V_DIM), BF16),
        grid=(m // c,),
        in_specs=[qk_spec] * 6 + [v_spec, st_spec, st_spec,
                                  pl.BlockSpec((c, V_DIM), lambda i: (i, MAIN_COL_R)),
                                  pl.BlockSpec((1, V_DIM), lambda i: (0, 0))],
        out_specs=v_spec,
        compiler_params=_params("parallel"), name="gla_out",
    )(qmf, kmf, qef, qmb, kmb, qeb, vb, stf, stb, p_main, gain.reshape(1, V_DIM))


ONE_F32_BITS = 0x3F800000
SEARCH_STEPS = 31


def _route_kernel(logit_ref, g_ref, *, ctx_rows):
    m = logit_ref.shape[0]
    rb = ROW_BLOCK
    lane = lax.broadcasted_iota(jnp.int32, (1, LANES), 1)
    valid = lane < N_EXPERTS

    def softmax_block(i, carry):
        r0 = pl.multiple_of(i * rb, rb)
        lg = jnp.where(valid, logit_ref[pl.ds(r0, rb), :], -jnp.inf)
        ex = jnp.exp(lg - jnp.max(lg, axis=-1, keepdims=True))
        g_ref[pl.ds(r0, rb), :] = ex / jnp.sum(ex, axis=-1, keepdims=True)
        return carry

    lax.fori_loop(0, m // rb, softmax_block, 0)

    def bits_of(r0):
        return lax.bitcast_convert_type(g_ref[pl.ds(r0, rb), :], jnp.int32)

    def count_ge(blk0, nblk, t):
        def body(i, acc):
            r0 = pl.multiple_of((blk0 + i) * rb, rb)
            hit = jnp.where(bits_of(r0) >= t, 1, 0)
            return acc + jnp.sum(hit.reshape(rb // 8, 8, LANES), axis=0)

        acc = lax.fori_loop(0, nblk, body, jnp.zeros((8, LANES), jnp.int32))
        return jnp.sum(acc, axis=0, keepdims=True)

    def route_set(blk0, nblk):
        cap = EC_CAPACITY * nblk * rb // N_EXPERTS

        def halve(_, lohi):
            lo, hi = lohi
            mid = lo + ((hi - lo) >> 1)
            ok = count_ge(blk0, nblk, mid) >= cap
            return jnp.where(ok, mid, lo), jnp.where(ok, hi, mid)

        thr, _ = lax.fori_loop(0, SEARCH_STEPS, halve,
                               (jnp.zeros((1, LANES), jnp.int32),
                                jnp.full((1, LANES), ONE_F32_BITS + 1, jnp.int32)))
        need = (cap - count_ge(blk0, nblk, thr + 1)).astype(F32)
        ri = lax.broadcasted_iota(jnp.int32, (rb, rb), 0)
        ci = lax.broadcasted_iota(jnp.int32, (rb, rb), 1)
        earlier = jnp.where(ci < ri, 1.0, 0.0).astype(BF16)

        def finalize(i, seen):
            r0 = pl.multiple_of((blk0 + i) * rb, rb)
            aff = g_ref[pl.ds(r0, rb), :]
            bits = lax.bitcast_convert_type(aff, jnp.int32)
            tie = bits == thr
            tie_f = jnp.where(tie, 1.0, 0.0)
            rank = seen + jnp.dot(earlier, tie_f.astype(BF16), preferred_element_type=F32)
            g_ref[pl.ds(r0, rb), :] = jnp.where(tie, jnp.where(rank < need, aff, 0.0),
                                                jnp.where(bits > thr, aff, 0.0))
            return seen + jnp.sum(tie_f, axis=0, keepdims=True)

        lax.fori_loop(0, nblk, finalize, jnp.zeros((1, LANES), F32))

    route_set(0, ctx_rows // rb)
    route_set(ctx_rows // rb, (m - ctx_rows) // rb)


def _route(logits, ctx_rows):
    m = logits.shape[0]
    return pl.pallas_call(
        functools.partial(_route_kernel, ctx_rows=ctx_rows),
        out_shape=jax.ShapeDtypeStruct((m, LANES), F32),
        grid=(1,),
        in_specs=[pl.BlockSpec((m, LANES), lambda i: (0, 0))],
        out_specs=pl.BlockSpec((m, LANES), lambda i: (0, 0)),
        compiler_params=_params("arbitrary"), name="route",
    )(logits)


SLOTS_PER_BLOCK = 64
GATE_PARTS = 3


def _slots_kernel(g_ref, pos_ref, post_ref, cnt_ref):
    rb = g_ref.shape[0]
    picked = jnp.where(g_ref[...] > 0.0, 1.0, 0.0)
    ri = lax.broadcasted_iota(jnp.int32, (rb, rb), 0)
    ci = lax.broadcasted_iota(jnp.int32, (rb, rb), 1)
    earlier = jnp.where(ci < ri, 1.0, 0.0).astype(BF16)
    rank = jnp.dot(earlier, picked.astype(BF16), preferred_element_type=F32)
    pos = jnp.where(picked > 0.0, rank, -1.0)
    pos_ref[...] = pos
    post_ref[...] = pos.T[:N_EXPERTS, :]
    cnt_ref[0] = jnp.broadcast_to(jnp.sum(picked, axis=0, keepdims=True), (8, LANES))


def _slots(gates):
    m = gates.shape[0]
    rb = ROW_BLOCK
    nb = m // rb
    return pl.pallas_call(
        _slots_kernel,
        out_shape=(jax.ShapeDtypeStruct((m, LANES), F32),
                   jax.ShapeDtypeStruct((N_EXPERTS, m), F32),
                   jax.ShapeDtypeStruct((nb, 8, LANES), F32)),
        grid=(nb,),
        in_specs=[pl.BlockSpec((rb, LANES), lambda i: (i, 0))],
        out_specs=(pl.BlockSpec((rb, LANES), lambda i: (i, 0)),
                   pl.BlockSpec((N_EXPERTS, rb), lambda i: (0, i)),
                   pl.BlockSpec((1, 8, LANES), lambda i: (i, 0, 0))),
        compiler_params=_params("parallel"), name="moe_slots",
    )(gates)


def _slot_onehot(post):
    rb = post.shape[1]
    slot = lax.broadcasted_iota(jnp.int32, (SLOTS_PER_BLOCK, rb), 0).astype(F32)
    parts = [jnp.where(slot == post[e:e + 1, :], 1.0, 0.0) for e in range(N_EXPERTS)]
    return jnp.concatenate(parts, axis=0).astype(BF16)


def _dispatch_kernel(h_ref, post_ref, g_ref, xs_ref, gs_ref):
    onehot = _slot_onehot(post_ref[...])
    d = h_ref.shape[1]
    xs = jnp.dot(onehot, h_ref[...], preferred_element_type=F32)
    xs_ref[...] = xs.astype(BF16).reshape(N_EXPERTS, SLOTS_PER_BLOCK, d)
    g = g_ref[...]
    pieces = jnp.zeros_like(g)
    for k in range(GATE_PARTS):
        piece = g.astype(BF16).astype(F32)
        pieces = pieces + (pltpu.roll(piece, k * N_EXPERTS, axis=1) if k else piece)
        g = g - piece
    gs = jnp.dot(onehot, pieces.astype(BF16), preferred_element_type=F32)
    gs_ref[...] = gs.reshape(N_EXPERTS, SLOTS_PER_BLOCK, LANES)


def _dispatch(h, post, gates):
    m, d = h.shape
    rb = ROW_BLOCK
    nb = m // rb
    s = SLOTS_PER_BLOCK
    return pl.pallas_call(
        _dispatch_kernel,
        out_shape=(jax.ShapeDtypeStruct((N_EXPERTS, nb * s, d), BF16),
                   jax.ShapeDtypeStruct((N_EXPERTS, nb * s, LANES), F32)),
        grid=(nb,),
        in_specs=[pl.BlockSpec((rb, d), lambda i: (i, 0)),
                  pl.BlockSpec((N_EXPERTS, rb), lambda i: (0, i)),
                  pl.BlockSpec((rb, LANES), lambda i: (i, 0))],
        out_specs=(pl.BlockSpec((N_EXPERTS, s, d), lambda i: (0, i, 0)),
                   pl.BlockSpec((N_EXPERTS, s, LANES), lambda i: (0, i, 0))),
        compiler_params=_params("parallel"), name="moe_dispatch",
    )(h, post, gates)


def _expert_kernel(xs_ref, w1_ref, w3_ref, w2_ref, gs_ref, y_ref):
    e = pl.program_id(0)
    lane = lax.broadcasted_iota(jnp.int32, (1, LANES), 1)
    mine = (lane % N_EXPERTS == e) & (lane < GATE_PARTS * N_EXPERTS)
    gate = jnp.sum(jnp.where(mine, gs_ref[0], 0.0), axis=-1, keepdims=True)
    x = xs_ref[0]
    h1 = jnp.dot(x, w1_ref[0], preferred_element_type=F32)
    h3 = jnp.dot(x, w3_ref[0], preferred_element_type=F32)
    hid = (h1 * jax.nn.sigmoid(h1) * h3 * gate).astype(BF16)
    y_ref[0] = jnp.dot(hid, w2_ref[0], preferred_element_type=F32).astype(BF16)


def _expert_ffn(xs, gs, w1, w3, w2):
    e, rows, d = xs.shape
    f = w1.shape[-1]
    tr = _pick(rows, (832, 1024, 512, 256, 128, 64, 16))
    return pl.pallas_call(
        _expert_kernel,
        out_shape=jax.ShapeDtypeStruct((e, rows, d), BF16),
        grid=(e, rows // tr),
        in_specs=[pl.BlockSpec((1, tr, d), lambda i, j: (i, j, 0)),
                  pl.BlockSpec((1, d, f), lambda i, j: (i, 0, 0)),
                  pl.BlockSpec((1, d, f), lambda i, j: (i, 0, 0)),
                  pl.BlockSpec((1, f, d), lambda i, j: (i, 0, 0)),
                  pl.BlockSpec((1, tr, LANES), lambda i, j: (i, j, 0))],
        out_specs=pl.BlockSpec((1, tr, d), lambda i, j: (i, j, 0)),
        compiler_params=_params("parallel", "arbitrary"), name="expert_ffn",
    )(xs, w1, w3, w2, gs)


def _combine_kernel(ys_ref, post_ref, x_ref, mod_ref, o_ref, *, gate_row):
    onehot = _slot_onehot(post_ref[...])
    d = x_ref.shape[1]
    ys = ys_ref[...].reshape(N_EXPERTS * SLOTS_PER_BLOCK, d)
    moe = lax.dot_general(onehot, ys, TN_DIMS, preferred_element_type=F32)
    o_ref[...] = x_ref[...] + mod_ref[0, gate_row:gate_row + 1, :] * moe


def _combine(ys, post, x, mods, ctx_rows, gate_row):
    m, d = x.shape
    rb = ROW_BLOCK
    s = SLOTS_PER_BLOCK
    stream = _stream_of_block(rb, ctx_rows)
    return pl.pallas_call(
        functools.partial(_combine_kernel, gate_row=gate_row),
        out_shape=jax.ShapeDtypeStruct((m, d), F32),
        grid=(m // rb,),
        in_specs=[pl.BlockSpec((N_EXPERTS, s, d), lambda i: (0, i, 0)),
                  pl.BlockSpec((N_EXPERTS, rb), lambda i: (0, i)),
                  pl.BlockSpec((rb, d), lambda i: (i, 0)),
                  pl.BlockSpec((1, 8, d), lambda i: (stream(i), 0, 0))],
        out_specs=pl.BlockSpec((rb, d), lambda i: (i, 0)),
        compiler_params=_params("parallel"), name="moe_combine",
    )(ys, post, x, mods)


def _moe_hidden_kernel(h_ref, w1_ref, w3_ref, g_ref, o_ref):
    e = pl.program_id(1)
    lane = lax.broadcasted_iota(jnp.int32, (1, LANES), 1)
    gate = jnp.sum(jnp.where(lane == e, g_ref[...], 0.0), axis=-1, keepdims=True)
    h = h_ref[...]
    h1 = jnp.dot(h, w1_ref[0], preferred_element_type=F32)
    h3 = jnp.dot(h, w3_ref[0], preferred_element_type=F32)
    o_ref[...] = (h1 * jax.nn.sigmoid(h1) * h3 * gate).astype(BF16)


def _moe_hidden(h, gates, w1, w3):
    m, d = h.shape
    e, _, f = w1.shape
    tm = _pick(m, (1280, 1024, 768, 512, 256))
    return pl.pallas_call(
        _moe_hidden_kernel,
        out_shape=jax.ShapeDtypeStruct((m, e * f), BF16),
        grid=(m // tm, e),
        in_specs=[pl.BlockSpec((tm, d), lambda i, j: (i, 0)),
                  pl.BlockSpec((1, d, f), lambda i, j: (j, 0, 0)),
                  pl.BlockSpec((1, d, f), lambda i, j: (j, 0, 0)),
                  pl.BlockSpec((tm, LANES), lambda i, j: (i, 0))],
        out_specs=pl.BlockSpec((tm, f), lambda i, j: (i, j)),
        compiler_params=_params("parallel", "arbitrary"), name="moe_hidden",
    )(h, w1, w3, gates)


def kernel(x, c, ctx, c_ctx, ada_w, ada_b, norm1_g, norm2_g, w_in, w_dec_up, b_dec, gla_norm_g,
           sg_ln_g, sg_ln_b, sg_w, sg_b, w_branch_a, w_branch_b, w_out, w_router, w_exp1, w_exp3,
           w_exp2, final_g):
    batch, seq, d = x.shape
    ctx_rows = ctx.shape[1]
    depth = ada_w.shape[0]
    assert batch == 1 and seq % ROW_BLOCK == 0 and ctx_rows % ROW_BLOCK == 0

    xa = jnp.concatenate([ctx[0], x[0]], axis=0)
    cond2 = jnp.stack([c_ctx, c[0]], axis=1)
    mods_all = _adaln(cond2, ada_w, ada_b).reshape(depth, 2, 6, d)
    mods_all = jnp.pad(mods_all, ((0, 0), (0, 0), (0, 2), (0, 0)))

    for l in range(depth):
        mods = mods_all[l]
        wl = w_in[l]
        w_main = jnp.concatenate([wl[:, :OFF_DF], wl[:, SCAN_COLS:OFF_GA]], axis=1).astype(BF16)
        w_dlow = jnp.pad(wl[:, OFF_DF:SCAN_COLS], ((0, 0), (0, LANES - 2 * DECAY_RANK))).astype(BF16)
        w_gates = wl[:, OFF_GA:].astype(BF16)
        wup = jnp.zeros((LANES, 2 * QK_DIM), F32)
        wup = wup.at[:DECAY_RANK, :QK_DIM].set(w_dec_up[l, 0])
        wup = wup.at[DECAY_RANK:2 * DECAY_RANK, QK_DIM:].set(w_dec_up[l, 1])
        bdec = b_dec[l].reshape(1, 2 * QK_DIM)
        b_s_bcast = jnp.broadcast_to(sg_b[l][:, :, None], (SG_GROUPS, SG_CHUNK, SG_GROUP_CH))
        w_r = jnp.pad(w_router[l], ((0, 0), (0, LANES - N_EXPERTS)))

        h = _prenorm(xa, norm1_g[l], mods, ctx_rows, shift_row=0, scale_row=1)
        p_main = _matmul(h, w_main)
        dlow = _matmul(h, w_dlow)

        prep = _gla_prep(p_main, dlow, wup, bdec)
        stf, stb = _gla_scan(prep[7], prep[8], prep[9], ctx_rows // ROW_BLOCK)
        a = _gla_out(prep[:7], stf, stb, p_main, gla_norm_g[l])
        s = _sgu(p_main, sg_ln_g[l], sg_ln_b[l], sg_w[l].astype(BF16), b_s_bcast)
        mix = _merge(h, a, s, w_gates, w_branch_a[l].astype(BF16), w_branch_b[l].astype(BF16))
        xa = _matmul_gated_residual(mix, w_out[l].astype(BF16), xa, mods, ctx_rows, gate_row=2)

        h2, logits = _prenorm(xa, norm2_g[l], mods, ctx_rows, shift_row=3, scale_row=4, w_router=w_r)
        gates = _route(logits, ctx_rows)
        pos, post, cnt = _slots(gates)
        w1, w3, w2 = w_exp1[l].astype(BF16), w_exp3[l].astype(BF16), w_exp2[l].astype(BF16)
        xs, gs = _dispatch(h2, post, gates)
        ys = _expert_ffn(xs, gs, w1, w3, w2)
        xa = _combine(ys, post, xa, mods, ctx_rows, gate_row=5)

        def dense_overflow(xa, h2=h2, gates=gates, pos=pos, w1=w1, w3=w3, w2=w2, mods=mods):
            late = jnp.where(pos >= SLOTS_PER_BLOCK, gates, 0.0)
            hid = _moe_hidden(h2, late, w1, w3)
            return _matmul_gated_residual(hid, w2.reshape(N_EXPERTS * EXPERT_FF, d), xa, mods,
                                          ctx_rows, gate_row=5)

        xa = lax.cond(jnp.max(cnt) > SLOTS_PER_BLOCK, dense_overflow, lambda xa: xa, xa)

    return _final_norm(xa, final_g, ctx_rows)[None]
```

```python
import functools

import jax
import jax.numpy as jnp
from jax import lax
from jax.experimental import pallas as pl
from jax.experimental.pallas import tpu as pltpu

EPS = 1e-6
GLA_HEADS = 8
GLA_DK = 64
GLA_DV = 128
DECAY_RANK = 16
GLA_TAU = 16.0
SG_GROUPS = 8
SG_GROUP_CH = 128
SG_CHUNK = 128
N_EXPERTS = 16
EXPERT_FF = 256
EC_CAPACITY = 2

QK_DIM = GLA_HEADS * GLA_DK
V_DIM = GLA_HEADS * GLA_DV
SG_DIM = SG_GROUPS * SG_GROUP_CH
OFF_K = QK_DIM
OFF_V = 2 * QK_DIM
OFF_DF = OFF_V + V_DIM
OFF_DB = OFF_DF + DECAY_RANK
SCAN_COLS = OFF_DB + DECAY_RANK
OFF_U = SCAN_COLS + V_DIM
OFF_VS = OFF_U + SG_DIM
OFF_GA = OFF_VS + SG_DIM

RUV_COL_R, RUV_COL_U, RUV_COL_VS = 0, 1, 2

LANES = 128
ROW_BLOCK = 256
HEAD_PAIR = 2 * GLA_DK
EXP_CLAMP = 80.0
VMEM_LIMIT_BYTES = 52 * 1024 * 1024

F32 = jnp.float32
BF16 = jnp.bfloat16
NT_DIMS = (((1,), (1,)), ((), ()))
TN_DIMS = (((0,), (0,)), ((), ()))


def _pick(n, candidates):
    for c in candidates:
        if n % c == 0:
            return c
    raise ValueError(f"no tile of {candidates} divides {n}")


def _params(*semantics):
    return pltpu.CompilerParams(dimension_semantics=semantics,
                                vmem_limit_bytes=VMEM_LIMIT_BYTES)


def _stream_of_block(tm, ctx_rows):
    return lambda i: (i * tm >= ctx_rows).astype(jnp.int32)


def _adaln_kernel(cond_ref, w_ref, b_ref, out_ref):
    @pl.when(pl.program_id(1) == 0)
    def _():
        out_ref[0] = jnp.broadcast_to(b_ref[0], out_ref.shape[1:])

    cond = cond_ref[...]
    s = cond * jax.nn.sigmoid(cond)
    s0 = s[:, 0:1]
    s1 = s[:, 1:2]
    n6 = out_ref.shape[-1]
    tn = _pick(n6, (2048, 1024, 512, 256, 128))

    def body(j, carry):
        c0 = pl.multiple_of(j * tn, tn)
        w = w_ref[0, :, pl.ds(c0, tn)]
        a0 = jnp.sum(w * s0, axis=0, keepdims=True)
        a1 = jnp.sum(w * s1, axis=0, keepdims=True)
        out_ref[0, :, pl.ds(c0, tn)] += jnp.concatenate([a0, a1], axis=0)
        return carry

    lax.fori_loop(0, n6 // tn, body, 0)


def _adaln(cond2, ada_w, ada_b):
    nl, d, n6 = ada_w.shape
    tk = _pick(d, (128, 64, 8))
    return pl.pallas_call(
        _adaln_kernel,
        out_shape=jax.ShapeDtypeStruct((nl, 2, n6), F32),
        grid=(nl, d // tk),
        in_specs=[pl.BlockSpec((tk, 2), lambda l, k: (k, 0)),
                  pl.BlockSpec((1, tk, n6), lambda l, k: (l, k, 0)),
                  pl.BlockSpec((1, 1, n6), lambda l, k: (l, 0, 0))],
        out_specs=pl.BlockSpec((1, 2, n6), lambda l, k: (l, 0, 0)),
        compiler_params=_params("parallel", "arbitrary"),
        name="adaln",
    )(cond2, ada_w, ada_b.reshape(nl, 1, n6))


def _modulated_norm(x, g, mod, shift_row, scale_row):
    ms = jnp.mean(x * x, axis=-1, keepdims=True)
    y = x * lax.rsqrt(ms + EPS) * g
    return y * (1.0 + mod[scale_row:scale_row + 1, :]) + mod[shift_row:shift_row + 1, :]


def _prenorm_kernel(x_ref, g_ref, mod_ref, h_ref, *, shift_row, scale_row):
    h = _modulated_norm(x_ref[...], g_ref[...], mod_ref[0], shift_row, scale_row)
    h_ref[...] = h.astype(BF16)


def _prenorm_router_kernel(x_ref, g_ref, mod_ref, wr_hi_ref, wr_lo_ref, h_ref, logit_ref, *,
                           shift_row, scale_row):
    h = _modulated_norm(x_ref[...], g_ref[...], mod_ref[0], shift_row, scale_row)
    h_hi, h_lo = _split_bf16(h)
    h_ref[...] = h_hi
    logit_ref[...] = (jnp.dot(h_hi, wr_hi_ref[...], preferred_element_type=F32)
                      + jnp.dot(h_lo, wr_hi_ref[...], preferred_element_type=F32)
                      + jnp.dot(h_hi, wr_lo_ref[...], preferred_element_type=F32))


def _prenorm(x, g, mods, ctx_rows, shift_row, scale_row, w_router=None):
    m, d = x.shape
    tm = ROW_BLOCK
    stream = _stream_of_block(tm, ctx_rows)
    in_specs = [pl.BlockSpec((tm, d), lambda i: (i, 0)),
                pl.BlockSpec((1, d), lambda i: (0, 0)),
                pl.BlockSpec((1, 8, d), lambda i: (stream(i), 0, 0))]
    h_spec = pl.BlockSpec((tm, d), lambda i: (i, 0))
    h_shape = jax.ShapeDtypeStruct((m, d), BF16)
    if w_router is None:
        return pl.pallas_call(
            functools.partial(_prenorm_kernel, shift_row=shift_row, scale_row=scale_row),
            out_shape=h_shape, grid=(m // tm,), in_specs=in_specs, out_specs=h_spec,
            compiler_params=_params("parallel"), name="prenorm",
        )(x, g.reshape(1, d), mods)
    return pl.pallas_call(
        functools.partial(_prenorm_router_kernel, shift_row=shift_row, scale_row=scale_row),
        out_shape=(h_shape, jax.ShapeDtypeStruct((m, LANES), F32)),
        grid=(m // tm,),
        in_specs=in_specs + [pl.BlockSpec((d, LANES), lambda i: (0, 0))] * 2,
        out_specs=(h_spec, pl.BlockSpec((tm, LANES), lambda i: (i, 0))),
        compiler_params=_params("parallel"), name="prenorm_router",
    )(x, g.reshape(1, d), mods, *_split_bf16(w_router))


def _cast_shift_kernel(a_ref, b_ref, o_ref, *, shift):
    o_ref[...] = jnp.concatenate([a_ref[:, shift:], b_ref[:, :shift]], axis=1).astype(BF16)


def _cast_columns(w, start, n_cols):
    k = w.shape[0]
    tn = 512
    shift = start % LANES
    base = start - shift
    assert shift and base % tn == 0 and n_cols % tn == 0
    tk = _pick(k, (1024, 512, 256, 128))
    return pl.pallas_call(
        functools.partial(_cast_shift_kernel, shift=shift),
        out_shape=jax.ShapeDtypeStruct((k, n_cols), BF16),
        grid=(k // tk, n_cols // tn),
        in_specs=[pl.BlockSpec((tk, tn), lambda i, j: (i, base // tn + j)),
                  pl.BlockSpec((tk, LANES), lambda i, j: (i, (base + tn * (j + 1)) // LANES))],
        out_specs=pl.BlockSpec((tk, tn), lambda i, j: (i, j)),
        compiler_params=_params("parallel", "parallel"), name="cast_columns",
    )(w, w)


def _mm_kernel(a_ref, w_ref, o_ref):
    o_ref[...] = jnp.dot(a_ref[...], w_ref[...], preferred_element_type=F32).astype(o_ref.dtype)


def _matmul(a, w, out_dtype=F32, n_cols=None):
    m, k = a.shape
    n = w.shape[1] if n_cols is None else n_cols
    tm = _pick(m, (1280, 1024, 768, 512, 256))
    tn = _pick(n, (512, 256, 128))
    return pl.pallas_call(
        _mm_kernel,
        out_shape=jax.ShapeDtypeStruct((m, n), out_dtype),
        grid=(m // tm, n // tn),
        in_specs=[pl.BlockSpec((tm, k), lambda i, j: (i, 0)),
                  pl.BlockSpec((k, tn), lambda i, j: (0, j))],
        out_specs=pl.BlockSpec((tm, tn), lambda i, j: (i, j)),
        compiler_params=_params("parallel", "arbitrary"), name="matmul",
    )(a, w)


def _mm_resid_kernel(a_ref, w_ref, x_ref, mod_ref, o_ref, *, gate_row, ctx_rows):
    tm = a_ref.shape[0]
    acc = jnp.dot(a_ref[...], w_ref[...], preferred_element_type=F32)
    row = pl.program_id(0) * tm + lax.broadcasted_iota(jnp.int32, (tm, 1), 0)
    gate = jnp.where(row < ctx_rows, mod_ref[0, gate_row:gate_row + 1, :],
                     mod_ref[1, gate_row:gate_row + 1, :])
    o_ref[...] = x_ref[...] + gate * acc


def _matmul_gated_residual(a, w, x, mods, ctx_rows, gate_row):
    m, k = a.shape
    n = w.shape[1]
    tm = _pick(m, (1280, 1024, 768, 512, 256))
    tn = _pick(n, (512, 256, 128))
    return pl.pallas_call(
        functools.partial(_mm_resid_kernel, gate_row=gate_row, ctx_rows=ctx_rows),
        out_shape=jax.ShapeDtypeStruct((m, n), F32),
        grid=(m // tm, n // tn),
        in_specs=[pl.BlockSpec((tm, k), lambda i, j: (i, 0)),
                  pl.BlockSpec((k, tn), lambda i, j: (0, j)),
                  pl.BlockSpec((tm, tn), lambda i, j: (i, j)),
                  pl.BlockSpec((2, 8, tn), lambda i, j: (0, 0, j))],
        out_specs=pl.BlockSpec((tm, tn), lambda i, j: (i, j)),
        compiler_params=_params("parallel", "arbitrary"), name="out_proj",
    )(a, w, x, mods)


def _merge_kernel(h_ref, a_ref, s_ref, wga_ref, wgb_ref, wa_ref, wb_ref, m_ref):
    h = h_ref[...]
    ga = jnp.dot(h, wga_ref[...], preferred_element_type=F32)
    gb = jnp.dot(h, wgb_ref[...], preferred_element_type=F32)
    ya = jnp.dot(a_ref[...], wa_ref[...], preferred_element_type=F32)
    yb = jnp.dot(s_ref[...], wb_ref[...], preferred_element_type=F32)
    m_ref[...] = (jax.nn.sigmoid(ga) * ya + jax.nn.sigmoid(gb) * yb).astype(BF16)


def _merge(h, a, s, w_gates, col_ga, w_a, w_b):
    m, d = h.shape
    ka = a.shape[1]
    kb = s.shape[1]
    tm = _pick(m, (1280, 1024, 768, 512, 256))
    tn = _pick(d, (256, 128))
    nb = d // tn
    j0 = col_ga // tn
    return pl.pallas_call(
        _merge_kernel,
        out_shape=jax.ShapeDtypeStruct((m, d), BF16),
        grid=(m // tm, nb),
        in_specs=[pl.BlockSpec((tm, d), lambda i, j: (i, 0)),
                  pl.BlockSpec((tm, ka), lambda i, j: (i, 0)),
                  pl.BlockSpec((tm, kb), lambda i, j: (i, 0)),
                  pl.BlockSpec((d, tn), lambda i, j: (0, j0 + j)),
                  pl.BlockSpec((d, tn), lambda i, j: (0, j0 + nb + j)),
                  pl.BlockSpec((ka, tn), lambda i, j: (0, j)),
                  pl.BlockSpec((kb, tn), lambda i, j: (0, j))],
        out_specs=pl.BlockSpec((tm, tn), lambda i, j: (i, j)),
        compiler_params=_params("parallel", "arbitrary"), name="merge",
    )(h, a, s, w_gates, w_gates, w_a, w_b)


def _sgu_kernel(u_ref, vs_ref, lng_ref, lnb_ref, ws_ref, bs_ref, s_ref):
    tm = u_ref.shape[0]
    u = jax.nn.gelu(u_ref[...].astype(F32))
    v = jax.nn.gelu(vs_ref[...].astype(F32))
    vc = v - jnp.mean(v, axis=-1, keepdims=True)
    vn = vc * lax.rsqrt(jnp.mean(vc * vc, axis=-1, keepdims=True) + EPS)
    vn = (vn * lng_ref[...] + lnb_ref[...]).astype(BF16)
    for ch in range(tm // SG_CHUNK):
        rows = slice(ch * SG_CHUNK, (ch + 1) * SG_CHUNK)
        for g in range(SG_GROUPS):
            cols = slice(g * SG_GROUP_CH, (g + 1) * SG_GROUP_CH)
            mixed = jnp.dot(ws_ref[g], vn[rows, cols], preferred_element_type=F32) + bs_ref[g]
            s_ref[rows, cols] = (u[rows, cols] * mixed).astype(BF16)


def _sgu(p_ruv, ln_g, ln_b, w_s, b_s_bcast):
    m = p_ruv.shape[0]
    tm = ROW_BLOCK
    return pl.pallas_call(
        _sgu_kernel,
        out_shape=jax.ShapeDtypeStruct((m, SG_DIM), BF16),
        grid=(m // tm,),
        in_specs=[pl.BlockSpec((tm, SG_DIM), lambda i: (i, RUV_COL_U)),
                  pl.BlockSpec((tm, SG_DIM), lambda i: (i, RUV_COL_VS)),
                  pl.BlockSpec((1, SG_DIM), lambda i: (0, 0)),
                  pl.BlockSpec((1, SG_DIM), lambda i: (0, 0)),
                  pl.BlockSpec((SG_GROUPS, SG_CHUNK, SG_CHUNK), lambda i: (0, 0, 0)),
                  pl.BlockSpec((SG_GROUPS, SG_CHUNK, SG_GROUP_CH), lambda i: (0, 0, 0))],
        out_specs=pl.BlockSpec((tm, SG_DIM), lambda i: (i, 0)),
        compiler_params=_params("parallel"), name="sgu",
    )(p_ruv, p_ruv, ln_g.reshape(1, SG_DIM), ln_b.reshape(1, SG_DIM), w_s, b_s_bcast)


def _split_bf16(x):
    hi = x.astype(BF16)
    lo = (x - hi.astype(F32)).astype(BF16)
    return hi, lo


def _gla_prep_kernel(q_ref, k_ref, v_ref, dl_ref, wup_ref, bdec_ref,
                     qmf_ref, kmf_ref, qef_ref, qmb_ref, kmb_ref, qeb_ref,
                     utf_ref, utb_ref, dtot_ref):
    c = q_ref.shape[0]
    half = c // 2
    z = jnp.dot(dl_ref[...], wup_ref[...], precision=lax.Precision.HIGHEST,
                preferred_element_type=F32) + bdec_ref[...]
    g = -(jnp.maximum(-z, 0.0) + jnp.log1p(jnp.exp(-jnp.abs(z)))) * (1.0 / GLA_TAU)
    ri = lax.broadcasted_iota(jnp.int32, (c, c), 0)
    ci = lax.broadcasted_iota(jnp.int32, (c, c), 1)
    lane = lax.broadcasted_iota(jnp.int32, (1, HEAD_PAIR), 1)
    q = q_ref[...].astype(F32) * (GLA_DK ** -0.5)
    k = k_ref[...].astype(F32)
    vb = v_ref[...]

    def direction(gd, tri, mid_row, end_row, qm_ref, km_ref, qe_ref, ut_ref):
        hi, lo = _split_bf16(gd)
        tri = tri.astype(BF16)
        bc = (jnp.dot(tri, hi, preferred_element_type=F32)
              + jnp.dot(tri, lo, preferred_element_type=F32))
        rho = bc[mid_row:mid_row + 1, :]
        tot = bc[end_row:end_row + 1, :]
        e1 = bc - rho
        qm_ref[...] = (q * jnp.exp(jnp.minimum(e1, EXP_CLAMP))).astype(BF16)
        km_ref[...] = (k * jnp.exp(jnp.minimum(-e1, EXP_CLAMP))).astype(BF16)
        qe_ref[...] = (q * jnp.exp(bc)).astype(BF16)
        k2 = (k * jnp.exp(tot - bc)).astype(BF16)
        for p in range(GLA_HEADS // 2):
            kp = k2[:, p * HEAD_PAIR:(p + 1) * HEAD_PAIR]
            ut = jnp.zeros((GLA_DV, HEAD_PAIR), F32)
            for hh in range(2):
                h = 2 * p + hh
                keep = (lane < GLA_DK) if hh == 0 else (lane >= GLA_DK)
                kh = jnp.where(keep, kp, jnp.zeros_like(kp))
                ut = ut + lax.dot_general(vb[:, h * GLA_DV:(h + 1) * GLA_DV], kh, TN_DIMS,
                                          preferred_element_type=F32)
            ut_ref[0, :, p * HEAD_PAIR:(p + 1) * HEAD_PAIR] = ut
        return jnp.exp(tot)

    df = direction(g[:, :QK_DIM], ci <= ri, half - 1, c - 1, qmf_ref, kmf_ref, qef_ref, utf_ref)
    db = direction(g[:, QK_DIM:], ci >= ri, half, 0, qmb_ref, kmb_ref, qeb_ref, utb_ref)
    dtot_ref[0] = jnp.concatenate([df, db, jnp.zeros((6, QK_DIM), F32)], axis=0)


def _gla_prep(qkv, dlow, wup, bdec):
    m = qkv.shape[0]
    c = ROW_BLOCK
    nb = m // c
    qk_spec = pl.BlockSpec((c, QK_DIM), lambda i: (i, 0))
    bf_qk = jax.ShapeDtypeStruct((m, QK_DIM), BF16)
    ut_shape = jax.ShapeDtypeStruct((nb, GLA_DV, QK_DIM), F32)
    ut_spec = pl.BlockSpec((1, GLA_DV, QK_DIM), lambda i: (i, 0, 0))
    return pl.pallas_call(
        _gla_prep_kernel,
        out_shape=(bf_qk,) * 6 + (ut_shape, ut_shape, jax.ShapeDtypeStruct((nb, 8, QK_DIM), F32)),
        grid=(nb,),
        in_specs=[pl.BlockSpec((c, QK_DIM), lambda i: (i, 0)),
                  pl.BlockSpec((c, QK_DIM), lambda i: (i, 1)),
                  pl.BlockSpec((c, V_DIM), lambda i: (i, 1)),
                  pl.BlockSpec((c, LANES), lambda i: (i, 0)),
                  pl.BlockSpec((LANES, 2 * QK_DIM), lambda i: (0, 0)),
                  pl.BlockSpec((1, 2 * QK_DIM), lambda i: (0, 0))],
        out_specs=(qk_spec,) * 6 + (ut_spec, ut_spec,
                                    pl.BlockSpec((1, 8, QK_DIM), lambda i: (i, 0, 0))),
        compiler_params=_params("parallel"), name="gla_prep",
    )(qkv, qkv, qkv, dlow, wup, bdec)


def _gla_scan_kernel(utf_ref, dtf_ref, utb_ref, dtb_ref, stf_ref, stb_ref, sf_ref, sb_ref):
    @pl.when(pl.program_id(0) == 0)
    def _():
        sf_ref[...] = jnp.zeros_like(sf_ref)
        sb_ref[...] = jnp.zeros_like(sb_ref)

    sf = sf_ref[...]
    stf_ref[0] = sf.astype(BF16)
    sf_ref[...] = dtf_ref[0, 0:1, :] * sf + utf_ref[0]
    sb = sb_ref[...]
    stb_ref[0] = sb.astype(BF16)
    sb_ref[...] = dtb_ref[0, 1:2, :] * sb + utb_ref[0]


def _gla_scan(utf, utb, dtot, ctx_blocks):
    nb = utf.shape[0]

    def bwd(t):
        return jnp.where(t < ctx_blocks, ctx_blocks - 1 - t, nb + ctx_blocks - 1 - t)

    st_shape = jax.ShapeDtypeStruct((nb, GLA_DV, QK_DIM), BF16)
    blk = (1, GLA_DV, QK_DIM)
    return pl.pallas_call(
        _gla_scan_kernel,
        out_shape=(st_shape, st_shape),
        grid=(nb,),
        in_specs=[pl.BlockSpec(blk, lambda t: (t, 0, 0)),
                  pl.BlockSpec((1, 8, QK_DIM), lambda t: (t, 0, 0)),
                  pl.BlockSpec(blk, lambda t: (bwd(t), 0, 0)),
                  pl.BlockSpec((1, 8, QK_DIM), lambda t: (bwd(t), 0, 0))],
        out_specs=(pl.BlockSpec(blk, lambda t: (t, 0, 0)),
                   pl.BlockSpec(blk, lambda t: (bwd(t), 0, 0))),
        scratch_shapes=[pltpu.VMEM((GLA_DV, QK_DIM), F32), pltpu.VMEM((GLA_DV, QK_DIM), F32)],
        compiler_params=_params("arbitrary"), name="gla_scan",
    )(utf, dtot, utb, dtot)


def _gla_out_kernel(qmf_ref, kmf_ref, qef_ref, qmb_ref, kmb_ref, qeb_ref, vb_ref,
                    stf_ref, stb_ref, r_ref, gain_ref, a_ref):
    c = qmf_ref.shape[0]
    ri = lax.broadcasted_iota(jnp.int32, (c, c), 0)
    ci = lax.broadcasted_iota(jnp.int32, (c, c), 1)
    lower = ci <= ri
    upper = ci >= ri
    lane = lax.broadcasted_iota(jnp.int32, (1, HEAD_PAIR), 1)
    for p in range(GLA_HEADS // 2):
        pair = slice(p * HEAD_PAIR, (p + 1) * HEAD_PAIR)
        qmf, kmf, qef = qmf_ref[:, pair], kmf_ref[:, pair], qef_ref[:, pair]
        qmb, kmb, qeb = qmb_ref[:, pair], kmb_ref[:, pair], qeb_ref[:, pair]
        stf = stf_ref[0, :, pair]
        stb = stb_ref[0, :, pair]
        for hh in range(2):
            h = 2 * p + hh
            keep = (lane < GLA_DK) if hh == 0 else (lane >= GLA_DK)
            zero = jnp.zeros_like(kmf)
            af = lax.dot_general(qmf, jnp.where(keep, kmf, zero), NT_DIMS, preferred_element_type=F32)
            ab = lax.dot_general(qmb, jnp.where(keep, kmb, zero), NT_DIMS, preferred_element_type=F32)
            att = (jnp.where(lower, af, 0.0) + jnp.where(upper, ab, 0.0)).astype(BF16)
            cols = slice(h * GLA_DV, (h + 1) * GLA_DV)
            o = jnp.dot(att, vb_ref[:, cols], preferred_element_type=F32)
            o = o + lax.dot_general(jnp.where(keep, qef, zero), stf, NT_DIMS, preferred_element_type=F32)
            o = o + lax.dot_general(jnp.where(keep, qeb, zero), stb, NT_DIMS, preferred_element_type=F32)
            o = o * lax.rsqrt(jnp.mean(o * o, axis=-1, keepdims=True) + EPS)
            r = r_ref[:, cols].astype(F32)
            a_ref[:, cols] = (o * gain_ref[:, cols] * (r * jax.nn.sigmoid(r))).astype(BF16)


def _gla_out(prep, qkv, stf, stb, p_ruv, gain):
    qmf, kmf, qef, qmb, kmb, qeb = prep
    m = qkv.shape[0]
    c = ROW_BLOCK
    qk_spec = pl.BlockSpec((c, QK_DIM), lambda i: (i, 0))
    st_spec = pl.BlockSpec((1, GLA_DV, QK_DIM), lambda i: (i, 0, 0))
    return pl.pallas_call(
        _gla_out_kernel,
        out_shape=jax.ShapeDtypeStruct((m, V_DIM), BF16),
        grid=(m // c,),
        in_specs=[qk_spec] * 6 + [pl.BlockSpec((c, V_DIM), lambda i: (i, 1)), st_spec, st_spec,
                                  pl.BlockSpec((c, V_DIM), lambda i: (i, RUV_COL_R)),
                                  pl.BlockSpec((1, V_DIM), lambda i: (0, 0))],
        out_specs=pl.BlockSpec((c, V_DIM), lambda i: (i, 0)),
        compiler_params=_params("parallel"), name="gla_out",
    )(qmf, kmf, qef, qmb, kmb, qeb, qkv, stf, stb, p_ruv, gain.reshape(1, V_DIM))


ONE_F32_BITS = 0x3F800000
SEARCH_STEPS = 31


def _route_kernel(logit_ref, g_ref, *, ctx_rows):
    m = logit_ref.shape[0]
    rb = ROW_BLOCK
    lane = lax.broadcasted_iota(jnp.int32, (1, LANES), 1)
    valid = lane < N_EXPERTS

    def softmax_block(i, carry):
        r0 = pl.multiple_of(i * rb, rb)
        lg = jnp.where(valid, logit_ref[pl.ds(r0, rb), :], -jnp.inf)
        ex = jnp.exp(lg - jnp.max(lg, axis=-1, keepdims=True))
        g_ref[pl.ds(r0, rb), :] = ex / jnp.sum(ex, axis=-1, keepdims=True)
        return carry

    lax.fori_loop(0, m // rb, softmax_block, 0)

    def bits_of(r0):
        return lax.bitcast_convert_type(g_ref[pl.ds(r0, rb), :], jnp.int32)

    def count_ge(blk0, nblk, t):
        def body(i, acc):
            r0 = pl.multiple_of((blk0 + i) * rb, rb)
            hit = jnp.where(bits_of(r0) >= t, 1, 0)
            return acc + jnp.sum(hit.reshape(rb // 8, 8, LANES), axis=0)

        acc = lax.fori_loop(0, nblk, body, jnp.zeros((8, LANES), jnp.int32))
        return jnp.sum(acc, axis=0, keepdims=True)

    def route_set(blk0, nblk):
        cap = EC_CAPACITY * nblk * rb // N_EXPERTS

        def halve(_, lohi):
            lo, hi = lohi
            mid = lo + ((hi - lo) >> 1)
            ok = count_ge(blk0, nblk, mid) >= cap
            return jnp.where(ok, mid, lo), jnp.where(ok, hi, mid)

        thr, _ = lax.fori_loop(0, SEARCH_STEPS, halve,
                               (jnp.zeros((1, LANES), jnp.int32),
                                jnp.full((1, LANES), ONE_F32_BITS + 1, jnp.int32)))
        need = (cap - count_ge(blk0, nblk, thr + 1)).astype(F32)
        ri = lax.broadcasted_iota(jnp.int32, (rb, rb), 0)
        ci = lax.broadcasted_iota(jnp.int32, (rb, rb), 1)
        earlier = jnp.where(ci < ri, 1.0, 0.0).astype(BF16)

        def finalize(i, seen):
            r0 = pl.multiple_of((blk0 + i) * rb, rb)
            aff = g_ref[pl.ds(r0, rb), :]
            bits = lax.bitcast_convert_type(aff, jnp.int32)
            tie = bits == thr
            tie_f = jnp.where(tie, 1.0, 0.0)
            rank = seen + jnp.dot(earlier, tie_f.astype(BF16), preferred_element_type=F32)
            g_ref[pl.ds(r0, rb), :] = jnp.where(tie, jnp.where(rank < need, aff, 0.0),
                                                jnp.where(bits > thr, aff, 0.0))
            return seen + jnp.sum(tie_f, axis=0, keepdims=True)

        lax.fori_loop(0, nblk, finalize, jnp.zeros((1, LANES), F32))

    route_set(0, ctx_rows // rb)
    route_set(ctx_rows // rb, (m - ctx_rows) // rb)


def _route(logits, ctx_rows):
    m = logits.shape[0]
    return pl.pallas_call(
        functools.partial(_route_kernel, ctx_rows=ctx_rows),
        out_shape=jax.ShapeDtypeStruct((m, LANES), F32),
        grid=(1,),
        in_specs=[pl.BlockSpec((m, LANES), lambda i: (0, 0))],
        out_specs=pl.BlockSpec((m, LANES), lambda i: (0, 0)),
        compiler_params=_params("arbitrary"), name="route",
    )(logits)


SLOTS_PER_BLOCK = 64
GATE_PARTS = 3


def _slots_kernel(g_ref, pos_ref, post_ref, cnt_ref):
    rb = g_ref.shape[0]
    picked = jnp.where(g_ref[...] > 0.0, 1.0, 0.0)
    ri = lax.broadcasted_iota(jnp.int32, (rb, rb), 0)
    ci = lax.broadcasted_iota(jnp.int32, (rb, rb), 1)
    earlier = jnp.where(ci < ri, 1.0, 0.0).astype(BF16)
    rank = jnp.dot(earlier, picked.astype(BF16), preferred_element_type=F32)
    pos = jnp.where(picked > 0.0, rank, -1.0)
    pos_ref[...] = pos
    post_ref[...] = pos.T[:N_EXPERTS, :]
    cnt_ref[0] = jnp.broadcast_to(jnp.sum(picked, axis=0, keepdims=True), (8, LANES))


def _slots(gates):
    m = gates.shape[0]
    rb = ROW_BLOCK
    nb = m // rb
    return pl.pallas_call(
        _slots_kernel,
        out_shape=(jax.ShapeDtypeStruct((m, LANES), F32),
                   jax.ShapeDtypeStruct((N_EXPERTS, m), F32),
                   jax.ShapeDtypeStruct((nb, 8, LANES), F32)),
        grid=(nb,),
        in_specs=[pl.BlockSpec((rb, LANES), lambda i: (i, 0))],
        out_specs=(pl.BlockSpec((rb, LANES), lambda i: (i, 0)),
                   pl.BlockSpec((N_EXPERTS, rb), lambda i: (0, i)),
                   pl.BlockSpec((1, 8, LANES), lambda i: (i, 0, 0))),
        compiler_params=_params("parallel"), name="moe_slots",
    )(gates)


def _slot_onehot(post):
    rb = post.shape[1]
    slot = lax.broadcasted_iota(jnp.int32, (SLOTS_PER_BLOCK, rb), 0).astype(F32)
    parts = [jnp.where(slot == post[e:e + 1, :], 1.0, 0.0) for e in range(N_EXPERTS)]
    return jnp.concatenate(parts, axis=0).astype(BF16)


def _dispatch_kernel(h_ref, post_ref, g_ref, xs_ref, gs_ref):
    onehot = _slot_onehot(post_ref[...])
    d = h_ref.shape[1]
    xs = jnp.dot(onehot, h_ref[...], preferred_element_type=F32)
    xs_ref[...] = xs.astype(BF16).reshape(N_EXPERTS, SLOTS_PER_BLOCK, d)
    g = g_ref[...]
    pieces = jnp.zeros_like(g)
    for k in range(GATE_PARTS):
        piece = g.astype(BF16).astype(F32)
        pieces = pieces + (pltpu.roll(piece, k * N_EXPERTS, axis=1) if k else piece)
        g = g - piece
    gs = jnp.dot(onehot, pieces.astype(BF16), preferred_element_type=F32)
    gs_ref[...] = gs.reshape(N_EXPERTS, SLOTS_PER_BLOCK, LANES)


def _dispatch(h, post, gates):
    m, d = h.shape
    rb = ROW_BLOCK
    nb = m // rb
    s = SLOTS_PER_BLOCK
    return pl.pallas_call(
        _dispatch_kernel,
        out_shape=(jax.ShapeDtypeStruct((N_EXPERTS, nb * s, d), BF16),
                   jax.ShapeDtypeStruct((N_EXPERTS, nb * s, LANES), F32)),
        grid=(nb,),
        in_specs=[pl.BlockSpec((rb, d), lambda i: (i, 0)),
                  pl.BlockSpec((N_EXPERTS, rb), lambda i: (0, i)),
                  pl.BlockSpec((rb, LANES), lambda i: (i, 0))],
        out_specs=(pl.BlockSpec((N_EXPERTS, s, d), lambda i: (0, i, 0)),
                   pl.BlockSpec((N_EXPERTS, s, LANES), lambda i: (0, i, 0))),
        compiler_params=_params("parallel"), name="moe_dispatch",
    )(h, post, gates)


def _expert_kernel(xs_ref, w1_ref, w3_ref, w2_ref, gs_ref, y_ref):
    e = pl.program_id(0)
    lane = lax.broadcasted_iota(jnp.int32, (1, LANES), 1)
    mine = (lane % N_EXPERTS == e) & (lane < GATE_PARTS * N_EXPERTS)
    gate = jnp.sum(jnp.where(mine, gs_ref[0], 0.0), axis=-1, keepdims=True)
    x = xs_ref[0]
    h1 = jnp.dot(x, w1_ref[0], preferred_element_type=F32)
    h3 = jnp.dot(x, w3_ref[0], preferred_element_type=F32)
    hid = (h1 * jax.nn.sigmoid(h1) * h3 * gate).astype(BF16)
    y_ref[0] = jnp.dot(hid, w2_ref[0], preferred_element_type=F32).astype(BF16)


def _expert_ffn(xs, gs, w1, w3, w2):
    e, rows, d = xs.shape
    f = w1.shape[-1]
    tr = _pick(rows, (832, 1024, 512, 256, 128, 64, 16))
    return pl.pallas_call(
        _expert_kernel,
        out_shape=jax.ShapeDtypeStruct((e, rows, d), BF16),
        grid=(e, rows // tr),
        in_specs=[pl.BlockSpec((1, tr, d), lambda i, j: (i, j, 0)),
                  pl.BlockSpec((1, d, f), lambda i, j: (i, 0, 0)),
                  pl.BlockSpec((1, d, f), lambda i, j: (i, 0, 0)),
                  pl.BlockSpec((1, f, d), lambda i, j: (i, 0, 0)),
                  pl.BlockSpec((1, tr, LANES), lambda i, j: (i, j, 0))],
        out_specs=pl.BlockSpec((1, tr, d), lambda i, j: (i, j, 0)),
        compiler_params=_params("parallel", "arbitrary"), name="expert_ffn",
    )(xs, w1, w3, w2, gs)


def _combined_rows(ys_ref, post_ref, x_ref, mod_ref, gate_row):
    onehot = _slot_onehot(post_ref[...])
    d = x_ref.shape[1]
    ys = ys_ref[...].reshape(N_EXPERTS * SLOTS_PER_BLOCK, d)
    moe = lax.dot_general(onehot, ys, TN_DIMS, preferred_element_type=F32)
    return x_ref[...] + mod_ref[0, gate_row:gate_row + 1, :] * moe


def _combine_prenorm_kernel(ys_ref, post_ref, x_ref, mod_ref, g_ref, nmod_ref, o_ref, h_ref, *,
                            gate_row):
    x = _combined_rows(ys_ref, post_ref, x_ref, mod_ref, gate_row)
    o_ref[...] = x
    h_ref[...] = _modulated_norm(x, g_ref[...], nmod_ref[0], 0, 1).astype(BF16)


def _combine_final_kernel(ys_ref, post_ref, x_ref, mod_ref, g_ref, o_ref, *, gate_row):
    x = _combined_rows(ys_ref, post_ref, x_ref, mod_ref, gate_row)
    ms = jnp.mean(x * x, axis=-1, keepdims=True)
    o_ref[...] = x * lax.rsqrt(ms + EPS) * g_ref[...]


def _combine(ys, post, x, mods, ctx_rows, gate_row, norm_g, next_mods=None):
    m, d = x.shape
    rb = ROW_BLOCK
    s = SLOTS_PER_BLOCK
    stream = _stream_of_block(rb, ctx_rows)
    row_spec = pl.BlockSpec((rb, d), lambda i: (i, 0))
    mod_spec = pl.BlockSpec((1, 8, d), lambda i: (stream(i), 0, 0))
    in_specs = [pl.BlockSpec((N_EXPERTS, s, d), lambda i: (0, i, 0)),
                pl.BlockSpec((N_EXPERTS, rb), lambda i: (0, i)),
                row_spec, mod_spec, pl.BlockSpec((1, d), lambda i: (0, 0))]
    if next_mods is not None:
        return pl.pallas_call(
            functools.partial(_combine_prenorm_kernel, gate_row=gate_row),
            out_shape=(jax.ShapeDtypeStruct((m, d), F32), jax.ShapeDtypeStruct((m, d), BF16)),
            grid=(m // rb,),
            in_specs=in_specs + [mod_spec],
            out_specs=(row_spec, row_spec),
            compiler_params=_params("parallel"), name="moe_combine",
        )(ys, post, x, mods, norm_g.reshape(1, d), next_mods)
    skip = ctx_rows // rb
    return pl.pallas_call(
        functools.partial(_combine_final_kernel, gate_row=gate_row),
        out_shape=jax.ShapeDtypeStruct((m - ctx_rows, d), F32),
        grid=(m // rb,),
        in_specs=in_specs,
        out_specs=pl.BlockSpec((rb, d), lambda i: (jnp.maximum(i - skip, 0), 0)),
        compiler_params=_params("arbitrary"), name="moe_combine_final",
    )(ys, post, x, mods, norm_g.reshape(1, d))


def _moe_hidden_kernel(h_ref, w1_ref, w3_ref, g_ref, o_ref):
    e = pl.program_id(1)
    lane = lax.broadcasted_iota(jnp.int32, (1, LANES), 1)
    gate = jnp.sum(jnp.where(lane == e, g_ref[...], 0.0), axis=-1, keepdims=True)
    h = h_ref[...]
    h1 = jnp.dot(h, w1_ref[0], preferred_element_type=F32)
    h3 = jnp.dot(h, w3_ref[0], preferred_element_type=F32)
    o_ref[...] = (h1 * jax.nn.sigmoid(h1) * h3 * gate).astype(BF16)


def _moe_hidden(h, gates, w1, w3):
    m, d = h.shape
    e, _, f = w1.shape
    tm = _pick(m, (1280, 1024, 768, 512, 256))
    return pl.pallas_call(
        _moe_hidden_kernel,
        out_shape=jax.ShapeDtypeStruct((m, e * f), BF16),
        grid=(m // tm, e),
        in_specs=[pl.BlockSpec((tm, d), lambda i, j: (i, 0)),
                  pl.BlockSpec((1, d, f), lambda i, j: (j, 0, 0)),
                  pl.BlockSpec((1, d, f), lambda i, j: (j, 0, 0)),
                  pl.BlockSpec((tm, LANES), lambda i, j: (i, 0))],
        out_specs=pl.BlockSpec((tm, f), lambda i, j: (i, j)),
        compiler_params=_params("parallel", "arbitrary"), name="moe_hidden",
    )(h, w1, w3, gates)


def kernel(x, c, ctx, c_ctx, ada_w, ada_b, norm1_g, norm2_g, w_in, w_dec_up, b_dec, gla_norm_g,
           sg_ln_g, sg_ln_b, sg_w, sg_b, w_branch_a, w_branch_b, w_out, w_router, w_exp1, w_exp3,
           w_exp2, final_g):
    batch, seq, d = x.shape
    ctx_rows = ctx.shape[1]
    depth = ada_w.shape[0]
    assert batch == 1 and seq % ROW_BLOCK == 0 and ctx_rows % ROW_BLOCK == 0

    xa = jnp.concatenate([ctx[0], x[0]], axis=0)
    cond2 = jnp.stack([c_ctx, c[0]], axis=1)
    mods_all = _adaln(cond2, ada_w, ada_b).reshape(depth, 2, 6, d)
    mods_all = jnp.pad(mods_all, ((0, 0), (0, 0), (0, 2), (0, 0)))

    for l in range(depth):
        mods = mods_all[l]
        wl = w_in[l]
        w_qkv = wl[:, :OFF_DF].astype(BF16)
        w_dlow = wl[:, OFF_DF:OFF_DF + LANES].astype(BF16)
        w_rest = _cast_columns(wl, SCAN_COLS, wl.shape[1] - SCAN_COLS)
        wup = jnp.zeros((LANES, 2 * QK_DIM), F32)
        wup = wup.at[:DECAY_RANK, :QK_DIM].set(w_dec_up[l, 0])
        wup = wup.at[DECAY_RANK:2 * DECAY_RANK, QK_DIM:].set(w_dec_up[l, 1])
        bdec = b_dec[l].reshape(1, 2 * QK_DIM)
        b_s_bcast = jnp.broadcast_to(sg_b[l][:, :, None], (SG_GROUPS, SG_CHUNK, SG_GROUP_CH))
        w_r = jnp.pad(w_router[l], ((0, 0), (0, LANES - N_EXPERTS)))

        if l == 0:
            h = _prenorm(xa, norm1_g[l], mods, ctx_rows, shift_row=0, scale_row=1)
        p_qkv = _matmul(h, w_qkv, out_dtype=BF16)
        p_ruv = _matmul(h, w_rest, out_dtype=BF16, n_cols=3 * V_DIM)
        dlow = _matmul(h, w_dlow)

        prep = _gla_prep(p_qkv, dlow, wup, bdec)
        stf, stb = _gla_scan(prep[6], prep[7], prep[8], ctx_rows // ROW_BLOCK)
        a = _gla_out(prep[:6], p_qkv, stf, stb, p_ruv, gla_norm_g[l])
        s = _sgu(p_ruv, sg_ln_g[l], sg_ln_b[l], sg_w[l].astype(BF16), b_s_bcast)
        mix = _merge(h, a, s, w_rest, 3 * V_DIM, w_branch_a[l].astype(BF16), w_branch_b[l].astype(BF16))
        xa = _matmul_gated_residual(mix, w_out[l].astype(BF16), xa, mods, ctx_rows, gate_row=2)

        h2, logits = _prenorm(xa, norm2_g[l], mods, ctx_rows, shift_row=3, scale_row=4, w_router=w_r)
        gates = _route(logits, ctx_rows)
        pos, post, cnt = _slots(gates)
        w1, w3, w2 = w_exp1[l].astype(BF16), w_exp3[l].astype(BF16), w_exp2[l].astype(BF16)
        xs, gs = _dispatch(h2, post, gates)
        ys = _expert_ffn(xs, gs, w1, w3, w2)

        def dense_overflow(xa, h2=h2, gates=gates, pos=pos, w1=w1, w3=w3, w2=w2, mods=mods):
            late = jnp.where(pos >= SLOTS_PER_BLOCK, gates, 0.0)
            hid = _moe_hidden(h2, late, w1, w3)
            return _matmul_gated_residual(hid, w2.reshape(N_EXPERTS * EXPERT_FF, d), xa, mods,
                                          ctx_rows, gate_row=5)

        xa = lax.cond(jnp.max(cnt) > SLOTS_PER_BLOCK, dense_overflow, lambda xa: xa, xa)
        if l + 1 < depth:
            xa, h = _combine(ys, post, xa, mods, ctx_rows, 5, norm1_g[l + 1], mods_all[l + 1])
        else:
            out = _combine(ys, post, xa, mods, ctx_rows, 5, final_g)

    return out[None]
```

```python
import functools

import jax
import jax.numpy as jnp
from jax import lax
from jax.experimental import pallas as pl
from jax.experimental.pallas import tpu as pltpu

EPS = 1e-6
GLA_HEADS = 8
GLA_DK = 64
GLA_DV = 128
DECAY_RANK = 16
GLA_TAU = 16.0
SG_GROUPS = 8
SG_GROUP_CH = 128
SG_CHUNK = 128
N_EXPERTS = 16
EXPERT_FF = 256
EC_CAPACITY = 2

QK_DIM = GLA_HEADS * GLA_DK
V_DIM = GLA_HEADS * GLA_DV
SG_DIM = SG_GROUPS * SG_GROUP_CH
OFF_K = QK_DIM
OFF_V = 2 * QK_DIM
OFF_DF = OFF_V + V_DIM
OFF_DB = OFF_DF + DECAY_RANK
SCAN_COLS = OFF_DB + DECAY_RANK
OFF_U = SCAN_COLS + V_DIM
OFF_VS = OFF_U + SG_DIM
OFF_GA = OFF_VS + SG_DIM

RUV_COL_R, RUV_COL_U, RUV_COL_VS = 0, 1, 2

LANES = 128
ROW_BLOCK = 256
HEAD_PAIR = 2 * GLA_DK
EXP_CLAMP = 80.0
VMEM_LIMIT_BYTES = 52 * 1024 * 1024

F32 = jnp.float32
BF16 = jnp.bfloat16
NT_DIMS = (((1,), (1,)), ((), ()))
TN_DIMS = (((0,), (0,)), ((), ()))


def _pick(n, candidates):
    for c in candidates:
        if n % c == 0:
            return c
    raise ValueError(f"no tile of {candidates} divides {n}")


def _params(*semantics):
    return pltpu.CompilerParams(dimension_semantics=semantics,
                                vmem_limit_bytes=VMEM_LIMIT_BYTES)


def _stream_of_block(tm, ctx_rows):
    return lambda i: (i * tm >= ctx_rows).astype(jnp.int32)


def _adaln_kernel(cond_ref, w_ref, b_ref, out_ref, *, rows):
    d = cond_ref.shape[0]
    tn = out_ref.shape[-1]

    def body(i, acc):
        r0 = pl.multiple_of(i * rows, rows)
        cond = cond_ref[pl.ds(r0, rows), :]
        s = cond * jax.nn.sigmoid(cond)
        w = w_ref[0, pl.ds(r0, rows), :]
        a0 = jnp.sum(w * s[:, 0:1], axis=0, keepdims=True)
        a1 = jnp.sum(w * s[:, 1:2], axis=0, keepdims=True)
        return acc + jnp.concatenate([a0, a1], axis=0)

    acc = lax.fori_loop(0, d // rows, body, jnp.zeros((2, tn), F32))
    out_ref[0] = acc + b_ref[0]


def _adaln(cond2, ada_w, ada_b):
    nl, d, n6 = ada_w.shape
    tn = _pick(n6, (1024, 512, 256, 128))
    return pl.pallas_call(
        functools.partial(_adaln_kernel, rows=_pick(d, (256, 128, 8))),
        out_shape=jax.ShapeDtypeStruct((nl, 2, n6), F32),
        grid=(nl, n6 // tn),
        in_specs=[pl.BlockSpec((d, 2), lambda l, j: (0, 0)),
                  pl.BlockSpec((1, d, tn), lambda l, j: (l, 0, j)),
                  pl.BlockSpec((1, 1, tn), lambda l, j: (l, 0, j))],
        out_specs=pl.BlockSpec((1, 2, tn), lambda l, j: (l, 0, j)),
        compiler_params=_params("parallel", "parallel"),
        name="adaln",
    )(cond2, ada_w, ada_b.reshape(nl, 1, n6))


def _modulated_norm(x, g, mod, shift_row, scale_row):
    ms = jnp.mean(x * x, axis=-1, keepdims=True)
    y = x * lax.rsqrt(ms + EPS) * g
    return y * (1.0 + mod[scale_row:scale_row + 1, :]) + mod[shift_row:shift_row + 1, :]


def _prenorm_kernel(x_ref, g_ref, mod_ref, h_ref, *, shift_row, scale_row):
    h = _modulated_norm(x_ref[...], g_ref[...], mod_ref[0], shift_row, scale_row)
    h_ref[...] = h.astype(BF16)


def _prenorm_router_kernel(x_ref, g_ref, mod_ref, wr_hi_ref, wr_lo_ref, h_ref, logit_ref, *,
                           shift_row, scale_row):
    h = _modulated_norm(x_ref[...], g_ref[...], mod_ref[0], shift_row, scale_row)
    h_hi, h_lo = _split_bf16(h)
    h_ref[...] = h_hi
    logit_ref[...] = (jnp.dot(h_hi, wr_hi_ref[...], preferred_element_type=F32)
                      + jnp.dot(h_lo, wr_hi_ref[...], preferred_element_type=F32)
                      + jnp.dot(h_hi, wr_lo_ref[...], preferred_element_type=F32))


def _prenorm(x, g, mods, ctx_rows, shift_row, scale_row, w_router=None):
    m, d = x.shape
    tm = ROW_BLOCK
    stream = _stream_of_block(tm, ctx_rows)
    in_specs = [pl.BlockSpec((tm, d), lambda i: (i, 0)),
                pl.BlockSpec((1, d), lambda i: (0, 0)),
                pl.BlockSpec((1, 8, d), lambda i: (stream(i), 0, 0))]
    h_spec = pl.BlockSpec((tm, d), lambda i: (i, 0))
    h_shape = jax.ShapeDtypeStruct((m, d), BF16)
    if w_router is None:
        return pl.pallas_call(
            functools.partial(_prenorm_kernel, shift_row=shift_row, scale_row=scale_row),
            out_shape=h_shape, grid=(m // tm,), in_specs=in_specs, out_specs=h_spec,
            compiler_params=_params("parallel"), name="prenorm",
        )(x, g.reshape(1, d), mods)
    return pl.pallas_call(
        functools.partial(_prenorm_router_kernel, shift_row=shift_row, scale_row=scale_row),
        out_shape=(h_shape, jax.ShapeDtypeStruct((m, LANES), F32)),
        grid=(m // tm,),
        in_specs=in_specs + [pl.BlockSpec((d, LANES), lambda i: (0, 0))] * 2,
        out_specs=(h_spec, pl.BlockSpec((tm, LANES), lambda i: (i, 0))),
        compiler_params=_params("parallel"), name="prenorm_router",
    )(x, g.reshape(1, d), mods, *_split_bf16(w_router))


def _cast_kernel(w_ref, o_ref):
    o_ref[...] = w_ref[...].astype(BF16)


def _cast_rows(w_rows, row0, n_rows):
    k = w_rows.shape[1]
    tr = _pick(n_rows, (512, 544, 256, 128))
    assert row0 % 8 == 0
    return pl.pallas_call(
        _cast_kernel,
        out_shape=jax.ShapeDtypeStruct((n_rows, k), BF16),
        grid=(n_rows // tr,),
        in_specs=[pl.BlockSpec((pl.Element(tr), pl.Element(k)),
                               lambda j: (pl.multiple_of(row0 + j * tr, 8), 0))],
        out_specs=pl.BlockSpec((tr, k), lambda j: (j, 0)),
        compiler_params=_params("parallel"), name="cast_rows",
    )(w_rows)


def _weight_spec(w, layer, rows, tn, col_block):
    if w.ndim == 2:
        return pl.BlockSpec((rows, tn), lambda i, j: (0, col_block(j)))
    return pl.BlockSpec((None, rows, tn), lambda i, j: (layer, 0, col_block(j)))


def _mm_nt_kernel(a_ref, wt_ref, o_ref):
    o_ref[...] = lax.dot_general(a_ref[...], wt_ref[...], NT_DIMS,
                                 preferred_element_type=F32).astype(o_ref.dtype)


def _matmul_nt(a, w_t, out_dtype=F32, row0=0, n_rows=None):
    m, k = a.shape
    n = w_t.shape[0] - row0 if n_rows is None else n_rows
    tm = _pick(m, (1280, 1024, 768, 512, 256))
    tn = _pick(n, (512, 256, 128))
    assert row0 % tn == 0
    return pl.pallas_call(
        _mm_nt_kernel,
        out_shape=jax.ShapeDtypeStruct((m, n), out_dtype),
        grid=(m // tm, n // tn),
        in_specs=[pl.BlockSpec((tm, k), lambda i, j: (i, 0)),
                  pl.BlockSpec((tn, k), lambda i, j: (row0 // tn + j, 0))],
        out_specs=pl.BlockSpec((tm, tn), lambda i, j: (i, j)),
        compiler_params=_params("parallel", "arbitrary"), name="matmul",
    )(a, w_t)


def _mm_resid_kernel(a_ref, w_ref, x_ref, mod_ref, o_ref, *, gate_row, ctx_rows):
    tm = a_ref.shape[0]
    acc = jnp.dot(a_ref[...], w_ref[...].astype(BF16), preferred_element_type=F32)
    row = pl.program_id(0) * tm + lax.broadcasted_iota(jnp.int32, (tm, 1), 0)
    gate = jnp.where(row < ctx_rows, mod_ref[0, gate_row:gate_row + 1, :],
                     mod_ref[1, gate_row:gate_row + 1, :])
    o_ref[...] = x_ref[...] + gate * acc


def _matmul_gated_residual(a, w, x, mods, ctx_rows, gate_row, layer=None):
    m, k = a.shape
    n = w.shape[-1]
    tm = _pick(m, (1280, 1024, 768, 512, 256))
    tn = _pick(n, (512, 256, 128))
    return pl.pallas_call(
        functools.partial(_mm_resid_kernel, gate_row=gate_row, ctx_rows=ctx_rows),
        out_shape=jax.ShapeDtypeStruct((m, n), F32),
        grid=(m // tm, n // tn),
        in_specs=[pl.BlockSpec((tm, k), lambda i, j: (i, 0)),
                  _weight_spec(w, layer, k, tn, lambda j: j),
                  pl.BlockSpec((tm, tn), lambda i, j: (i, j)),
                  pl.BlockSpec((2, 8, tn), lambda i, j: (0, 0, j))],
        out_specs=pl.BlockSpec((tm, tn), lambda i, j: (i, j)),
        compiler_params=_params("parallel", "arbitrary"), name="out_proj",
    )(a, w, x, mods)


def _merge_kernel(h_ref, a_ref, s_ref, wga_ref, wgb_ref, wa_ref, wb_ref, m_ref):
    h = h_ref[...]
    ga = lax.dot_general(h, wga_ref[...], NT_DIMS, preferred_element_type=F32)
    gb = lax.dot_general(h, wgb_ref[...], NT_DIMS, preferred_element_type=F32)
    ya = jnp.dot(a_ref[...], wa_ref[...].astype(BF16), preferred_element_type=F32)
    yb = jnp.dot(s_ref[...], wb_ref[...].astype(BF16), preferred_element_type=F32)
    m_ref[...] = (jax.nn.sigmoid(ga) * ya + jax.nn.sigmoid(gb) * yb).astype(BF16)


def _merge(h, a, s, w_gates_t, row_ga, w_a, w_b, layer):
    m, d = h.shape
    ka = a.shape[1]
    kb = s.shape[1]
    tm = _pick(m, (1280, 1024, 768, 512, 256))
    tn = _pick(d, (256, 128))
    nb = d // tn
    j0 = row_ga // tn
    return pl.pallas_call(
        _merge_kernel,
        out_shape=jax.ShapeDtypeStruct((m, d), BF16),
        grid=(m // tm, nb),
        in_specs=[pl.BlockSpec((tm, d), lambda i, j: (i, 0)),
                  pl.BlockSpec((tm, ka), lambda i, j: (i, 0)),
                  pl.BlockSpec((tm, kb), lambda i, j: (i, 0)),
                  pl.BlockSpec((tn, d), lambda i, j: (j0 + j, 0)),
                  pl.BlockSpec((tn, d), lambda i, j: (j0 + nb + j, 0)),
                  _weight_spec(w_a, layer, ka, tn, lambda j: j),
                  _weight_spec(w_b, layer, kb, tn, lambda j: j)],
        out_specs=pl.BlockSpec((tm, tn), lambda i, j: (i, j)),
        compiler_params=_params("parallel", "arbitrary"), name="merge",
    )(h, a, s, w_gates_t, w_gates_t, w_a, w_b)


def _sgu_kernel(u_ref, vs_ref, lng_ref, lnb_ref, ws_ref, bs_ref, s_ref):
    tm = u_ref.shape[0]
    u = jax.nn.gelu(u_ref[...].astype(F32))
    v = jax.nn.gelu(vs_ref[...].astype(F32))
    vc = v - jnp.mean(v, axis=-1, keepdims=True)
    vn = vc * lax.rsqrt(jnp.mean(vc * vc, axis=-1, keepdims=True) + EPS)
    vn = (vn * lng_ref[...] + lnb_ref[...]).astype(BF16)
    for ch in range(tm // SG_CHUNK):
        rows = slice(ch * SG_CHUNK, (ch + 1) * SG_CHUNK)
        for g in range(SG_GROUPS):
            cols = slice(g * SG_GROUP_CH, (g + 1) * SG_GROUP_CH)
            mixed = jnp.dot(ws_ref[g], vn[rows, cols], preferred_element_type=F32) + bs_ref[g]
            s_ref[rows, cols] = (u[rows, cols] * mixed).astype(BF16)


def _sgu(p_ruv, ln_g, ln_b, w_s, layer, b_s_bcast):
    m = p_ruv.shape[0]
    tm = ROW_BLOCK
    return pl.pallas_call(
        _sgu_kernel,
        out_shape=jax.ShapeDtypeStruct((m, SG_DIM), BF16),
        grid=(m // tm,),
        in_specs=[pl.BlockSpec((tm, SG_DIM), lambda i: (i, RUV_COL_U)),
                  pl.BlockSpec((tm, SG_DIM), lambda i: (i, RUV_COL_VS)),
                  pl.BlockSpec((1, SG_DIM), lambda i: (0, 0)),
                  pl.BlockSpec((1, SG_DIM), lambda i: (0, 0)),
                  pl.BlockSpec((None, SG_GROUPS, SG_CHUNK, SG_CHUNK), lambda i: (layer, 0, 0, 0)),
                  pl.BlockSpec((SG_GROUPS, SG_CHUNK, SG_GROUP_CH), lambda i: (0, 0, 0))],
        out_specs=pl.BlockSpec((tm, SG_DIM), lambda i: (i, 0)),
        compiler_params=_params("parallel"), name="sgu",
    )(p_ruv, p_ruv, ln_g.reshape(1, SG_DIM), ln_b.reshape(1, SG_DIM), w_s, b_s_bcast)


def _split_bf16(x):
    hi = x.astype(BF16)
    lo = (x - hi.astype(F32)).astype(BF16)
    return hi, lo


def _gla_prep_kernel(q_ref, k_ref, v_ref, dl_ref, wup_ref, bdec_ref,
                     qmf_ref, kmf_ref, qef_ref, qmb_ref, kmb_ref, qeb_ref,
                     utf_ref, utb_ref, dtot_ref):
    c = q_ref.shape[0]
    half = c // 2
    z = jnp.dot(dl_ref[...], wup_ref[...], precision=lax.Precision.HIGHEST,
                preferred_element_type=F32) + bdec_ref[...]
    g = -(jnp.maximum(-z, 0.0) + jnp.log1p(jnp.exp(-jnp.abs(z)))) * (1.0 / GLA_TAU)
    ri = lax.broadcasted_iota(jnp.int32, (c, c), 0)
    ci = lax.broadcasted_iota(jnp.int32, (c, c), 1)
    lane = lax.broadcasted_iota(jnp.int32, (1, HEAD_PAIR), 1)
    q = q_ref[...].astype(F32) * (GLA_DK ** -0.5)
    k = k_ref[...].astype(F32)
    vb = v_ref[...]

    def direction(gd, tri, mid_row, end_row, qm_ref, km_ref, qe_ref, ut_ref):
        hi, lo = _split_bf16(gd)
        tri = tri.astype(BF16)
        bc = (jnp.dot(tri, hi, preferred_element_type=F32)
              + jnp.dot(tri, lo, preferred_element_type=F32))
        rho = bc[mid_row:mid_row + 1, :]
        tot = bc[end_row:end_row + 1, :]
        e1 = bc - rho
        qm_ref[...] = (q * jnp.exp(jnp.minimum(e1, EXP_CLAMP))).astype(BF16)
        km_ref[...] = (k * jnp.exp(jnp.minimum(-e1, EXP_CLAMP))).astype(BF16)
        qe_ref[...] = (q * jnp.exp(bc)).astype(BF16)
        k2 = (k * jnp.exp(tot - bc)).astype(BF16)
        for p in range(GLA_HEADS // 2):
            kp = k2[:, p * HEAD_PAIR:(p + 1) * HEAD_PAIR]
            ut = jnp.zeros((GLA_DV, HEAD_PAIR), F32)
            for hh in range(2):
                h = 2 * p + hh
                keep = (lane < GLA_DK) if hh == 0 else (lane >= GLA_DK)
                kh = jnp.where(keep, kp, jnp.zeros_like(kp))
                ut = ut + lax.dot_general(vb[:, h * GLA_DV:(h + 1) * GLA_DV], kh, TN_DIMS,
                                          preferred_element_type=F32)
            ut_ref[0, :, p * HEAD_PAIR:(p + 1) * HEAD_PAIR] = ut
        return jnp.exp(tot)

    df = direction(g[:, :QK_DIM], ci <= ri, half - 1, c - 1, qmf_ref, kmf_ref, qef_ref, utf_ref)
    db = direction(g[:, QK_DIM:], ci >= ri, half, 0, qmb_ref, kmb_ref, qeb_ref, utb_ref)
    dtot_ref[0] = jnp.concatenate([df, db, jnp.zeros((6, QK_DIM), F32)], axis=0)


def _gla_prep(qkv, dlow, wup, bdec):
    m = qkv.shape[0]
    c = ROW_BLOCK
    nb = m // c
    qk_spec = pl.BlockSpec((c, QK_DIM), lambda i: (i, 0))
    bf_qk = jax.ShapeDtypeStruct((m, QK_DIM), BF16)
    ut_shape = jax.ShapeDtypeStruct((nb, GLA_DV, QK_DIM), F32)
    ut_spec = pl.BlockSpec((1, GLA_DV, QK_DIM), lambda i: (i, 0, 0))
    return pl.pallas_call(
        _gla_prep_kernel,
        out_shape=(bf_qk,) * 6 + (ut_shape, ut_shape, jax.ShapeDtypeStruct((nb, 8, QK_DIM), F32)),
        grid=(nb,),
        in_specs=[pl.BlockSpec((c, QK_DIM), lambda i: (i, 0)),
                  pl.BlockSpec((c, QK_DIM), lambda i: (i, 1)),
                  pl.BlockSpec((c, V_DIM), lambda i: (i, 1)),
                  pl.BlockSpec((c, LANES), lambda i: (i, 0)),
                  pl.BlockSpec((LANES, 2 * QK_DIM), lambda i: (0, 0)),
                  pl.BlockSpec((1, 2 * QK_DIM), lambda i: (0, 0))],
        out_specs=(qk_spec,) * 6 + (ut_spec, ut_spec,
                                    pl.BlockSpec((1, 8, QK_DIM), lambda i: (i, 0, 0))),
        compiler_params=_params("parallel"), name="gla_prep",
    )(qkv, qkv, qkv, dlow, wup, bdec)


def _gla_scan_kernel(utf_ref, dtf_ref, utb_ref, dtb_ref, stf_ref, stb_ref, sf_ref, sb_ref):
    @pl.when(pl.program_id(0) == 0)
    def _():
        sf_ref[...] = jnp.zeros_like(sf_ref)
        sb_ref[...] = jnp.zeros_like(sb_ref)

    sf = sf_ref[...]
    stf_ref[0] = sf.astype(BF16)
    sf_ref[...] = dtf_ref[0, 0:1, :] * sf + utf_ref[0]
    sb = sb_ref[...]
    stb_ref[0] = sb.astype(BF16)
    sb_ref[...] = dtb_ref[0, 1:2, :] * sb + utb_ref[0]


def _gla_scan(utf, utb, dtot, ctx_blocks):
    nb = utf.shape[0]

    def bwd(t):
        return jnp.where(t < ctx_blocks, ctx_blocks - 1 - t, nb + ctx_blocks - 1 - t)

    st_shape = jax.ShapeDtypeStruct((nb, GLA_DV, QK_DIM), BF16)
    blk = (1, GLA_DV, QK_DIM)
    return pl.pallas_call(
        _gla_scan_kernel,
        out_shape=(st_shape, st_shape),
        grid=(nb,),
        in_specs=[pl.BlockSpec(blk, lambda t: (t, 0, 0)),
                  pl.BlockSpec((1, 8, QK_DIM), lambda t: (t, 0, 0)),
                  pl.BlockSpec(blk, lambda t: (bwd(t), 0, 0)),
                  pl.BlockSpec((1, 8, QK_DIM), lambda t: (bwd(t), 0, 0))],
        out_specs=(pl.BlockSpec(blk, lambda t: (t, 0, 0)),
                   pl.BlockSpec(blk, lambda t: (bwd(t), 0, 0))),
        scratch_shapes=[pltpu.VMEM((GLA_DV, QK_DIM), F32), pltpu.VMEM((GLA_DV, QK_DIM), F32)],
        compiler_params=_params("arbitrary"), name="gla_scan",
    )(utf, dtot, utb, dtot)


def _gla_out_kernel(qmf_ref, kmf_ref, qef_ref, qmb_ref, kmb_ref, qeb_ref, vb_ref,
                    stf_ref, stb_ref, r_ref, gain_ref, a_ref):
    c = qmf_ref.shape[0]
    ri = lax.broadcasted_iota(jnp.int32, (c, c), 0)
    ci = lax.broadcasted_iota(jnp.int32, (c, c), 1)
    lower = ci <= ri
    upper = ci >= ri
    lane = lax.broadcasted_iota(jnp.int32, (1, HEAD_PAIR), 1)
    for p in range(GLA_HEADS // 2):
        pair = slice(p * HEAD_PAIR, (p + 1) * HEAD_PAIR)
        qmf, kmf, qef = qmf_ref[:, pair], kmf_ref[:, pair], qef_ref[:, pair]
        qmb, kmb, qeb = qmb_ref[:, pair], kmb_ref[:, pair], qeb_ref[:, pair]
        stf = stf_ref[0, :, pair]
        stb = stb_ref[0, :, pair]
        for hh in range(2):
            h = 2 * p + hh
            keep = (lane < GLA_DK) if hh == 0 else (lane >= GLA_DK)
            zero = jnp.zeros_like(kmf)
            af = lax.dot_general(qmf, jnp.where(keep, kmf, zero), NT_DIMS, preferred_element_type=F32)
            ab = lax.dot_general(qmb, jnp.where(keep, kmb, zero), NT_DIMS, preferred_element_type=F32)
            att = (jnp.where(lower, af, 0.0) + jnp.where(upper, ab, 0.0)).astype(BF16)
            cols = slice(h * GLA_DV, (h + 1) * GLA_DV)
            o = jnp.dot(att, vb_ref[:, cols], preferred_element_type=F32)
            o = o + lax.dot_general(jnp.where(keep, qef, zero), stf, NT_DIMS, preferred_element_type=F32)
            o = o + lax.dot_general(jnp.where(keep, qeb, zero), stb, NT_DIMS, preferred_element_type=F32)
            o = o * lax.rsqrt(jnp.mean(o * o, axis=-1, keepdims=True) + EPS)
            r = r_ref[:, cols].astype(F32)
            a_ref[:, cols] = (o * gain_ref[:, cols] * (r * jax.nn.sigmoid(r))).astype(BF16)


def _gla_out(prep, qkv, stf, stb, p_ruv, gain):
    qmf, kmf, qef, qmb, kmb, qeb = prep
    m = qkv.shape[0]
    c = ROW_BLOCK
    qk_spec = pl.BlockSpec((c, QK_DIM), lambda i: (i, 0))
    st_spec = pl.BlockSpec((1, GLA_DV, QK_DIM), lambda i: (i, 0, 0))
    return pl.pallas_call(
        _gla_out_kernel,
        out_shape=jax.ShapeDtypeStruct((m, V_DIM), BF16),
        grid=(m // c,),
        in_specs=[qk_spec] * 6 + [pl.BlockSpec((c, V_DIM), lambda i: (i, 1)), st_spec, st_spec,
                                  pl.BlockSpec((c, V_DIM), lambda i: (i, RUV_COL_R)),
                                  pl.BlockSpec((1, V_DIM), lambda i: (0, 0))],
        out_specs=pl.BlockSpec((c, V_DIM), lambda i: (i, 0)),
        compiler_params=_params("parallel"), name="gla_out",
    )(qmf, kmf, qef, qmb, kmb, qeb, qkv, stf, stb, p_ruv, gain.reshape(1, V_DIM))


ONE_F32_BITS = 0x3F800000
SEARCH_STEPS = 31


def _route_kernel(logit_ref, g_ref, *, ctx_rows):
    m = logit_ref.shape[0]
    rb = ROW_BLOCK
    lane = lax.broadcasted_iota(jnp.int32, (1, LANES), 1)
    valid = lane < N_EXPERTS

    def softmax_block(i, carry):
        r0 = pl.multiple_of(i * rb, rb)
        lg = jnp.where(valid, logit_ref[pl.ds(r0, rb), :], -jnp.inf)
        ex = jnp.exp(lg - jnp.max(lg, axis=-1, keepdims=True))
        g_ref[pl.ds(r0, rb), :] = ex / jnp.sum(ex, axis=-1, keepdims=True)
        return carry

    lax.fori_loop(0, m // rb, softmax_block, 0)

    def bits_of(r0):
        return lax.bitcast_convert_type(g_ref[pl.ds(r0, rb), :], jnp.int32)

    def count_ge(blk0, nblk, t):
        def body(i, acc):
            r0 = pl.multiple_of((blk0 + i) * rb, rb)
            hit = jnp.where(bits_of(r0) >= t, 1, 0)
            return acc + jnp.sum(hit.reshape(rb // 8, 8, LANES), axis=0)

        acc = lax.fori_loop(0, nblk, body, jnp.zeros((8, LANES), jnp.int32))
        return jnp.sum(acc, axis=0, keepdims=True)

    def route_set(blk0, nblk):
        cap = EC_CAPACITY * nblk * rb // N_EXPERTS

        def halve(_, lohi):
            lo, hi = lohi
            mid = lo + ((hi - lo) >> 1)
            ok = count_ge(blk0, nblk, mid) >= cap
            return jnp.where(ok, mid, lo), jnp.where(ok, hi, mid)

        thr, _ = lax.fori_loop(0, SEARCH_STEPS, halve,
                               (jnp.zeros((1, LANES), jnp.int32),
                                jnp.full((1, LANES), ONE_F32_BITS + 1, jnp.int32)))
        need = (cap - count_ge(blk0, nblk, thr + 1)).astype(F32)
        ri = lax.broadcasted_iota(jnp.int32, (rb, rb), 0)
        ci = lax.broadcasted_iota(jnp.int32, (rb, rb), 1)
        earlier = jnp.where(ci < ri, 1.0, 0.0).astype(BF16)

        def finalize(i, seen):
            r0 = pl.multiple_of((blk0 + i) * rb, rb)
            aff = g_ref[pl.ds(r0, rb), :]
            bits = lax.bitcast_convert_type(aff, jnp.int32)
            tie = bits == thr
            tie_f = jnp.where(tie, 1.0, 0.0)
            rank = seen + jnp.dot(earlier, tie_f.astype(BF16), preferred_element_type=F32)
            g_ref[pl.ds(r0, rb), :] = jnp.where(tie, jnp.where(rank < need, aff, 0.0),
                                                jnp.where(bits > thr, aff, 0.0))
            return seen + jnp.sum(tie_f, axis=0, keepdims=True)

        lax.fori_loop(0, nblk, finalize, jnp.zeros((1, LANES), F32))

    route_set(0, ctx_rows // rb)
    route_set(ctx_rows // rb, (m - ctx_rows) // rb)


def _route(logits, ctx_rows):
    m = logits.shape[0]
    return pl.pallas_call(
        functools.partial(_route_kernel, ctx_rows=ctx_rows),
        out_shape=jax.ShapeDtypeStruct((m, LANES), F32),
        grid=(1,),
        in_specs=[pl.BlockSpec((m, LANES), lambda i: (0, 0))],
        out_specs=pl.BlockSpec((m, LANES), lambda i: (0, 0)),
        compiler_params=_params("arbitrary"), name="route",
    )(logits)


SLOTS_PER_BLOCK = 64
GATE_PARTS = 3


def _slots_kernel(g_ref, pos_ref, post_ref, cnt_ref):
    rb = g_ref.shape[0]
    picked = jnp.where(g_ref[...] > 0.0, 1.0, 0.0)
    ri = lax.broadcasted_iota(jnp.int32, (rb, rb), 0)
    ci = lax.broadcasted_iota(jnp.int32, (rb, rb), 1)
    earlier = jnp.where(ci < ri, 1.0, 0.0).astype(BF16)
    rank = jnp.dot(earlier, picked.astype(BF16), preferred_element_type=F32)
    pos = jnp.where(picked > 0.0, rank, -1.0)
    pos_ref[...] = pos
    post_ref[...] = pos.T[:N_EXPERTS, :]
    cnt_ref[0] = jnp.broadcast_to(jnp.sum(picked, axis=0, keepdims=True), (8, LANES))


def _slots(gates):
    m = gates.shape[0]
    rb = ROW_BLOCK
    nb = m // rb
    return pl.pallas_call(
        _slots_kernel,
        out_shape=(jax.ShapeDtypeStruct((m, LANES), F32),
                   jax.ShapeDtypeStruct((N_EXPERTS, m), F32),
                   jax.ShapeDtypeStruct((nb, 8, LANES), F32)),
        grid=(nb,),
        in_specs=[pl.BlockSpec((rb, LANES), lambda i: (i, 0))],
        out_specs=(pl.BlockSpec((rb, LANES), lambda i: (i, 0)),
                   pl.BlockSpec((N_EXPERTS, rb), lambda i: (0, i)),
                   pl.BlockSpec((1, 8, LANES), lambda i: (i, 0, 0))),
        compiler_params=_params("parallel"), name="moe_slots",
    )(gates)


def _slot_onehot(post):
    rb = post.shape[1]
    slot = lax.broadcasted_iota(jnp.int32, (SLOTS_PER_BLOCK, rb), 0).astype(F32)
    parts = [jnp.where(slot == post[e:e + 1, :], 1.0, 0.0) for e in range(N_EXPERTS)]
    return jnp.concatenate(parts, axis=0).astype(BF16)


def _dispatch_kernel(h_ref, post_ref, g_ref, xs_ref, gs_ref):
    onehot = _slot_onehot(post_ref[...])
    d = h_ref.shape[1]
    xs = jnp.dot(onehot, h_ref[...], preferred_element_type=F32)
    xs_ref[...] = xs.astype(BF16).reshape(N_EXPERTS, SLOTS_PER_BLOCK, d)
    g = g_ref[...]
    pieces = jnp.zeros_like(g)
    for k in range(GATE_PARTS):
        piece = g.astype(BF16).astype(F32)
        pieces = pieces + (pltpu.roll(piece, k * N_EXPERTS, axis=1) if k else piece)
        g = g - piece
    gs = jnp.dot(onehot, pieces.astype(BF16), preferred_element_type=F32)
    gs_ref[...] = gs.reshape(N_EXPERTS, SLOTS_PER_BLOCK, LANES)


def _dispatch(h, post, gates):
    m, d = h.shape
    rb = ROW_BLOCK
    nb = m // rb
    s = SLOTS_PER_BLOCK
    return pl.pallas_call(
        _dispatch_kernel,
        out_shape=(jax.ShapeDtypeStruct((N_EXPERTS, nb * s, d), BF16),
                   jax.ShapeDtypeStruct((N_EXPERTS, nb * s, LANES), F32)),
        grid=(nb,),
        in_specs=[pl.BlockSpec((rb, d), lambda i: (i, 0)),
                  pl.BlockSpec((N_EXPERTS, rb), lambda i: (0, i)),
                  pl.BlockSpec((rb, LANES), lambda i: (i, 0))],
        out_specs=(pl.BlockSpec((N_EXPERTS, s, d), lambda i: (0, i, 0)),
                   pl.BlockSpec((N_EXPERTS, s, LANES), lambda i: (0, i, 0))),
        compiler_params=_params("parallel"), name="moe_dispatch",
    )(h, post, gates)


def _expert_kernel(xs_ref, w1_ref, w3_ref, w2_ref, gs_ref, y_ref):
    e = pl.program_id(0)
    lane = lax.broadcasted_iota(jnp.int32, (1, LANES), 1)
    mine = (lane % N_EXPERTS == e) & (lane < GATE_PARTS * N_EXPERTS)
    gate = jnp.sum(jnp.where(mine, gs_ref[0], 0.0), axis=-1, keepdims=True)
    x = xs_ref[0]
    h1 = jnp.dot(x, w1_ref[0], preferred_element_type=F32)
    h3 = jnp.dot(x, w3_ref[0], preferred_element_type=F32)
    hid = (h1 * jax.nn.sigmoid(h1) * h3 * gate).astype(BF16)
    y_ref[0] = jnp.dot(hid, w2_ref[0], preferred_element_type=F32).astype(BF16)


def _expert_ffn(xs, gs, w1, w3, w2, layer):
    e, rows, d = xs.shape
    f = w1.shape[-1]
    tr = _pick(rows, (832, 1024, 512, 256, 128, 64, 16))
    return pl.pallas_call(
        _expert_kernel,
        out_shape=jax.ShapeDtypeStruct((e, rows, d), BF16),
        grid=(e, rows // tr),
        in_specs=[pl.BlockSpec((1, tr, d), lambda i, j: (i, j, 0)),
                  pl.BlockSpec((None, 1, d, f), lambda i, j: (layer, i, 0, 0)),
                  pl.BlockSpec((None, 1, d, f), lambda i, j: (layer, i, 0, 0)),
                  pl.BlockSpec((None, 1, f, d), lambda i, j: (layer, i, 0, 0)),
                  pl.BlockSpec((1, tr, LANES), lambda i, j: (i, j, 0))],
        out_specs=pl.BlockSpec((1, tr, d), lambda i, j: (i, j, 0)),
        compiler_params=_params("parallel", "arbitrary"), name="expert_ffn",
    )(xs, w1, w3, w2, gs)


def _combined_rows(ys_ref, post_ref, x_ref, mod_ref, gate_row):
    onehot = _slot_onehot(post_ref[...])
    d = x_ref.shape[1]
    ys = ys_ref[...].reshape(N_EXPERTS * SLOTS_PER_BLOCK, d)
    moe = lax.dot_general(onehot, ys, TN_DIMS, preferred_element_type=F32)
    return x_ref[...] + mod_ref[0, gate_row:gate_row + 1, :] * moe


def _combine_prenorm_kernel(ys_ref, post_ref, x_ref, mod_ref, g_ref, nmod_ref, o_ref, h_ref, *,
                            gate_row):
    x = _combined_rows(ys_ref, post_ref, x_ref, mod_ref, gate_row)
    o_ref[...] = x
    h_ref[...] = _modulated_norm(x, g_ref[...], nmod_ref[0], 0, 1).astype(BF16)


def _combine_final_kernel(ys_ref, post_ref, x_ref, mod_ref, g_ref, o_ref, *, gate_row):
    x = _combined_rows(ys_ref, post_ref, x_ref, mod_ref, gate_row)
    ms = jnp.mean(x * x, axis=-1, keepdims=True)
    o_ref[...] = x * lax.rsqrt(ms + EPS) * g_ref[...]


def _combine(ys, post, x, mods, ctx_rows, gate_row, norm_g, next_mods=None):
    m, d = x.shape
    rb = ROW_BLOCK
    s = SLOTS_PER_BLOCK
    stream = _stream_of_block(rb, ctx_rows)
    row_spec = pl.BlockSpec((rb, d), lambda i: (i, 0))
    mod_spec = pl.BlockSpec((1, 8, d), lambda i: (stream(i), 0, 0))
    in_specs = [pl.BlockSpec((N_EXPERTS, s, d), lambda i: (0, i, 0)),
                pl.BlockSpec((N_EXPERTS, rb), lambda i: (0, i)),
                row_spec, mod_spec, pl.BlockSpec((1, d), lambda i: (0, 0))]
    if next_mods is not None:
        return pl.pallas_call(
            functools.partial(_combine_prenorm_kernel, gate_row=gate_row),
            out_shape=(jax.ShapeDtypeStruct((m, d), F32), jax.ShapeDtypeStruct((m, d), BF16)),
            grid=(m // rb,),
            in_specs=in_specs + [mod_spec],
            out_specs=(row_spec, row_spec),
            compiler_params=_params("parallel"), name="moe_combine",
        )(ys, post, x, mods, norm_g.reshape(1, d), next_mods)
    skip = ctx_rows // rb
    return pl.pallas_call(
        functools.partial(_combine_final_kernel, gate_row=gate_row),
        out_shape=jax.ShapeDtypeStruct((m - ctx_rows, d), F32),
        grid=(m // rb,),
        in_specs=in_specs,
        out_specs=pl.BlockSpec((rb, d), lambda i: (jnp.maximum(i - skip, 0), 0)),
        compiler_params=_params("arbitrary"), name="moe_combine_final",
    )(ys, post, x, mods, norm_g.reshape(1, d))


def _moe_hidden_kernel(h_ref, w1_ref, w3_ref, g_ref, o_ref):
    e = pl.program_id(1)
    lane = lax.broadcasted_iota(jnp.int32, (1, LANES), 1)
    gate = jnp.sum(jnp.where(lane == e, g_ref[...], 0.0), axis=-1, keepdims=True)
    h = h_ref[...]
    h1 = jnp.dot(h, w1_ref[0].astype(BF16), preferred_element_type=F32)
    h3 = jnp.dot(h, w3_ref[0].astype(BF16), preferred_element_type=F32)
    o_ref[...] = (h1 * jax.nn.sigmoid(h1) * h3 * gate).astype(BF16)


def _moe_hidden(h, gates, w1, w3, layer):
    m, d = h.shape
    _, e, _, f = w1.shape
    tm = _pick(m, (1280, 1024, 768, 512, 256))
    return pl.pallas_call(
        _moe_hidden_kernel,
        out_shape=jax.ShapeDtypeStruct((m, e * f), BF16),
        grid=(m // tm, e),
        in_specs=[pl.BlockSpec((tm, d), lambda i, j: (i, 0)),
                  pl.BlockSpec((None, 1, d, f), lambda i, j: (layer, j, 0, 0)),
                  pl.BlockSpec((None, 1, d, f), lambda i, j: (layer, j, 0, 0)),
                  pl.BlockSpec((tm, LANES), lambda i, j: (i, 0))],
        out_specs=pl.BlockSpec((tm, f), lambda i, j: (i, j)),
        compiler_params=_params("parallel", "arbitrary"), name="moe_hidden",
    )(h, w1, w3, gates)


def kernel(x, c, ctx, c_ctx, ada_w, ada_b, norm1_g, norm2_g, w_in, w_dec_up, b_dec, gla_norm_g,
           sg_ln_g, sg_ln_b, sg_w, sg_b, w_branch_a, w_branch_b, w_out, w_router, w_exp1, w_exp3,
           w_exp2, final_g):
    batch, seq, d = x.shape
    ctx_rows = ctx.shape[1]
    depth = ada_w.shape[0]
    assert batch == 1 and seq % ROW_BLOCK == 0 and ctx_rows % ROW_BLOCK == 0

    xa = jnp.concatenate([ctx[0], x[0]], axis=0)
    cond2 = jnp.stack([c_ctx, c[0]], axis=1)
    mods_all = _adaln(cond2, ada_w, ada_b).reshape(depth, 2, 6, d)
    mods_all = jnp.pad(mods_all, ((0, 0), (0, 0), (0, 2), (0, 0)))

    w1_all, w3_all, w2_all = (w.astype(BF16) for w in (w_exp1, w_exp3, w_exp2))
    w2_flat = w2_all.reshape(depth, N_EXPERTS * EXPERT_FF, d)
    sg_w_all = sg_w.astype(BF16)
    n_in = w_in.shape[2]
    w_in_rows = jnp.swapaxes(w_in, 1, 2).reshape(depth * n_in, d)

    for l in range(depth):
        mods = mods_all[l]
        w_head_t = _cast_rows(w_in_rows, l * n_in, OFF_DF + LANES)
        w_rest_t = _cast_rows(w_in_rows, l * n_in + SCAN_COLS, n_in - SCAN_COLS)
        wup = jnp.zeros((LANES, 2 * QK_DIM), F32)
        wup = wup.at[:DECAY_RANK, :QK_DIM].set(w_dec_up[l, 0])
        wup = wup.at[DECAY_RANK:2 * DECAY_RANK, QK_DIM:].set(w_dec_up[l, 1])
        bdec = b_dec[l].reshape(1, 2 * QK_DIM)
        b_s_bcast = jnp.broadcast_to(sg_b[l][:, :, None], (SG_GROUPS, SG_CHUNK, SG_GROUP_CH))
        w_r = jnp.pad(w_router[l], ((0, 0), (0, LANES - N_EXPERTS)))

        if l == 0:
            h = _prenorm(xa, norm1_g[l], mods, ctx_rows, shift_row=0, scale_row=1)
        p_qkv = _matmul_nt(h, w_head_t, out_dtype=BF16, n_rows=OFF_DF)
        p_ruv = _matmul_nt(h, w_rest_t, out_dtype=BF16, n_rows=3 * V_DIM)
        dlow = _matmul_nt(h, w_head_t, row0=OFF_DF)

        prep = _gla_prep(p_qkv, dlow, wup, bdec)
        stf, stb = _gla_scan(prep[6], prep[7], prep[8], ctx_rows // ROW_BLOCK)
        a = _gla_out(prep[:6], p_qkv, stf, stb, p_ruv, gla_norm_g[l])
        s = _sgu(p_ruv, sg_ln_g[l], sg_ln_b[l], sg_w_all, l, b_s_bcast)
        mix = _merge(h, a, s, w_rest_t, 3 * V_DIM, w_branch_a, w_branch_b, l)
        xa = _matmul_gated_residual(mix, w_out, xa, mods, ctx_rows, gate_row=2, layer=l)

        h2, logits = _prenorm(xa, norm2_g[l], mods, ctx_rows, shift_row=3, scale_row=4, w_router=w_r)
        gates = _route(logits, ctx_rows)
        pos, post, cnt = _slots(gates)
        xs, gs = _dispatch(h2, post, gates)
        ys = _expert_ffn(xs, gs, w1_all, w3_all, w2_all, l)

        def combine(xa, l=l, ys=ys, post=post, mods=mods):
            if l + 1 < depth:
                return _combine(ys, post, xa, mods, ctx_rows, 5, norm1_g[l + 1], mods_all[l + 1])
            return _combine(ys, post, xa, mods, ctx_rows, 5, final_g)

        def combine_after_overflow(xa, l=l, h2=h2, gates=gates, pos=pos, mods=mods, combine=combine):
            late = jnp.where(pos >= SLOTS_PER_BLOCK, gates, 0.0)
            hid = _moe_hidden(h2, late, w1_all, w3_all, l)
            return combine(_matmul_gated_residual(hid, w2_flat, xa, mods, ctx_rows, gate_row=5, layer=l))

        res = lax.cond(jnp.max(cnt) > SLOTS_PER_BLOCK, combine_after_overflow, combine, xa)
        if l + 1 < depth:
            xa, h = res
        else:
            out = res

    return out[None]
```

```python
import functools

import jax
import jax.numpy as jnp
from jax import lax
from jax.experimental import pallas as pl
from jax.experimental.pallas import tpu as pltpu

EPS = 1e-6
GLA_HEADS = 8
GLA_DK = 64
GLA_DV = 128
DECAY_RANK = 16
GLA_TAU = 16.0
SG_GROUPS = 8
SG_GROUP_CH = 128
SG_CHUNK = 128
N_EXPERTS = 16
EXPERT_FF = 256
EC_CAPACITY = 2

QK_DIM = GLA_HEADS * GLA_DK
V_DIM = GLA_HEADS * GLA_DV
SG_DIM = SG_GROUPS * SG_GROUP_CH
OFF_K = QK_DIM
OFF_V = 2 * QK_DIM
OFF_DF = OFF_V + V_DIM
OFF_DB = OFF_DF + DECAY_RANK
SCAN_COLS = OFF_DB + DECAY_RANK
OFF_U = SCAN_COLS + V_DIM
OFF_VS = OFF_U + SG_DIM
OFF_GA = OFF_VS + SG_DIM

RUV_COL_R, RUV_COL_U, RUV_COL_VS = 0, 1, 2

LANES = 128
ROW_BLOCK = 256
HEAD_PAIR = 2 * GLA_DK
EXP_CLAMP = 80.0
VMEM_LIMIT_BYTES = 52 * 1024 * 1024

F32 = jnp.float32
BF16 = jnp.bfloat16
NT_DIMS = (((1,), (1,)), ((), ()))
TN_DIMS = (((0,), (0,)), ((), ()))


def _pick(n, candidates):
    for c in candidates:
        if n % c == 0:
            return c
    raise ValueError(f"no tile of {candidates} divides {n}")


def _params(*semantics):
    return pltpu.CompilerParams(dimension_semantics=semantics,
                                vmem_limit_bytes=VMEM_LIMIT_BYTES)


def _stream_of_block(tm, ctx_rows):
    return lambda i: (i * tm >= ctx_rows).astype(jnp.int32)


def _adaln_kernel(cond_ref, w_ref, b_ref, out_ref, *, rows):
    d = cond_ref.shape[0]
    tn = out_ref.shape[-1]

    def body(i, acc):
        r0 = pl.multiple_of(i * rows, rows)
        cond = cond_ref[pl.ds(r0, rows), :]
        s = cond * jax.nn.sigmoid(cond)
        w = w_ref[0, pl.ds(r0, rows), :]
        a0 = jnp.sum(w * s[:, 0:1], axis=0, keepdims=True)
        a1 = jnp.sum(w * s[:, 1:2], axis=0, keepdims=True)
        return acc + jnp.concatenate([a0, a1], axis=0)

    acc = lax.fori_loop(0, d // rows, body, jnp.zeros((2, tn), F32))
    out_ref[0] = acc + b_ref[0]


def _adaln(cond2, ada_w, ada_b):
    nl, d, n6 = ada_w.shape
    tn = _pick(n6, (1024, 512, 256, 128))
    return pl.pallas_call(
        functools.partial(_adaln_kernel, rows=_pick(d, (256, 128, 8))),
        out_shape=jax.ShapeDtypeStruct((nl, 2, n6), F32),
        grid=(nl, n6 // tn),
        in_specs=[pl.BlockSpec((d, 2), lambda l, j: (0, 0)),
                  pl.BlockSpec((1, d, tn), lambda l, j: (l, 0, j)),
                  pl.BlockSpec((1, 1, tn), lambda l, j: (l, 0, j))],
        out_specs=pl.BlockSpec((1, 2, tn), lambda l, j: (l, 0, j)),
        compiler_params=_params("parallel", "parallel"),
        name="adaln",
    )(cond2, ada_w, ada_b.reshape(nl, 1, n6))


def _modulated_norm(x, g, mod, shift_row, scale_row):
    ms = jnp.mean(x * x, axis=-1, keepdims=True)
    y = x * lax.rsqrt(ms + EPS) * g
    return y * (1.0 + mod[scale_row:scale_row + 1, :]) + mod[shift_row:shift_row + 1, :]


def _stack_prenorm_kernel(ctx_ref, x_ref, g_ref, mod_ref, xa_ref, h_ref, *, ctx_blocks):
    xin = jnp.where(pl.program_id(0) < ctx_blocks, ctx_ref[...], x_ref[...])
    xa_ref[...] = xin
    h_ref[...] = _modulated_norm(xin, g_ref[...], mod_ref[0], 0, 1).astype(BF16)


def _stack_prenorm(ctx2d, x2d, g, mods):
    ctx_rows, d = ctx2d.shape
    m = ctx_rows + x2d.shape[0]
    tm = ROW_BLOCK
    nc = ctx_rows // tm
    row_spec = pl.BlockSpec((tm, d), lambda i: (i, 0))
    return pl.pallas_call(
        functools.partial(_stack_prenorm_kernel, ctx_blocks=nc),
        out_shape=(jax.ShapeDtypeStruct((m, d), F32), jax.ShapeDtypeStruct((m, d), BF16)),
        grid=(m // tm,),
        in_specs=[pl.BlockSpec((tm, d), lambda i: (jnp.minimum(i, nc - 1), 0)),
                  pl.BlockSpec((tm, d), lambda i: (jnp.maximum(i - nc, 0), 0)),
                  pl.BlockSpec((1, d), lambda i: (0, 0)),
                  pl.BlockSpec((1, 8, d), lambda i: (_stream_of_block(tm, ctx_rows)(i), 0, 0))],
        out_specs=(row_spec, row_spec),
        compiler_params=_params("parallel"), name="stack_prenorm",
    )(ctx2d, x2d, g.reshape(1, d), mods)


def _prenorm_router_kernel(x_ref, g_ref, mod_ref, wr_ref, h_ref, logit_ref, *, shift_row, scale_row):
    h = _modulated_norm(x_ref[...], g_ref[...], mod_ref[0], shift_row, scale_row)
    h_hi, h_lo = _split_bf16(h)
    h_ref[...] = h_hi
    hi = jnp.dot(h_hi, wr_ref[...], preferred_element_type=F32)
    lo = jnp.dot(h_lo, wr_ref[:, :LANES], preferred_element_type=F32)
    logit_ref[...] = hi[:, :LANES] + hi[:, LANES:] + lo


def _prenorm_router(x, g, mods, ctx_rows, shift_row, scale_row, w_router):
    m, d = x.shape
    tm = ROW_BLOCK
    stream = _stream_of_block(tm, ctx_rows)
    row_spec = pl.BlockSpec((tm, d), lambda i: (i, 0))
    return pl.pallas_call(
        functools.partial(_prenorm_router_kernel, shift_row=shift_row, scale_row=scale_row),
        out_shape=(jax.ShapeDtypeStruct((m, d), BF16), jax.ShapeDtypeStruct((m, LANES), F32)),
        grid=(m // tm,),
        in_specs=[row_spec,
                  pl.BlockSpec((1, d), lambda i: (0, 0)),
                  pl.BlockSpec((1, 8, d), lambda i: (stream(i), 0, 0)),
                  pl.BlockSpec((d, 2 * LANES), lambda i: (0, 0))],
        out_specs=(row_spec, pl.BlockSpec((tm, LANES), lambda i: (i, 0))),
        compiler_params=_params("parallel"), name="prenorm_router",
    )(x, g.reshape(1, d), mods, jnp.concatenate(_split_bf16(w_router), axis=1))


def _cast_kernel(w_ref, o_ref):
    o_ref[...] = w_ref[...].astype(BF16)


def _cast_rows(w_rows, row0, n_rows):
    k = w_rows.shape[1]
    tr = _pick(n_rows, (512, 544, 256, 128))
    assert row0 % 8 == 0
    return pl.pallas_call(
        _cast_kernel,
        out_shape=jax.ShapeDtypeStruct((n_rows, k), BF16),
        grid=(n_rows // tr,),
        in_specs=[pl.BlockSpec((pl.Element(tr), pl.Element(k)),
                               lambda j: (pl.multiple_of(row0 + j * tr, 8), 0))],
        out_specs=pl.BlockSpec((tr, k), lambda j: (j, 0)),
        compiler_params=_params("parallel"), name="cast_rows",
    )(w_rows)


def _weight_spec(w, layer, rows, tn, col_block):
    if w.ndim == 2:
        return pl.BlockSpec((rows, tn), lambda i, j: (0, col_block(j)))
    return pl.BlockSpec((None, rows, tn), lambda i, j: (layer, 0, col_block(j)))


def _mm_nt_kernel(a_ref, wt_ref, o_ref):
    o_ref[...] = lax.dot_general(a_ref[...], wt_ref[...], NT_DIMS,
                                 preferred_element_type=F32).astype(o_ref.dtype)


def _matmul_nt(a, w_t, n_rows, out_dtype):
    m, k = a.shape
    n = n_rows
    tm = _pick(m, (1280, 1024, 768, 512, 256))
    tn = _pick(n, (512, 256, 128))
    return pl.pallas_call(
        _mm_nt_kernel,
        out_shape=jax.ShapeDtypeStruct((m, n), out_dtype),
        grid=(m // tm, n // tn),
        in_specs=[pl.BlockSpec((tm, k), lambda i, j: (i, 0)),
                  pl.BlockSpec((tn, k), lambda i, j: (j, 0))],
        out_specs=pl.BlockSpec((tm, tn), lambda i, j: (i, j)),
        compiler_params=_params("parallel", "arbitrary"), name="matmul",
    )(a, w_t)


def _mm_nt_side_kernel(a_ref, wt_ref, ws_ref, o_ref, side_ref):
    a = a_ref[...]
    o_ref[...] = lax.dot_general(a, wt_ref[...], NT_DIMS, preferred_element_type=F32).astype(o_ref.dtype)

    @pl.when(pl.program_id(1) == 0)
    def _():
        side_ref[...] = lax.dot_general(a, ws_ref[...], NT_DIMS, preferred_element_type=F32)


def _matmul_nt_with_side(a, w_t, n_rows, out_dtype):
    m, k = a.shape
    tm = _pick(m, (1280, 1024, 768, 512, 256))
    tn = _pick(n_rows, (512, 256, 128))
    assert w_t.shape[0] - n_rows == LANES and n_rows % LANES == 0
    return pl.pallas_call(
        _mm_nt_side_kernel,
        out_shape=(jax.ShapeDtypeStruct((m, n_rows), out_dtype), jax.ShapeDtypeStruct((m, LANES), F32)),
        grid=(m // tm, n_rows // tn),
        in_specs=[pl.BlockSpec((tm, k), lambda i, j: (i, 0)),
                  pl.BlockSpec((tn, k), lambda i, j: (j, 0)),
                  pl.BlockSpec((LANES, k), lambda i, j: (n_rows // LANES, 0))],
        out_specs=(pl.BlockSpec((tm, tn), lambda i, j: (i, j)),
                   pl.BlockSpec((tm, LANES), lambda i, j: (i, 0))),
        compiler_params=_params("parallel", "arbitrary"), name="matmul_side",
    )(a, w_t, w_t)


def _mm_resid_kernel(a_ref, w_ref, x_ref, mod_ref, o_ref, *, gate_row, ctx_rows):
    tm = a_ref.shape[0]
    acc = jnp.dot(a_ref[...], w_ref[...].astype(BF16), preferred_element_type=F32)
    row = pl.program_id(0) * tm + lax.broadcasted_iota(jnp.int32, (tm, 1), 0)
    gate = jnp.where(row < ctx_rows, mod_ref[0, gate_row:gate_row + 1, :],
                     mod_ref[1, gate_row:gate_row + 1, :])
    o_ref[...] = x_ref[...] + gate * acc


def _matmul_gated_residual(a, w, x, mods, ctx_rows, gate_row, layer=None):
    m, k = a.shape
    n = w.shape[-1]
    tm = _pick(m, (1280, 1024, 768, 512, 256))
    tn = _pick(n, (512, 256, 128))
    return pl.pallas_call(
        functools.partial(_mm_resid_kernel, gate_row=gate_row, ctx_rows=ctx_rows),
        out_shape=jax.ShapeDtypeStruct((m, n), F32),
        grid=(m // tm, n // tn),
        in_specs=[pl.BlockSpec((tm, k), lambda i, j: (i, 0)),
                  _weight_spec(w, layer, k, tn, lambda j: j),
                  pl.BlockSpec((tm, tn), lambda i, j: (i, j)),
                  pl.BlockSpec((2, 8, tn), lambda i, j: (0, 0, j))],
        out_specs=pl.BlockSpec((tm, tn), lambda i, j: (i, j)),
        compiler_params=_params("parallel", "arbitrary"), name="out_proj",
    )(a, w, x, mods)


def _merge_kernel(h_ref, a_ref, s_ref, wga_ref, wgb_ref, wa_ref, wb_ref, m_ref):
    h = h_ref[...]
    ga = lax.dot_general(h, wga_ref[...], NT_DIMS, preferred_element_type=F32)
    gb = lax.dot_general(h, wgb_ref[...], NT_DIMS, preferred_element_type=F32)
    ya = jnp.dot(a_ref[...], wa_ref[...].astype(BF16), preferred_element_type=F32)
    yb = jnp.dot(s_ref[...], wb_ref[...].astype(BF16), preferred_element_type=F32)
    m_ref[...] = (jax.nn.sigmoid(ga) * ya + jax.nn.sigmoid(gb) * yb).astype(BF16)


def _merge(h, a, s, w_gates_t, row_ga, w_a, w_b, layer):
    m, d = h.shape
    ka = a.shape[1]
    kb = s.shape[1]
    tm = _pick(m, (1280, 1024, 768, 512, 256))
    tn = _pick(d, (256, 128))
    nb = d // tn
    j0 = row_ga // tn
    return pl.pallas_call(
        _merge_kernel,
        out_shape=jax.ShapeDtypeStruct((m, d), BF16),
        grid=(m // tm, nb),
        in_specs=[pl.BlockSpec((tm, d), lambda i, j: (i, 0)),
                  pl.BlockSpec((tm, ka), lambda i, j: (i, 0)),
                  pl.BlockSpec((tm, kb), lambda i, j: (i, 0)),
                  pl.BlockSpec((tn, d), lambda i, j: (j0 + j, 0)),
                  pl.BlockSpec((tn, d), lambda i, j: (j0 + nb + j, 0)),
                  _weight_spec(w_a, layer, ka, tn, lambda j: j),
                  _weight_spec(w_b, layer, kb, tn, lambda j: j)],
        out_specs=pl.BlockSpec((tm, tn), lambda i, j: (i, j)),
        compiler_params=_params("parallel", "arbitrary"), name="merge",
    )(h, a, s, w_gates_t, w_gates_t, w_a, w_b)


def _sgu_kernel(u_ref, vs_ref, lng_ref, lnb_ref, ws_ref, bs_ref, s_ref):
    tm = u_ref.shape[0]
    u = jax.nn.gelu(u_ref[...].astype(F32))
    v = jax.nn.gelu(vs_ref[...].astype(F32))
    vc = v - jnp.mean(v, axis=-1, keepdims=True)
    vn = vc * lax.rsqrt(jnp.mean(vc * vc, axis=-1, keepdims=True) + EPS)
    vn = (vn * lng_ref[...] + lnb_ref[...]).astype(BF16)
    for ch in range(tm // SG_CHUNK):
        rows = slice(ch * SG_CHUNK, (ch + 1) * SG_CHUNK)
        for g in range(SG_GROUPS):
            cols = slice(g * SG_GROUP_CH, (g + 1) * SG_GROUP_CH)
            mixed = jnp.dot(ws_ref[g], vn[rows, cols], preferred_element_type=F32) + bs_ref[g]
            s_ref[rows, cols] = (u[rows, cols] * mixed).astype(BF16)


def _sgu(p_ruv, ln_g, ln_b, w_s, layer, b_s_bcast):
    m = p_ruv.shape[0]
    tm = ROW_BLOCK
    return pl.pallas_call(
        _sgu_kernel,
        out_shape=jax.ShapeDtypeStruct((m, SG_DIM), BF16),
        grid=(m // tm,),
        in_specs=[pl.BlockSpec((tm, SG_DIM), lambda i: (i, RUV_COL_U)),
                  pl.BlockSpec((tm, SG_DIM), lambda i: (i, RUV_COL_VS)),
                  pl.BlockSpec((1, SG_DIM), lambda i: (0, 0)),
                  pl.BlockSpec((1, SG_DIM), lambda i: (0, 0)),
                  pl.BlockSpec((None, SG_GROUPS, SG_CHUNK, SG_CHUNK), lambda i: (layer, 0, 0, 0)),
                  pl.BlockSpec((SG_GROUPS, SG_CHUNK, SG_GROUP_CH), lambda i: (0, 0, 0))],
        out_specs=pl.BlockSpec((tm, SG_DIM), lambda i: (i, 0)),
        compiler_params=_params("parallel"), name="sgu",
    )(p_ruv, p_ruv, ln_g.reshape(1, SG_DIM), ln_b.reshape(1, SG_DIM), w_s, b_s_bcast)


def _split_bf16(x):
    hi = x.astype(BF16)
    lo = (x - hi.astype(F32)).astype(BF16)
    return hi, lo


def _gla_prep_kernel(q_ref, k_ref, v_ref, dl_ref, wup_ref, bdec_ref,
                     qmf_ref, kmf_ref, qef_ref, qmb_ref, kmb_ref, qeb_ref,
                     utf_ref, utb_ref, dtot_ref):
    c = q_ref.shape[0]
    half = c // 2
    z = jnp.dot(dl_ref[...], wup_ref[...], precision=lax.Precision.HIGHEST,
                preferred_element_type=F32) + bdec_ref[...]
    g = -(jnp.maximum(-z, 0.0) + jnp.log1p(jnp.exp(-jnp.abs(z)))) * (1.0 / GLA_TAU)
    ri = lax.broadcasted_iota(jnp.int32, (c, c), 0)
    ci = lax.broadcasted_iota(jnp.int32, (c, c), 1)
    lane = lax.broadcasted_iota(jnp.int32, (1, HEAD_PAIR), 1)
    q = q_ref[...].astype(F32) * (GLA_DK ** -0.5)
    k = k_ref[...].astype(F32)
    vb = v_ref[...]

    def direction(gd, tri, mid_row, end_row, qm_ref, km_ref, qe_ref, ut_ref):
        hi, lo = _split_bf16(gd)
        tri = tri.astype(BF16)
        bc = (jnp.dot(tri, hi, preferred_element_type=F32)
              + jnp.dot(tri, lo, preferred_element_type=F32))
        rho = bc[mid_row:mid_row + 1, :]
        tot = bc[end_row:end_row + 1, :]
        e1 = bc - rho
        qm_ref[...] = (q * jnp.exp(jnp.minimum(e1, EXP_CLAMP))).astype(BF16)
        km_ref[...] = (k * jnp.exp(jnp.minimum(-e1, EXP_CLAMP))).astype(BF16)
        qe_ref[...] = (q * jnp.exp(bc)).astype(BF16)
        k2 = (k * jnp.exp(tot - bc)).astype(BF16)
        for p in range(GLA_HEADS // 2):
            kp = k2[:, p * HEAD_PAIR:(p + 1) * HEAD_PAIR]
            ut = jnp.zeros((GLA_DV, HEAD_PAIR), F32)
            for hh in range(2):
                h = 2 * p + hh
                keep = (lane < GLA_DK) if hh == 0 else (lane >= GLA_DK)
                kh = jnp.where(keep, kp, jnp.zeros_like(kp))
                ut = ut + lax.dot_general(vb[:, h * GLA_DV:(h + 1) * GLA_DV], kh, TN_DIMS,
                                          preferred_element_type=F32)
            ut_ref[0, :, p * HEAD_PAIR:(p + 1) * HEAD_PAIR] = ut
        return jnp.exp(tot)

    df = direction(g[:, :QK_DIM], ci <= ri, half - 1, c - 1, qmf_ref, kmf_ref, qef_ref, utf_ref)
    db = direction(g[:, QK_DIM:], ci >= ri, half, 0, qmb_ref, kmb_ref, qeb_ref, utb_ref)
    dtot_ref[0] = jnp.concatenate([df, db, jnp.zeros((6, QK_DIM), F32)], axis=0)


def _gla_prep(qkv, dlow, wup, bdec):
    m = qkv.shape[0]
    c = ROW_BLOCK
    nb = m // c
    qk_spec = pl.BlockSpec((c, QK_DIM), lambda i: (i, 0))
    bf_qk = jax.ShapeDtypeStruct((m, QK_DIM), BF16)
    ut_shape = jax.ShapeDtypeStruct((nb, GLA_DV, QK_DIM), F32)
    ut_spec = pl.BlockSpec((1, GLA_DV, QK_DIM), lambda i: (i, 0, 0))
    return pl.pallas_call(
        _gla_prep_kernel,
        out_shape=(bf_qk,) * 6 + (ut_shape, ut_shape, jax.ShapeDtypeStruct((nb, 8, QK_DIM), F32)),
        grid=(nb,),
        in_specs=[pl.BlockSpec((c, QK_DIM), lambda i: (i, 0)),
                  pl.BlockSpec((c, QK_DIM), lambda i: (i, 1)),
                  pl.BlockSpec((c, V_DIM), lambda i: (i, 1)),
                  pl.BlockSpec((c, LANES), lambda i: (i, 0)),
                  pl.BlockSpec((LANES, 2 * QK_DIM), lambda i: (0, 0)),
                  pl.BlockSpec((1, 2 * QK_DIM), lambda i: (0, 0))],
        out_specs=(qk_spec,) * 6 + (ut_spec, ut_spec,
                                    pl.BlockSpec((1, 8, QK_DIM), lambda i: (i, 0, 0))),
        compiler_params=_params("parallel"), name="gla_prep",
    )(qkv, qkv, qkv, dlow, wup, bdec)


def _gla_scan_kernel(utf_ref, dtf_ref, utb_ref, dtb_ref, stf_ref, stb_ref, sf_ref, sb_ref):
    @pl.when(pl.program_id(0) == 0)
    def _():
        sf_ref[...] = jnp.zeros_like(sf_ref)
        sb_ref[...] = jnp.zeros_like(sb_ref)

    sf = sf_ref[...]
    stf_ref[0] = sf.astype(BF16)
    sf_ref[...] = dtf_ref[0, 0:1, :] * sf + utf_ref[0]
    sb = sb_ref[...]
    stb_ref[0] = sb.astype(BF16)
    sb_ref[...] = dtb_ref[0, 1:2, :] * sb + utb_ref[0]


def _gla_scan(utf, utb, dtot, ctx_blocks):
    nb = utf.shape[0]

    def bwd(t):
        return jnp.where(t < ctx_blocks, ctx_blocks - 1 - t, nb + ctx_blocks - 1 - t)

    st_shape = jax.ShapeDtypeStruct((nb, GLA_DV, QK_DIM), BF16)
    blk = (1, GLA_DV, QK_DIM)
    return pl.pallas_call(
        _gla_scan_kernel,
        out_shape=(st_shape, st_shape),
        grid=(nb,),
        in_specs=[pl.BlockSpec(blk, lambda t: (t, 0, 0)),
                  pl.BlockSpec((1, 8, QK_DIM), lambda t: (t, 0, 0)),
                  pl.BlockSpec(blk, lambda t: (bwd(t), 0, 0)),
                  pl.BlockSpec((1, 8, QK_DIM), lambda t: (bwd(t), 0, 0))],
        out_specs=(pl.BlockSpec(blk, lambda t: (t, 0, 0)),
                   pl.BlockSpec(blk, lambda t: (bwd(t), 0, 0))),
        scratch_shapes=[pltpu.VMEM((GLA_DV, QK_DIM), F32), pltpu.VMEM((GLA_DV, QK_DIM), F32)],
        compiler_params=_params("arbitrary"), name="gla_scan",
    )(utf, dtot, utb, dtot)


def _gla_out_kernel(qmf_ref, kmf_ref, qef_ref, qmb_ref, kmb_ref, qeb_ref, vb_ref,
                    stf_ref, stb_ref, r_ref, gain_ref, a_ref):
    c = qmf_ref.shape[0]
    ri = lax.broadcasted_iota(jnp.int32, (c, c), 0)
    ci = lax.broadcasted_iota(jnp.int32, (c, c), 1)
    lower = ci <= ri
    upper = ci >= ri
    lane = lax.broadcasted_iota(jnp.int32, (1, HEAD_PAIR), 1)
    for p in range(GLA_HEADS // 2):
        pair = slice(p * HEAD_PAIR, (p + 1) * HEAD_PAIR)
        qmf, kmf, qef = qmf_ref[:, pair], kmf_ref[:, pair], qef_ref[:, pair]
        qmb, kmb, qeb = qmb_ref[:, pair], kmb_ref[:, pair], qeb_ref[:, pair]
        stf = stf_ref[0, :, pair]
        stb = stb_ref[0, :, pair]
        for hh in range(2):
            h = 2 * p + hh
            keep = (lane < GLA_DK) if hh == 0 else (lane >= GLA_DK)
            zero = jnp.zeros_like(kmf)
            af = lax.dot_general(qmf, jnp.where(keep, kmf, zero), NT_DIMS, preferred_element_type=F32)
            ab = lax.dot_general(qmb, jnp.where(keep, kmb, zero), NT_DIMS, preferred_element_type=F32)
            att = (jnp.where(lower, af, 0.0) + jnp.where(upper, ab, 0.0)).astype(BF16)
            cols = slice(h * GLA_DV, (h + 1) * GLA_DV)
            o = jnp.dot(att, vb_ref[:, cols], preferred_element_type=F32)
            o = o + lax.dot_general(jnp.where(keep, qef, zero), stf, NT_DIMS, preferred_element_type=F32)
            o = o + lax.dot_general(jnp.where(keep, qeb, zero), stb, NT_DIMS, preferred_element_type=F32)
            o = o * lax.rsqrt(jnp.mean(o * o, axis=-1, keepdims=True) + EPS)
            r = r_ref[:, cols].astype(F32)
            a_ref[:, cols] = (o * gain_ref[:, cols] * (r * jax.nn.sigmoid(r))).astype(BF16)


def _gla_out(prep, qkv, stf, stb, p_ruv, gain):
    qmf, kmf, qef, qmb, kmb, qeb = prep
    m = qkv.shape[0]
    c = ROW_BLOCK
    qk_spec = pl.BlockSpec((c, QK_DIM), lambda i: (i, 0))
    st_spec = pl.BlockSpec((1, GLA_DV, QK_DIM), lambda i: (i, 0, 0))
    return pl.pallas_call(
        _gla_out_kernel,
        out_shape=jax.ShapeDtypeStruct((m, V_DIM), BF16),
        grid=(m // c,),
        in_specs=[qk_spec] * 6 + [pl.BlockSpec((c, V_DIM), lambda i: (i, 1)), st_spec, st_spec,
                                  pl.BlockSpec((c, V_DIM), lambda i: (i, RUV_COL_R)),
                                  pl.BlockSpec((1, V_DIM), lambda i: (0, 0))],
        out_specs=pl.BlockSpec((c, V_DIM), lambda i: (i, 0)),
        compiler_params=_params("parallel"), name="gla_out",
    )(qmf, kmf, qef, qmb, kmb, qeb, qkv, stf, stb, p_ruv, gain.reshape(1, V_DIM))


ONE_F32_BITS = 0x3F800000
SEARCH_STEPS = 31


def _route_kernel(logit_ref, g_ref, *, ctx_rows):
    m = logit_ref.shape[0]
    rb = ROW_BLOCK
    lane = lax.broadcasted_iota(jnp.int32, (1, LANES), 1)
    valid = lane < N_EXPERTS

    def softmax_block(i, carry):
        r0 = pl.multiple_of(i * rb, rb)
        lg = jnp.where(valid, logit_ref[pl.ds(r0, rb), :], -jnp.inf)
        ex = jnp.exp(lg - jnp.max(lg, axis=-1, keepdims=True))
        g_ref[pl.ds(r0, rb), :] = ex / jnp.sum(ex, axis=-1, keepdims=True)
        return carry

    lax.fori_loop(0, m // rb, softmax_block, 0)

    def bits_of(r0):
        return lax.bitcast_convert_type(g_ref[pl.ds(r0, rb), :], jnp.int32)

    def count_ge(blk0, nblk, t):
        def body(i, acc):
            r0 = pl.multiple_of((blk0 + i) * rb, rb)
            hit = jnp.where(bits_of(r0) >= t, 1, 0)
            return acc + jnp.sum(hit.reshape(rb // 8, 8, LANES), axis=0)

        acc = lax.fori_loop(0, nblk, body, jnp.zeros((8, LANES), jnp.int32))
        return jnp.sum(acc, axis=0, keepdims=True)

    def route_set(blk0, nblk):
        cap = EC_CAPACITY * nblk * rb // N_EXPERTS

        def halve(_, lohi):
            lo, hi = lohi
            mid = lo + ((hi - lo) >> 1)
            ok = count_ge(blk0, nblk, mid) >= cap
            return jnp.where(ok, mid, lo), jnp.where(ok, hi, mid)

        thr, _ = lax.fori_loop(0, SEARCH_STEPS, halve,
                               (jnp.zeros((1, LANES), jnp.int32),
                                jnp.full((1, LANES), ONE_F32_BITS + 1, jnp.int32)))
        need = (cap - count_ge(blk0, nblk, thr + 1)).astype(F32)
        ri = lax.broadcasted_iota(jnp.int32, (rb, rb), 0)
        ci = lax.broadcasted_iota(jnp.int32, (rb, rb), 1)
        earlier = jnp.where(ci < ri, 1.0, 0.0).astype(BF16)

        def finalize(i, seen):
            r0 = pl.multiple_of((blk0 + i) * rb, rb)
            aff = g_ref[pl.ds(r0, rb), :]
            bits = lax.bitcast_convert_type(aff, jnp.int32)
            tie = bits == thr
            tie_f = jnp.where(tie, 1.0, 0.0)
            rank = seen + jnp.dot(earlier, tie_f.astype(BF16), preferred_element_type=F32)
            g_ref[pl.ds(r0, rb), :] = jnp.where(tie, jnp.where(rank < need, aff, 0.0),
                                                jnp.where(bits > thr, aff, 0.0))
            return seen + jnp.sum(tie_f, axis=0, keepdims=True)

        lax.fori_loop(0, nblk, finalize, jnp.zeros((1, LANES), F32))

    route_set(0, ctx_rows // rb)
    route_set(ctx_rows // rb, (m - ctx_rows) // rb)


def _route(logits, ctx_rows):
    m = logits.shape[0]
    return pl.pallas_call(
        functools.partial(_route_kernel, ctx_rows=ctx_rows),
        out_shape=jax.ShapeDtypeStruct((m, LANES), F32),
        grid=(1,),
        in_specs=[pl.BlockSpec((m, LANES), lambda i: (0, 0))],
        out_specs=pl.BlockSpec((m, LANES), lambda i: (0, 0)),
        compiler_params=_params("arbitrary"), name="route",
    )(logits)


SLOTS_PER_BLOCK = 64
GATE_PARTS = 3


def _slots_kernel(g_ref, pos_ref, post_ref, cnt_ref):
    rb = g_ref.shape[0]
    picked = jnp.where(g_ref[...] > 0.0, 1.0, 0.0)
    ri = lax.broadcasted_iota(jnp.int32, (rb, rb), 0)
    ci = lax.broadcasted_iota(jnp.int32, (rb, rb), 1)
    earlier = jnp.where(ci < ri, 1.0, 0.0).astype(BF16)
    rank = jnp.dot(earlier, picked.astype(BF16), preferred_element_type=F32)
    pos = jnp.where(picked > 0.0, rank, -1.0)
    pos_ref[...] = pos
    post_ref[...] = pos.T[:N_EXPERTS, :]
    cnt_ref[0] = jnp.broadcast_to(jnp.sum(picked, axis=0, keepdims=True), (8, LANES))


def _slots(gates):
    m = gates.shape[0]
    rb = ROW_BLOCK
    nb = m // rb
    return pl.pallas_call(
        _slots_kernel,
        out_shape=(jax.ShapeDtypeStruct((m, LANES), F32),
                   jax.ShapeDtypeStruct((N_EXPERTS, m), F32),
                   jax.ShapeDtypeStruct((nb, 8, LANES), F32)),
        grid=(nb,),
        in_specs=[pl.BlockSpec((rb, LANES), lambda i: (i, 0))],
        out_specs=(pl.BlockSpec((rb, LANES), lambda i: (i, 0)),
                   pl.BlockSpec((N_EXPERTS, rb), lambda i: (0, i)),
                   pl.BlockSpec((1, 8, LANES), lambda i: (i, 0, 0))),
        compiler_params=_params("parallel"), name="moe_slots",
    )(gates)


def _slot_onehot(post):
    rb = post.shape[1]
    slot = lax.broadcasted_iota(jnp.int32, (SLOTS_PER_BLOCK, rb), 0).astype(F32)
    parts = [jnp.where(slot == post[e:e + 1, :], 1.0, 0.0) for e in range(N_EXPERTS)]
    return jnp.concatenate(parts, axis=0).astype(BF16)


def _dispatch_kernel(h_ref, post_ref, g_ref, xs_ref, gs_ref):
    onehot = _slot_onehot(post_ref[...])
    d = h_ref.shape[1]
    xs = jnp.dot(onehot, h_ref[...], preferred_element_type=F32)
    xs_ref[...] = xs.astype(BF16).reshape(N_EXPERTS, SLOTS_PER_BLOCK, d)
    g = g_ref[...]
    pieces = jnp.zeros_like(g)
    for k in range(GATE_PARTS):
        piece = g.astype(BF16).astype(F32)
        pieces = pieces + (pltpu.roll(piece, k * N_EXPERTS, axis=1) if k else piece)
        g = g - piece
    gs = jnp.dot(onehot, pieces.astype(BF16), preferred_element_type=F32)
    gs_ref[...] = gs.reshape(N_EXPERTS, SLOTS_PER_BLOCK, LANES)


def _dispatch(h, post, gates):
    m, d = h.shape
    rb = ROW_BLOCK
    nb = m // rb
    s = SLOTS_PER_BLOCK
    return pl.pallas_call(
        _dispatch_kernel,
        out_shape=(jax.ShapeDtypeStruct((N_EXPERTS, nb * s, d), BF16),
                   jax.ShapeDtypeStruct((N_EXPERTS, nb * s, LANES), F32)),
        grid=(nb,),
        in_specs=[pl.BlockSpec((rb, d), lambda i: (i, 0)),
                  pl.BlockSpec((N_EXPERTS, rb), lambda i: (0, i)),
                  pl.BlockSpec((rb, LANES), lambda i: (i, 0))],
        out_specs=(pl.BlockSpec((N_EXPERTS, s, d), lambda i: (0, i, 0)),
                   pl.BlockSpec((N_EXPERTS, s, LANES), lambda i: (0, i, 0))),
        compiler_params=_params("parallel"), name="moe_dispatch",
    )(h, post, gates)


def _expert_kernel(xs_ref, w1_ref, w3_ref, w2_ref, gs_ref, y_ref):
    e = pl.program_id(0)
    lane = lax.broadcasted_iota(jnp.int32, (1, LANES), 1)
    mine = (lane % N_EXPERTS == e) & (lane < GATE_PARTS * N_EXPERTS)
    gate = jnp.sum(jnp.where(mine, gs_ref[0], 0.0), axis=-1, keepdims=True)
    x = xs_ref[0]
    h1 = jnp.dot(x, w1_ref[0], preferred_element_type=F32)
    h3 = jnp.dot(x, w3_ref[0], preferred_element_type=F32)
    hid = (h1 * jax.nn.sigmoid(h1) * h3 * gate).astype(BF16)
    y_ref[0] = jnp.dot(hid, w2_ref[0], preferred_element_type=F32).astype(BF16)


def _expert_ffn(xs, gs, w1, w3, w2, layer):
    e, rows, d = xs.shape
    f = w1.shape[-1]
    tr = _pick(rows, (832, 1024, 512, 256, 128, 64, 16))
    return pl.pallas_call(
        _expert_kernel,
        out_shape=jax.ShapeDtypeStruct((e, rows, d), BF16),
        grid=(e, rows // tr),
        in_specs=[pl.BlockSpec((1, tr, d), lambda i, j: (i, j, 0)),
                  pl.BlockSpec((None, 1, d, f), lambda i, j: (layer, i, 0, 0)),
                  pl.BlockSpec((None, 1, d, f), lambda i, j: (layer, i, 0, 0)),
                  pl.BlockSpec((None, 1, f, d), lambda i, j: (layer, i, 0, 0)),
                  pl.BlockSpec((1, tr, LANES), lambda i, j: (i, j, 0))],
        out_specs=pl.BlockSpec((1, tr, d), lambda i, j: (i, j, 0)),
        compiler_params=_params("parallel", "arbitrary"), name="expert_ffn",
    )(xs, w1, w3, w2, gs)


def _combined_rows(ys_ref, post_ref, x_ref, mod_ref, gate_row):
    onehot = _slot_onehot(post_ref[...])
    d = x_ref.shape[1]
    ys = ys_ref[...].reshape(N_EXPERTS * SLOTS_PER_BLOCK, d)
    moe = lax.dot_general(onehot, ys, TN_DIMS, preferred_element_type=F32)
    return x_ref[...] + mod_ref[0, gate_row:gate_row + 1, :] * moe


def _combine_prenorm_kernel(ys_ref, post_ref, x_ref, mod_ref, g_ref, nmod_ref, o_ref, h_ref, *,
                            gate_row):
    x = _combined_rows(ys_ref, post_ref, x_ref, mod_ref, gate_row)
    o_ref[...] = x
    h_ref[...] = _modulated_norm(x, g_ref[...], nmod_ref[0], 0, 1).astype(BF16)


def _combine_final_kernel(ys_ref, post_ref, x_ref, mod_ref, g_ref, o_ref, *, gate_row):
    x = _combined_rows(ys_ref, post_ref, x_ref, mod_ref, gate_row)
    ms = jnp.mean(x * x, axis=-1, keepdims=True)
    o_ref[...] = x * lax.rsqrt(ms + EPS) * g_ref[...]


def _combine(ys, post, x, mods, ctx_rows, gate_row, norm_g, next_mods=None):
    m, d = x.shape
    rb = ROW_BLOCK
    s = SLOTS_PER_BLOCK
    stream = _stream_of_block(rb, ctx_rows)
    row_spec = pl.BlockSpec((rb, d), lambda i: (i, 0))
    mod_spec = pl.BlockSpec((1, 8, d), lambda i: (stream(i), 0, 0))
    in_specs = [pl.BlockSpec((N_EXPERTS, s, d), lambda i: (0, i, 0)),
                pl.BlockSpec((N_EXPERTS, rb), lambda i: (0, i)),
                row_spec, mod_spec, pl.BlockSpec((1, d), lambda i: (0, 0))]
    if next_mods is not None:
        return pl.pallas_call(
            functools.partial(_combine_prenorm_kernel, gate_row=gate_row),
            out_shape=(jax.ShapeDtypeStruct((m, d), F32), jax.ShapeDtypeStruct((m, d), BF16)),
            grid=(m // rb,),
            in_specs=in_specs + [mod_spec],
            out_specs=(row_spec, row_spec),
            compiler_params=_params("parallel"), name="moe_combine",
        )(ys, post, x, mods, norm_g.reshape(1, d), next_mods)
    skip = ctx_rows // rb
    return pl.pallas_call(
        functools.partial(_combine_final_kernel, gate_row=gate_row),
        out_shape=jax.ShapeDtypeStruct((m - ctx_rows, d), F32),
        grid=(m // rb,),
        in_specs=in_specs,
        out_specs=pl.BlockSpec((rb, d), lambda i: (jnp.maximum(i - skip, 0), 0)),
        compiler_params=_params("arbitrary"), name="moe_combine_final",
    )(ys, post, x, mods, norm_g.reshape(1, d))


def _moe_hidden_kernel(h_ref, w1_ref, w3_ref, g_ref, o_ref):
    e = pl.program_id(1)
    lane = lax.broadcasted_iota(jnp.int32, (1, LANES), 1)
    gate = jnp.sum(jnp.where(lane == e, g_ref[...], 0.0), axis=-1, keepdims=True)
    h = h_ref[...]
    h1 = jnp.dot(h, w1_ref[0].astype(BF16), preferred_element_type=F32)
    h3 = jnp.dot(h, w3_ref[0].astype(BF16), preferred_element_type=F32)
    o_ref[...] = (h1 * jax.nn.sigmoid(h1) * h3 * gate).astype(BF16)


def _moe_hidden(h, gates, w1, w3, layer):
    m, d = h.shape
    _, e, _, f = w1.shape
    tm = _pick(m, (1280, 1024, 768, 512, 256))
    return pl.pallas_call(
        _moe_hidden_kernel,
        out_shape=jax.ShapeDtypeStruct((m, e * f), BF16),
        grid=(m // tm, e),
        in_specs=[pl.BlockSpec((tm, d), lambda i, j: (i, 0)),
                  pl.BlockSpec((None, 1, d, f), lambda i, j: (layer, j, 0, 0)),
                  pl.BlockSpec((None, 1, d, f), lambda i, j: (layer, j, 0, 0)),
                  pl.BlockSpec((tm, LANES), lambda i, j: (i, 0))],
        out_specs=pl.BlockSpec((tm, f), lambda i, j: (i, j)),
        compiler_params=_params("parallel", "arbitrary"), name="moe_hidden",
    )(h, w1, w3, gates)


def kernel(x, c, ctx, c_ctx, ada_w, ada_b, norm1_g, norm2_g, w_in, w_dec_up, b_dec, gla_norm_g,
           sg_ln_g, sg_ln_b, sg_w, sg_b, w_branch_a, w_branch_b, w_out, w_router, w_exp1, w_exp3,
           w_exp2, final_g):
    batch, seq, d = x.shape
    ctx_rows = ctx.shape[1]
    depth = ada_w.shape[0]
    assert batch == 1 and seq % ROW_BLOCK == 0 and ctx_rows % ROW_BLOCK == 0

    cond2 = jnp.stack([c_ctx, c[0]], axis=1)
    mods_all = _adaln(cond2, ada_w, ada_b).reshape(depth, 2, 6, d)
    mods_all = jnp.pad(mods_all, ((0, 0), (0, 0), (0, 2), (0, 0)))

    w1_all, w3_all, w2_all = (w.astype(BF16) for w in (w_exp1, w_exp3, w_exp2))
    w2_flat = w2_all.reshape(depth, N_EXPERTS * EXPERT_FF, d)
    sg_w_all = sg_w.astype(BF16)
    n_in = w_in.shape[2]
    w_in_rows = jnp.swapaxes(w_in, 1, 2).reshape(depth * n_in, d)

    for l in range(depth):
        mods = mods_all[l]
        w_head_t = _cast_rows(w_in_rows, l * n_in, OFF_DF + LANES)
        w_rest_t = _cast_rows(w_in_rows, l * n_in + SCAN_COLS, n_in - SCAN_COLS)
        wup = jnp.zeros((LANES, 2 * QK_DIM), F32)
        wup = wup.at[:DECAY_RANK, :QK_DIM].set(w_dec_up[l, 0])
        wup = wup.at[DECAY_RANK:2 * DECAY_RANK, QK_DIM:].set(w_dec_up[l, 1])
        bdec = b_dec[l].reshape(1, 2 * QK_DIM)
        b_s_bcast = jnp.broadcast_to(sg_b[l][:, :, None], (SG_GROUPS, SG_CHUNK, SG_GROUP_CH))
        w_r = jnp.pad(w_router[l], ((0, 0), (0, LANES - N_EXPERTS)))

        if l == 0:
            xa, h = _stack_prenorm(ctx[0], x[0], norm1_g[l], mods)
        p_qkv, dlow = _matmul_nt_with_side(h, w_head_t, OFF_DF, BF16)
        p_ruv = _matmul_nt(h, w_rest_t, 3 * V_DIM, BF16)

        prep = _gla_prep(p_qkv, dlow, wup, bdec)
        stf, stb = _gla_scan(prep[6], prep[7], prep[8], ctx_rows // ROW_BLOCK)
        a = _gla_out(prep[:6], p_qkv, stf, stb, p_ruv, gla_norm_g[l])
        s = _sgu(p_ruv, sg_ln_g[l], sg_ln_b[l], sg_w_all, l, b_s_bcast)
        mix = _merge(h, a, s, w_rest_t, 3 * V_DIM, w_branch_a, w_branch_b, l)
        xa = _matmul_gated_residual(mix, w_out, xa, mods, ctx_rows, gate_row=2, layer=l)

        h2, logits = _prenorm_router(xa, norm2_g[l], mods, ctx_rows, 3, 4, w_r)
        gates = _route(logits, ctx_rows)
        pos, post, cnt = _slots(gates)
        xs, gs = _dispatch(h2, post, gates)
        ys = _expert_ffn(xs, gs, w1_all, w3_all, w2_all, l)

        def combine(xa, l=l, ys=ys, post=post, mods=mods):
            if l + 1 < depth:
                return _combine(ys, post, xa, mods, ctx_rows, 5, norm1_g[l + 1], mods_all[l + 1])
            return _combine(ys, post, xa, mods, ctx_rows, 5, final_g)

        def combine_after_overflow(xa, l=l, h2=h2, gates=gates, pos=pos, mods=mods, combine=combine):
            late = jnp.where(pos >= SLOTS_PER_BLOCK, gates, 0.0)
            hid = _moe_hidden(h2, late, w1_all, w3_all, l)
            return combine(_matmul_gated_residual(hid, w2_flat, xa, mods, ctx_rows, gate_row=5, layer=l))

        res = lax.cond(jnp.max(cnt) > SLOTS_PER_BLOCK, combine_after_overflow, combine, xa)
        if l + 1 < depth:
            xa, h = res
        else:
            out = res

    return out[None]
```

```python
import functools

import jax
import jax.numpy as jnp
from jax import lax
from jax.experimental import pallas as pl
from jax.experimental.pallas import tpu as pltpu

EPS = 1e-6
GLA_HEADS = 8
GLA_DK = 64
GLA_DV = 128
DECAY_RANK = 16
GLA_TAU = 16.0
SG_GROUPS = 8
SG_GROUP_CH = 128
SG_CHUNK = 128
N_EXPERTS = 16
EXPERT_FF = 256
EC_CAPACITY = 2

QK_DIM = GLA_HEADS * GLA_DK
V_DIM = GLA_HEADS * GLA_DV
SG_DIM = SG_GROUPS * SG_GROUP_CH
OFF_K = QK_DIM
OFF_V = 2 * QK_DIM
OFF_DF = OFF_V + V_DIM
OFF_DB = OFF_DF + DECAY_RANK
SCAN_COLS = OFF_DB + DECAY_RANK
OFF_U = SCAN_COLS + V_DIM
OFF_VS = OFF_U + SG_DIM
OFF_GA = OFF_VS + SG_DIM

RUV_COL_R, RUV_COL_U, RUV_COL_VS = 0, 1, 2

LANES = 128
ROW_BLOCK = 256
HEAD_PAIR = 2 * GLA_DK
EXP_CLAMP = 80.0
VMEM_LIMIT_BYTES = 52 * 1024 * 1024

F32 = jnp.float32
BF16 = jnp.bfloat16
NT_DIMS = (((1,), (1,)), ((), ()))
TN_DIMS = (((0,), (0,)), ((), ()))


def _pick(n, candidates):
    for c in candidates:
        if n % c == 0:
            return c
    raise ValueError(f"no tile of {candidates} divides {n}")


def _params(*semantics):
    return pltpu.CompilerParams(dimension_semantics=semantics,
                                vmem_limit_bytes=VMEM_LIMIT_BYTES)


def _stream_of_block(tm, ctx_rows):
    return lambda i: (i * tm >= ctx_rows).astype(jnp.int32)


def _adaln_kernel(cond_ref, w_ref, b_ref, out_ref, *, rows):
    d = cond_ref.shape[0]
    tn = out_ref.shape[-1]

    def body(i, acc):
        r0 = pl.multiple_of(i * rows, rows)
        cond = cond_ref[pl.ds(r0, rows), :]
        s = cond * jax.nn.sigmoid(cond)
        w = w_ref[0, pl.ds(r0, rows), :]
        a0 = jnp.sum(w * s[:, 0:1], axis=0, keepdims=True)
        a1 = jnp.sum(w * s[:, 1:2], axis=0, keepdims=True)
        return acc + jnp.concatenate([a0, a1], axis=0)

    acc = lax.fori_loop(0, d // rows, body, jnp.zeros((2, tn), F32))
    out_ref[0] = acc + b_ref[0]


def _adaln(cond2, ada_w, ada_b):
    nl, d, n6 = ada_w.shape
    tn = _pick(n6, (1024, 512, 256, 128))
    return pl.pallas_call(
        functools.partial(_adaln_kernel, rows=_pick(d, (256, 128, 8))),
        out_shape=jax.ShapeDtypeStruct((nl, 2, n6), F32),
        grid=(nl, n6 // tn),
        in_specs=[pl.BlockSpec((d, 2), lambda l, j: (0, 0)),
                  pl.BlockSpec((1, d, tn), lambda l, j: (l, 0, j)),
                  pl.BlockSpec((1, 1, tn), lambda l, j: (l, 0, j))],
        out_specs=pl.BlockSpec((1, 2, tn), lambda l, j: (l, 0, j)),
        compiler_params=_params("parallel", "parallel"),
        name="adaln",
    )(cond2, ada_w, ada_b.reshape(nl, 1, n6))


def _modulated_norm(x, g, mod, shift_row, scale_row):
    ms = jnp.mean(x * x, axis=-1, keepdims=True)
    y = x * lax.rsqrt(ms + EPS) * g
    return y * (1.0 + mod[scale_row:scale_row + 1, :]) + mod[shift_row:shift_row + 1, :]


def _stack_prenorm_kernel(ctx_ref, x_ref, g_ref, mod_ref, xa_ref, h_ref, *, ctx_blocks):
    xin = jnp.where(pl.program_id(0) < ctx_blocks, ctx_ref[...], x_ref[...])
    xa_ref[...] = xin
    h_ref[...] = _modulated_norm(xin, g_ref[...], mod_ref[0], 0, 1).astype(BF16)


def _stack_prenorm(ctx2d, x2d, g, mods):
    ctx_rows, d = ctx2d.shape
    m = ctx_rows + x2d.shape[0]
    tm = ROW_BLOCK
    nc = ctx_rows // tm
    row_spec = pl.BlockSpec((tm, d), lambda i: (i, 0))
    return pl.pallas_call(
        functools.partial(_stack_prenorm_kernel, ctx_blocks=nc),
        out_shape=(jax.ShapeDtypeStruct((m, d), F32), jax.ShapeDtypeStruct((m, d), BF16)),
        grid=(m // tm,),
        in_specs=[pl.BlockSpec((tm, d), lambda i: (jnp.minimum(i, nc - 1), 0)),
                  pl.BlockSpec((tm, d), lambda i: (jnp.maximum(i - nc, 0), 0)),
                  pl.BlockSpec((1, d), lambda i: (0, 0)),
                  pl.BlockSpec((1, 8, d), lambda i: (_stream_of_block(tm, ctx_rows)(i), 0, 0))],
        out_specs=(row_spec, row_spec),
        compiler_params=_params("parallel"), name="stack_prenorm",
    )(ctx2d, x2d, g.reshape(1, d), mods)


def _prenorm_router_kernel(x_ref, g_ref, mod_ref, wr_ref, h_ref, logit_ref, *, shift_row, scale_row):
    h = _modulated_norm(x_ref[...], g_ref[...], mod_ref[0], shift_row, scale_row)
    h_hi, h_lo = _split_bf16(h)
    h_ref[...] = h_hi
    hi = jnp.dot(h_hi, wr_ref[...], preferred_element_type=F32)
    lo = jnp.dot(h_lo, wr_ref[:, :LANES], preferred_element_type=F32)
    logit_ref[...] = hi[:, :LANES] + hi[:, LANES:] + lo


def _prenorm_router(x, g, mods, ctx_rows, shift_row, scale_row, w_router):
    m, d = x.shape
    tm = ROW_BLOCK
    stream = _stream_of_block(tm, ctx_rows)
    row_spec = pl.BlockSpec((tm, d), lambda i: (i, 0))
    return pl.pallas_call(
        functools.partial(_prenorm_router_kernel, shift_row=shift_row, scale_row=scale_row),
        out_shape=(jax.ShapeDtypeStruct((m, d), BF16), jax.ShapeDtypeStruct((m, LANES), F32)),
        grid=(m // tm,),
        in_specs=[row_spec,
                  pl.BlockSpec((1, d), lambda i: (0, 0)),
                  pl.BlockSpec((1, 8, d), lambda i: (stream(i), 0, 0)),
                  pl.BlockSpec((d, 2 * LANES), lambda i: (0, 0))],
        out_specs=(row_spec, pl.BlockSpec((tm, LANES), lambda i: (i, 0))),
        compiler_params=_params("parallel"), name="prenorm_router",
    )(x, g.reshape(1, d), mods, jnp.concatenate(_split_bf16(w_router), axis=1))


def _cast_kernel(w_ref, o_ref):
    o_ref[...] = w_ref[...].astype(BF16)


def _cast_rows(w_rows, row0, n_rows):
    k = w_rows.shape[1]
    tr = _pick(n_rows, (512, 544, 256, 128))
    assert row0 % 8 == 0
    return pl.pallas_call(
        _cast_kernel,
        out_shape=jax.ShapeDtypeStruct((n_rows, k), BF16),
        grid=(n_rows // tr,),
        in_specs=[pl.BlockSpec((pl.Element(tr), pl.Element(k)),
                               lambda j: (pl.multiple_of(row0 + j * tr, 8), 0))],
        out_specs=pl.BlockSpec((tr, k), lambda j: (j, 0)),
        compiler_params=_params("parallel"), name="cast_rows",
    )(w_rows)


def _weight_spec(w, layer, rows, tn, col_block):
    if w.ndim == 2:
        return pl.BlockSpec((rows, tn), lambda i, j: (0, col_block(j)))
    return pl.BlockSpec((None, rows, tn), lambda i, j: (layer, 0, col_block(j)))


def _mm_nt_kernel(a_ref, wt_ref, o_ref):
    o_ref[...] = lax.dot_general(a_ref[...], wt_ref[...], NT_DIMS,
                                 preferred_element_type=F32).astype(o_ref.dtype)


def _matmul_nt(a, w_t, n_rows, out_dtype):
    m, k = a.shape
    n = n_rows
    tm = _pick(m, (1280, 1024, 768, 512, 256))
    tn = _pick(n, (512, 256, 128))
    return pl.pallas_call(
        _mm_nt_kernel,
        out_shape=jax.ShapeDtypeStruct((m, n), out_dtype),
        grid=(m // tm, n // tn),
        in_specs=[pl.BlockSpec((tm, k), lambda i, j: (i, 0)),
                  pl.BlockSpec((tn, k), lambda i, j: (j, 0))],
        out_specs=pl.BlockSpec((tm, tn), lambda i, j: (i, j)),
        compiler_params=_params("parallel", "arbitrary"), name="matmul",
    )(a, w_t)


def _mm_nt_side_kernel(a_ref, wt_ref, ws_ref, o_ref, side_ref):
    a = a_ref[...]
    o_ref[...] = lax.dot_general(a, wt_ref[...], NT_DIMS, preferred_element_type=F32).astype(o_ref.dtype)

    @pl.when(pl.program_id(1) == 0)
    def _():
        side_ref[...] = lax.dot_general(a, ws_ref[...], NT_DIMS, preferred_element_type=F32)


def _matmul_nt_with_side(a, w_t, n_rows, out_dtype):
    m, k = a.shape
    tm = _pick(m, (1280, 1024, 768, 512, 256))
    tn = _pick(n_rows, (512, 256, 128))
    assert w_t.shape[0] - n_rows == LANES and n_rows % LANES == 0
    return pl.pallas_call(
        _mm_nt_side_kernel,
        out_shape=(jax.ShapeDtypeStruct((m, n_rows), out_dtype), jax.ShapeDtypeStruct((m, LANES), F32)),
        grid=(m // tm, n_rows // tn),
        in_specs=[pl.BlockSpec((tm, k), lambda i, j: (i, 0)),
                  pl.BlockSpec((tn, k), lambda i, j: (j, 0)),
                  pl.BlockSpec((LANES, k), lambda i, j: (n_rows // LANES, 0))],
        out_specs=(pl.BlockSpec((tm, tn), lambda i, j: (i, j)),
                   pl.BlockSpec((tm, LANES), lambda i, j: (i, 0))),
        compiler_params=_params("parallel", "arbitrary"), name="matmul_side",
    )(a, w_t, w_t)


def _mm_resid_kernel(a_ref, w_ref, x_ref, mod_ref, o_ref, *, gate_row, ctx_rows):
    tm = a_ref.shape[0]
    acc = jnp.dot(a_ref[...], w_ref[...].astype(BF16), preferred_element_type=F32)
    row = pl.program_id(0) * tm + lax.broadcasted_iota(jnp.int32, (tm, 1), 0)
    gate = jnp.where(row < ctx_rows, mod_ref[0, gate_row:gate_row + 1, :],
                     mod_ref[1, gate_row:gate_row + 1, :])
    o_ref[...] = x_ref[...] + gate * acc


def _matmul_gated_residual(a, w, x, mods, ctx_rows, gate_row, layer=None):
    m, k = a.shape
    n = w.shape[-1]
    tm = _pick(m, (1280, 1024, 768, 512, 256))
    tn = _pick(n, (512, 256, 128))
    return pl.pallas_call(
        functools.partial(_mm_resid_kernel, gate_row=gate_row, ctx_rows=ctx_rows),
        out_shape=jax.ShapeDtypeStruct((m, n), F32),
        grid=(m // tm, n // tn),
        in_specs=[pl.BlockSpec((tm, k), lambda i, j: (i, 0)),
                  _weight_spec(w, layer, k, tn, lambda j: j),
                  pl.BlockSpec((tm, tn), lambda i, j: (i, j)),
                  pl.BlockSpec((2, 8, tn), lambda i, j: (0, 0, j))],
        out_specs=pl.BlockSpec((tm, tn), lambda i, j: (i, j)),
        compiler_params=_params("parallel", "arbitrary"), name="out_proj",
    )(a, w, x, mods)


def _merge_kernel(h_ref, a_ref, s_ref, wga_ref, wgb_ref, wa_ref, wb_ref, m_ref):
    h = h_ref[...]
    ga = lax.dot_general(h, wga_ref[...], NT_DIMS, preferred_element_type=F32)
    gb = lax.dot_general(h, wgb_ref[...], NT_DIMS, preferred_element_type=F32)
    ya = jnp.dot(a_ref[...], wa_ref[...].astype(BF16), preferred_element_type=F32)
    yb = jnp.dot(s_ref[...], wb_ref[...].astype(BF16), preferred_element_type=F32)
    m_ref[...] = (jax.nn.sigmoid(ga) * ya + jax.nn.sigmoid(gb) * yb).astype(BF16)


def _merge(h, a, s, w_gates_t, row_ga, w_a, w_b, layer):
    m, d = h.shape
    ka = a.shape[1]
    kb = s.shape[1]
    tm = _pick(m, (1280, 1024, 768, 512, 256))
    tn = _pick(d, (256, 128))
    nb = d // tn
    j0 = row_ga // tn
    return pl.pallas_call(
        _merge_kernel,
        out_shape=jax.ShapeDtypeStruct((m, d), BF16),
        grid=(m // tm, nb),
        in_specs=[pl.BlockSpec((tm, d), lambda i, j: (i, 0)),
                  pl.BlockSpec((tm, ka), lambda i, j: (i, 0)),
                  pl.BlockSpec((tm, kb), lambda i, j: (i, 0)),
                  pl.BlockSpec((tn, d), lambda i, j: (j0 + j, 0)),
                  pl.BlockSpec((tn, d), lambda i, j: (j0 + nb + j, 0)),
                  _weight_spec(w_a, layer, ka, tn, lambda j: j),
                  _weight_spec(w_b, layer, kb, tn, lambda j: j)],
        out_specs=pl.BlockSpec((tm, tn), lambda i, j: (i, j)),
        compiler_params=_params("parallel", "arbitrary"), name="merge",
    )(h, a, s, w_gates_t, w_gates_t, w_a, w_b)


def _sgu_kernel(u_ref, vs_ref, lng_ref, lnb_ref, ws_ref, bs_ref, s_ref):
    tm = u_ref.shape[0]
    u = jax.nn.gelu(u_ref[...].astype(F32))
    v = jax.nn.gelu(vs_ref[...].astype(F32))
    vc = v - jnp.mean(v, axis=-1, keepdims=True)
    vn = vc * lax.rsqrt(jnp.mean(vc * vc, axis=-1, keepdims=True) + EPS)
    vn = (vn * lng_ref[...] + lnb_ref[...]).astype(BF16)
    for ch in range(tm // SG_CHUNK):
        rows = slice(ch * SG_CHUNK, (ch + 1) * SG_CHUNK)
        for g in range(SG_GROUPS):
            cols = slice(g * SG_GROUP_CH, (g + 1) * SG_GROUP_CH)
            mixed = jnp.dot(ws_ref[g], vn[rows, cols], preferred_element_type=F32) + bs_ref[g]
            s_ref[rows, cols] = (u[rows, cols] * mixed).astype(BF16)


def _sgu(p_ruv, ln_g, ln_b, w_s, layer, b_s_bcast):
    m = p_ruv.shape[0]
    tm = ROW_BLOCK
    return pl.pallas_call(
        _sgu_kernel,
        out_shape=jax.ShapeDtypeStruct((m, SG_DIM), BF16),
        grid=(m // tm,),
        in_specs=[pl.BlockSpec((tm, SG_DIM), lambda i: (i, RUV_COL_U)),
                  pl.BlockSpec((tm, SG_DIM), lambda i: (i, RUV_COL_VS)),
                  pl.BlockSpec((1, SG_DIM), lambda i: (0, 0)),
                  pl.BlockSpec((1, SG_DIM), lambda i: (0, 0)),
                  pl.BlockSpec((None, SG_GROUPS, SG_CHUNK, SG_CHUNK), lambda i: (layer, 0, 0, 0)),
                  pl.BlockSpec((SG_GROUPS, SG_CHUNK, SG_GROUP_CH), lambda i: (0, 0, 0))],
        out_specs=pl.BlockSpec((tm, SG_DIM), lambda i: (i, 0)),
        compiler_params=_params("parallel"), name="sgu",
    )(p_ruv, p_ruv, ln_g.reshape(1, SG_DIM), ln_b.reshape(1, SG_DIM), w_s, b_s_bcast)


def _split_bf16(x):
    hi = x.astype(BF16)
    lo = (x - hi.astype(F32)).astype(BF16)
    return hi, lo


def _gla_prep_kernel(q_ref, k_ref, v_ref, dl_ref, wup_ref, bdec_ref,
                     qmf_ref, kmf_ref, qef_ref, qmb_ref, kmb_ref, qeb_ref,
                     utf_ref, utb_ref, dtot_ref):
    c = q_ref.shape[0]
    half = c // 2
    n2 = 2 * QK_DIM
    dl_hi, dl_lo = _split_bf16(dl_ref[...])
    zz = jnp.dot(dl_hi, wup_ref[...], preferred_element_type=F32)
    z = (zz[:, :n2] + zz[:, n2:] + jnp.dot(dl_lo, wup_ref[:, :n2], preferred_element_type=F32)
         + bdec_ref[...])
    g = -(jnp.maximum(-z, 0.0) + jnp.log1p(jnp.exp(-jnp.abs(z)))) * (1.0 / GLA_TAU)
    ri = lax.broadcasted_iota(jnp.int32, (c, c), 0)
    ci = lax.broadcasted_iota(jnp.int32, (c, c), 1)
    lane = lax.broadcasted_iota(jnp.int32, (1, HEAD_PAIR), 1)
    q = q_ref[...].astype(F32) * (GLA_DK ** -0.5)
    k = k_ref[...].astype(F32)
    vb = v_ref[...]

    def direction(gd, tri, mid_row, end_row, qm_ref, km_ref, qe_ref, ut_ref):
        hi, lo = _split_bf16(gd)
        tri = tri.astype(BF16)
        bc = (jnp.dot(tri, hi, preferred_element_type=F32)
              + jnp.dot(tri, lo, preferred_element_type=F32))
        rho = bc[mid_row:mid_row + 1, :]
        tot = bc[end_row:end_row + 1, :]
        e1 = bc - rho
        qm_ref[...] = (q * jnp.exp(jnp.minimum(e1, EXP_CLAMP))).astype(BF16)
        km_ref[...] = (k * jnp.exp(jnp.minimum(-e1, EXP_CLAMP))).astype(BF16)
        qe_ref[...] = (q * jnp.exp(bc)).astype(BF16)
        k2 = (k * jnp.exp(tot - bc)).astype(BF16)
        for p in range(GLA_HEADS // 2):
            kp = k2[:, p * HEAD_PAIR:(p + 1) * HEAD_PAIR]
            ut = jnp.zeros((GLA_DV, HEAD_PAIR), F32)
            for hh in range(2):
                h = 2 * p + hh
                keep = (lane < GLA_DK) if hh == 0 else (lane >= GLA_DK)
                kh = jnp.where(keep, kp, jnp.zeros_like(kp))
                ut = ut + lax.dot_general(vb[:, h * GLA_DV:(h + 1) * GLA_DV], kh, TN_DIMS,
                                          preferred_element_type=F32)
            ut_ref[0, :, p * HEAD_PAIR:(p + 1) * HEAD_PAIR] = ut
        return jnp.exp(tot)

    df = direction(g[:, :QK_DIM], ci <= ri, half - 1, c - 1, qmf_ref, kmf_ref, qef_ref, utf_ref)
    db = direction(g[:, QK_DIM:], ci >= ri, half, 0, qmb_ref, kmb_ref, qeb_ref, utb_ref)
    dtot_ref[0] = jnp.concatenate([df, db, jnp.zeros((6, QK_DIM), F32)], axis=0)


def _gla_prep(qkv, dlow, wup, bdec):
    m = qkv.shape[0]
    c = ROW_BLOCK
    nb = m // c
    qk_spec = pl.BlockSpec((c, QK_DIM), lambda i: (i, 0))
    bf_qk = jax.ShapeDtypeStruct((m, QK_DIM), BF16)
    ut_shape = jax.ShapeDtypeStruct((nb, GLA_DV, QK_DIM), F32)
    ut_spec = pl.BlockSpec((1, GLA_DV, QK_DIM), lambda i: (i, 0, 0))
    return pl.pallas_call(
        _gla_prep_kernel,
        out_shape=(bf_qk,) * 6 + (ut_shape, ut_shape, jax.ShapeDtypeStruct((nb, 8, QK_DIM), F32)),
        grid=(nb,),
        in_specs=[pl.BlockSpec((c, QK_DIM), lambda i: (i, 0)),
                  pl.BlockSpec((c, QK_DIM), lambda i: (i, 1)),
                  pl.BlockSpec((c, V_DIM), lambda i: (i, 1)),
                  pl.BlockSpec((c, LANES), lambda i: (i, 0)),
                  pl.BlockSpec((LANES, 4 * QK_DIM), lambda i: (0, 0)),
                  pl.BlockSpec((1, 2 * QK_DIM), lambda i: (0, 0))],
        out_specs=(qk_spec,) * 6 + (ut_spec, ut_spec,
                                    pl.BlockSpec((1, 8, QK_DIM), lambda i: (i, 0, 0))),
        compiler_params=_params("parallel"), name="gla_prep",
    )(qkv, qkv, qkv, dlow, wup, bdec)


def _gla_scan_kernel(utf_ref, dtf_ref, utb_ref, dtb_ref, stf_ref, stb_ref, sf_ref, sb_ref):
    @pl.when(pl.program_id(0) == 0)
    def _():
        sf_ref[...] = jnp.zeros_like(sf_ref)
        sb_ref[...] = jnp.zeros_like(sb_ref)

    sf = sf_ref[...]
    stf_ref[0] = sf.astype(BF16)
    sf_ref[...] = dtf_ref[0, 0:1, :] * sf + utf_ref[0]
    sb = sb_ref[...]
    stb_ref[0] = sb.astype(BF16)
    sb_ref[...] = dtb_ref[0, 1:2, :] * sb + utb_ref[0]


def _gla_scan(utf, utb, dtot, ctx_blocks):
    nb = utf.shape[0]

    def bwd(t):
        return jnp.where(t < ctx_blocks, ctx_blocks - 1 - t, nb + ctx_blocks - 1 - t)

    st_shape = jax.ShapeDtypeStruct((nb, GLA_DV, QK_DIM), BF16)
    blk = (1, GLA_DV, QK_DIM)
    return pl.pallas_call(
        _gla_scan_kernel,
        out_shape=(st_shape, st_shape),
        grid=(nb,),
        in_specs=[pl.BlockSpec(blk, lambda t: (t, 0, 0)),
                  pl.BlockSpec((1, 8, QK_DIM), lambda t: (t, 0, 0)),
                  pl.BlockSpec(blk, lambda t: (bwd(t), 0, 0)),
                  pl.BlockSpec((1, 8, QK_DIM), lambda t: (bwd(t), 0, 0))],
        out_specs=(pl.BlockSpec(blk, lambda t: (t, 0, 0)),
                   pl.BlockSpec(blk, lambda t: (bwd(t), 0, 0))),
        scratch_shapes=[pltpu.VMEM((GLA_DV, QK_DIM), F32), pltpu.VMEM((GLA_DV, QK_DIM), F32)],
        compiler_params=_params("arbitrary"), name="gla_scan",
    )(utf, dtot, utb, dtot)


def _gla_out_kernel(qmf_ref, kmf_ref, qef_ref, qmb_ref, kmb_ref, qeb_ref, vb_ref,
                    stf_ref, stb_ref, r_ref, gain_ref, a_ref):
    c = qmf_ref.shape[0]
    ri = lax.broadcasted_iota(jnp.int32, (c, c), 0)
    ci = lax.broadcasted_iota(jnp.int32, (c, c), 1)
    lower = ci <= ri
    upper = ci >= ri
    lane = lax.broadcasted_iota(jnp.int32, (1, HEAD_PAIR), 1)
    for p in range(GLA_HEADS // 2):
        pair = slice(p * HEAD_PAIR, (p + 1) * HEAD_PAIR)
        qmf, kmf, qef = qmf_ref[:, pair], kmf_ref[:, pair], qef_ref[:, pair]
        qmb, kmb, qeb = qmb_ref[:, pair], kmb_ref[:, pair], qeb_ref[:, pair]
        stf = stf_ref[0, :, pair]
        stb = stb_ref[0, :, pair]
        for hh in range(2):
            h = 2 * p + hh
            keep = (lane < GLA_DK) if hh == 0 else (lane >= GLA_DK)
            zero = jnp.zeros_like(kmf)
            af = lax.dot_general(qmf, jnp.where(keep, kmf, zero), NT_DIMS, preferred_element_type=F32)
            ab = lax.dot_general(qmb, jnp.where(keep, kmb, zero), NT_DIMS, preferred_element_type=F32)
            att = (jnp.where(lower, af, 0.0) + jnp.where(upper, ab, 0.0)).astype(BF16)
            cols = slice(h * GLA_DV, (h + 1) * GLA_DV)
            o = jnp.dot(att, vb_ref[:, cols], preferred_element_type=F32)
            o = o + lax.dot_general(jnp.where(keep, qef, zero), stf, NT_DIMS, preferred_element_type=F32)
            o = o + lax.dot_general(jnp.where(keep, qeb, zero), stb, NT_DIMS, preferred_element_type=F32)
            o = o * lax.rsqrt(jnp.mean(o * o, axis=-1, keepdims=True) + EPS)
            r = r_ref[:, cols].astype(F32)
            a_ref[:, cols] = (o * gain_ref[:, cols] * (r * jax.nn.sigmoid(r))).astype(BF16)


def _gla_out(prep, qkv, stf, stb, p_ruv, gain):
    qmf, kmf, qef, qmb, kmb, qeb = prep
    m = qkv.shape[0]
    c = ROW_BLOCK
    qk_spec = pl.BlockSpec((c, QK_DIM), lambda i: (i, 0))
    st_spec = pl.BlockSpec((1, GLA_DV, QK_DIM), lambda i: (i, 0, 0))
    return pl.pallas_call(
        _gla_out_kernel,
        out_shape=jax.ShapeDtypeStruct((m, V_DIM), BF16),
        grid=(m // c,),
        in_specs=[qk_spec] * 6 + [pl.BlockSpec((c, V_DIM), lambda i: (i, 1)), st_spec, st_spec,
                                  pl.BlockSpec((c, V_DIM), lambda i: (i, RUV_COL_R)),
                                  pl.BlockSpec((1, V_DIM), lambda i: (0, 0))],
        out_specs=pl.BlockSpec((c, V_DIM), lambda i: (i, 0)),
        compiler_params=_params("parallel"), name="gla_out",
    )(qmf, kmf, qef, qmb, kmb, qeb, qkv, stf, stb, p_ruv, gain.reshape(1, V_DIM))


ONE_F32_BITS = 0x3F800000
SEARCH_STEPS = 31


def _route_kernel(logit_ref, g_ref, *, ctx_rows):
    m = logit_ref.shape[0]
    rb = ROW_BLOCK
    lane = lax.broadcasted_iota(jnp.int32, (1, LANES), 1)
    valid = lane < N_EXPERTS

    def softmax_block(i, carry):
        r0 = pl.multiple_of(i * rb, rb)
        lg = jnp.where(valid, logit_ref[pl.ds(r0, rb), :], -jnp.inf)
        ex = jnp.exp(lg - jnp.max(lg, axis=-1, keepdims=True))
        g_ref[pl.ds(r0, rb), :] = ex / jnp.sum(ex, axis=-1, keepdims=True)
        return carry

    lax.fori_loop(0, m // rb, softmax_block, 0)

    def bits_of(r0):
        return lax.bitcast_convert_type(g_ref[pl.ds(r0, rb), :], jnp.int32)

    def count_ge(blk0, nblk, t):
        def body(i, acc):
            r0 = pl.multiple_of((blk0 + i) * rb, rb)
            hit = jnp.where(bits_of(r0) >= t, 1, 0)
            return acc + jnp.sum(hit.reshape(rb // 8, 8, LANES), axis=0)

        acc = lax.fori_loop(0, nblk, body, jnp.zeros((8, LANES), jnp.int32))
        return jnp.sum(acc, axis=0, keepdims=True)

    def route_set(blk0, nblk):
        cap = EC_CAPACITY * nblk * rb // N_EXPERTS

        def halve(_, lohi):
            lo, hi = lohi
            mid = lo + ((hi - lo) >> 1)
            ok = count_ge(blk0, nblk, mid) >= cap
            return jnp.where(ok, mid, lo), jnp.where(ok, hi, mid)

        thr, _ = lax.fori_loop(0, SEARCH_STEPS, halve,
                               (jnp.zeros((1, LANES), jnp.int32),
                                jnp.full((1, LANES), ONE_F32_BITS + 1, jnp.int32)))
        need = (cap - count_ge(blk0, nblk, thr + 1)).astype(F32)
        ri = lax.broadcasted_iota(jnp.int32, (rb, rb), 0)
        ci = lax.broadcasted_iota(jnp.int32, (rb, rb), 1)
        earlier = jnp.where(ci < ri, 1.0, 0.0).astype(BF16)

        def finalize(i, seen):
            r0 = pl.multiple_of((blk0 + i) * rb, rb)
            aff = g_ref[pl.ds(r0, rb), :]
            bits = lax.bitcast_convert_type(aff, jnp.int32)
            tie = bits == thr
            tie_f = jnp.where(tie, 1.0, 0.0)
            rank = seen + jnp.dot(earlier, tie_f.astype(BF16), preferred_element_type=F32)
            g_ref[pl.ds(r0, rb), :] = jnp.where(tie, jnp.where(rank < need, aff, 0.0),
                                                jnp.where(bits > thr, aff, 0.0))
            return seen + jnp.sum(tie_f, axis=0, keepdims=True)

        lax.fori_loop(0, nblk, finalize, jnp.zeros((1, LANES), F32))

    route_set(0, ctx_rows // rb)
    route_set(ctx_rows // rb, (m - ctx_rows) // rb)


def _route(logits, ctx_rows):
    m = logits.shape[0]
    return pl.pallas_call(
        functools.partial(_route_kernel, ctx_rows=ctx_rows),
        out_shape=jax.ShapeDtypeStruct((m, LANES), F32),
        grid=(1,),
        in_specs=[pl.BlockSpec((m, LANES), lambda i: (0, 0))],
        out_specs=pl.BlockSpec((m, LANES), lambda i: (0, 0)),
        compiler_params=_params("arbitrary"), name="route",
    )(logits)


SLOTS_PER_BLOCK = 64
GATE_PARTS = 3


def _slot_ranks(g):
    rb = g.shape[0]
    picked = jnp.where(g > 0.0, 1.0, 0.0)
    ri = lax.broadcasted_iota(jnp.int32, (rb, rb), 0)
    ci = lax.broadcasted_iota(jnp.int32, (rb, rb), 1)
    earlier = jnp.where(ci < ri, 1.0, 0.0).astype(BF16)
    rank = jnp.dot(earlier, picked.astype(BF16), preferred_element_type=F32)
    return jnp.where(picked > 0.0, rank, -1.0), jnp.sum(picked, axis=0, keepdims=True)


def _slots_kernel(g_ref, pos_ref):
    pos_ref[...] = _slot_ranks(g_ref[...])[0]


def _slots(gates):
    m = gates.shape[0]
    rb = ROW_BLOCK
    return pl.pallas_call(
        _slots_kernel,
        out_shape=jax.ShapeDtypeStruct((m, LANES), F32),
        grid=(m // rb,),
        in_specs=[pl.BlockSpec((rb, LANES), lambda i: (i, 0))],
        out_specs=pl.BlockSpec((rb, LANES), lambda i: (i, 0)),
        compiler_params=_params("parallel"), name="moe_slots",
    )(gates)


def _slot_onehot(post):
    rb = post.shape[1]
    slot = lax.broadcasted_iota(jnp.int32, (SLOTS_PER_BLOCK, rb), 0).astype(F32)
    parts = [jnp.where(slot == post[e:e + 1, :], 1.0, 0.0) for e in range(N_EXPERTS)]
    return jnp.concatenate(parts, axis=0).astype(BF16)


def _dispatch_kernel(h_ref, g_ref, xs_ref, gs_ref, post_ref, cnt_ref):
    pos, cnt = _slot_ranks(g_ref[...])
    post = pos.T[:N_EXPERTS, :]
    post_ref[...] = post
    cnt_ref[0] = jnp.broadcast_to(cnt, (8, LANES))
    onehot = _slot_onehot(post)
    d = h_ref.shape[1]
    xs = jnp.dot(onehot, h_ref[...], preferred_element_type=F32)
    xs_ref[...] = xs.astype(BF16).reshape(N_EXPERTS, SLOTS_PER_BLOCK, d)
    g = g_ref[...]
    pieces = jnp.zeros_like(g)
    for k in range(GATE_PARTS):
        piece = g.astype(BF16).astype(F32)
        pieces = pieces + (pltpu.roll(piece, k * N_EXPERTS, axis=1) if k else piece)
        g = g - piece
    gs = jnp.dot(onehot, pieces.astype(BF16), preferred_element_type=F32)
    gs_ref[...] = gs.reshape(N_EXPERTS, SLOTS_PER_BLOCK, LANES)


def _dispatch(h, gates):
    m, d = h.shape
    rb = ROW_BLOCK
    nb = m // rb
    s = SLOTS_PER_BLOCK
    return pl.pallas_call(
        _dispatch_kernel,
        out_shape=(jax.ShapeDtypeStruct((N_EXPERTS, nb * s, d), BF16),
                   jax.ShapeDtypeStruct((N_EXPERTS, nb * s, LANES), F32),
                   jax.ShapeDtypeStruct((N_EXPERTS, m), F32),
                   jax.ShapeDtypeStruct((nb, 8, LANES), F32)),
        grid=(nb,),
        in_specs=[pl.BlockSpec((rb, d), lambda i: (i, 0)),
                  pl.BlockSpec((rb, LANES), lambda i: (i, 0))],
        out_specs=(pl.BlockSpec((N_EXPERTS, s, d), lambda i: (0, i, 0)),
                   pl.BlockSpec((N_EXPERTS, s, LANES), lambda i: (0, i, 0)),
                   pl.BlockSpec((N_EXPERTS, rb), lambda i: (0, i)),
                   pl.BlockSpec((1, 8, LANES), lambda i: (i, 0, 0))),
        compiler_params=_params("parallel"), name="moe_dispatch",
    )(h, gates)


def _expert_kernel(xs_ref, w1_ref, w3_ref, w2_ref, gs_ref, y_ref):
    e = pl.program_id(0)
    lane = lax.broadcasted_iota(jnp.int32, (1, LANES), 1)
    mine = (lane % N_EXPERTS == e) & (lane < GATE_PARTS * N_EXPERTS)
    gate = jnp.sum(jnp.where(mine, gs_ref[0], 0.0), axis=-1, keepdims=True)
    x = xs_ref[0]
    h1 = jnp.dot(x, w1_ref[0], preferred_element_type=F32)
    h3 = jnp.dot(x, w3_ref[0], preferred_element_type=F32)
    hid = (h1 * jax.nn.sigmoid(h1) * h3 * gate).astype(BF16)
    y_ref[0] = jnp.dot(hid, w2_ref[0], preferred_element_type=F32).astype(BF16)


def _expert_ffn(xs, gs, w1, w3, w2, layer):
    e, rows, d = xs.shape
    f = w1.shape[-1]
    tr = _pick(rows, (832, 1024, 512, 256, 128, 64, 16))
    return pl.pallas_call(
        _expert_kernel,
        out_shape=jax.ShapeDtypeStruct((e, rows, d), BF16),
        grid=(e, rows // tr),
        in_specs=[pl.BlockSpec((1, tr, d), lambda i, j: (i, j, 0)),
                  pl.BlockSpec((None, 1, d, f), lambda i, j: (layer, i, 0, 0)),
                  pl.BlockSpec((None, 1, d, f), lambda i, j: (layer, i, 0, 0)),
                  pl.BlockSpec((None, 1, f, d), lambda i, j: (layer, i, 0, 0)),
                  pl.BlockSpec((1, tr, LANES), lambda i, j: (i, j, 0))],
        out_specs=pl.BlockSpec((1, tr, d), lambda i, j: (i, j, 0)),
        compiler_params=_params("parallel", "arbitrary"), name="expert_ffn",
    )(xs, w1, w3, w2, gs)


def _combined_rows(ys_ref, post_ref, x_ref, mod_ref, gate_row):
    onehot = _slot_onehot(post_ref[...])
    d = x_ref.shape[1]
    ys = ys_ref[...].reshape(N_EXPERTS * SLOTS_PER_BLOCK, d)
    moe = lax.dot_general(onehot, ys, TN_DIMS, preferred_element_type=F32)
    return x_ref[...] + mod_ref[0, gate_row:gate_row + 1, :] * moe


def _combine_prenorm_kernel(ys_ref, post_ref, x_ref, mod_ref, g_ref, nmod_ref, o_ref, h_ref, *,
                            gate_row):
    x = _combined_rows(ys_ref, post_ref, x_ref, mod_ref, gate_row)
    o_ref[...] = x
    h_ref[...] = _modulated_norm(x, g_ref[...], nmod_ref[0], 0, 1).astype(BF16)


def _combine_final_kernel(ys_ref, post_ref, x_ref, mod_ref, g_ref, o_ref, *, gate_row):
    x = _combined_rows(ys_ref, post_ref, x_ref, mod_ref, gate_row)
    ms = jnp.mean(x * x, axis=-1, keepdims=True)
    o_ref[...] = x * lax.rsqrt(ms + EPS) * g_ref[...]


def _combine(ys, post, x, mods, ctx_rows, gate_row, norm_g, next_mods=None):
    m, d = x.shape
    rb = ROW_BLOCK
    s = SLOTS_PER_BLOCK
    stream = _stream_of_block(rb, ctx_rows)
    row_spec = pl.BlockSpec((rb, d), lambda i: (i, 0))
    mod_spec = pl.BlockSpec((1, 8, d), lambda i: (stream(i), 0, 0))
    in_specs = [pl.BlockSpec((N_EXPERTS, s, d), lambda i: (0, i, 0)),
                pl.BlockSpec((N_EXPERTS, rb), lambda i: (0, i)),
                row_spec, mod_spec, pl.BlockSpec((1, d), lambda i: (0, 0))]
    if next_mods is not None:
        return pl.pallas_call(
            functools.partial(_combine_prenorm_kernel, gate_row=gate_row),
            out_shape=(jax.ShapeDtypeStruct((m, d), F32), jax.ShapeDtypeStruct((m, d), BF16)),
            grid=(m // rb,),
            in_specs=in_specs + [mod_spec],
            out_specs=(row_spec, row_spec),
            compiler_params=_params("parallel"), name="moe_combine",
        )(ys, post, x, mods, norm_g.reshape(1, d), next_mods)
    skip = ctx_rows // rb
    return pl.pallas_call(
        functools.partial(_combine_final_kernel, gate_row=gate_row),
        out_shape=jax.ShapeDtypeStruct((m - ctx_rows, d), F32),
        grid=(m // rb,),
        in_specs=in_specs,
        out_specs=pl.BlockSpec((rb, d), lambda i: (jnp.maximum(i - skip, 0), 0)),
        compiler_params=_params("arbitrary"), name="moe_combine_final",
    )(ys, post, x, mods, norm_g.reshape(1, d))


def _moe_hidden_kernel(h_ref, w1_ref, w3_ref, g_ref, o_ref):
    e = pl.program_id(1)
    lane = lax.broadcasted_iota(jnp.int32, (1, LANES), 1)
    gate = jnp.sum(jnp.where(lane == e, g_ref[...], 0.0), axis=-1, keepdims=True)
    h = h_ref[...]
    h1 = jnp.dot(h, w1_ref[0].astype(BF16), preferred_element_type=F32)
    h3 = jnp.dot(h, w3_ref[0].astype(BF16), preferred_element_type=F32)
    o_ref[...] = (h1 * jax.nn.sigmoid(h1) * h3 * gate).astype(BF16)


def _moe_hidden(h, gates, w1, w3, layer):
    m, d = h.shape
    _, e, _, f = w1.shape
    tm = _pick(m, (1280, 1024, 768, 512, 256))
    return pl.pallas_call(
        _moe_hidden_kernel,
        out_shape=jax.ShapeDtypeStruct((m, e * f), BF16),
        grid=(m // tm, e),
        in_specs=[pl.BlockSpec((tm, d), lambda i, j: (i, 0)),
                  pl.BlockSpec((None, 1, d, f), lambda i, j: (layer, j, 0, 0)),
                  pl.BlockSpec((None, 1, d, f), lambda i, j: (layer, j, 0, 0)),
                  pl.BlockSpec((tm, LANES), lambda i, j: (i, 0))],
        out_specs=pl.BlockSpec((tm, f), lambda i, j: (i, j)),
        compiler_params=_params("parallel", "arbitrary"), name="moe_hidden",
    )(h, w1, w3, gates)


def kernel(x, c, ctx, c_ctx, ada_w, ada_b, norm1_g, norm2_g, w_in, w_dec_up, b_dec, gla_norm_g,
           sg_ln_g, sg_ln_b, sg_w, sg_b, w_branch_a, w_branch_b, w_out, w_router, w_exp1, w_exp3,
           w_exp2, final_g):
    batch, seq, d = x.shape
    ctx_rows = ctx.shape[1]
    depth = ada_w.shape[0]
    assert batch == 1 and seq % ROW_BLOCK == 0 and ctx_rows % ROW_BLOCK == 0

    cond2 = jnp.stack([c_ctx, c[0]], axis=1)
    mods_all = _adaln(cond2, ada_w, ada_b).reshape(depth, 2, 6, d)
    mods_all = jnp.pad(mods_all, ((0, 0), (0, 0), (0, 2), (0, 0)))

    w1_all, w3_all, w2_all = (w.astype(BF16) for w in (w_exp1, w_exp3, w_exp2))
    w2_flat = w2_all.reshape(depth, N_EXPERTS * EXPERT_FF, d)
    sg_w_all = sg_w.astype(BF16)
    n_in = w_in.shape[2]
    w_in_rows = jnp.swapaxes(w_in, 1, 2).reshape(depth * n_in, d)

    for l in range(depth):
        mods = mods_all[l]
        w_head_t = _cast_rows(w_in_rows, l * n_in, OFF_DF + LANES)
        w_rest_t = _cast_rows(w_in_rows, l * n_in + SCAN_COLS, n_in - SCAN_COLS)
        wup = jnp.zeros((LANES, 2 * QK_DIM), F32)
        wup = wup.at[:DECAY_RANK, :QK_DIM].set(w_dec_up[l, 0])
        wup = wup.at[DECAY_RANK:2 * DECAY_RANK, QK_DIM:].set(w_dec_up[l, 1])
        bdec = b_dec[l].reshape(1, 2 * QK_DIM)
        b_s_bcast = jnp.broadcast_to(sg_b[l][:, :, None], (SG_GROUPS, SG_CHUNK, SG_GROUP_CH))
        w_r = jnp.pad(w_router[l], ((0, 0), (0, LANES - N_EXPERTS)))

        if l == 0:
            xa, h = _stack_prenorm(ctx[0], x[0], norm1_g[l], mods)
        p_qkv, dlow = _matmul_nt_with_side(h, w_head_t, OFF_DF, BF16)
        p_ruv = _matmul_nt(h, w_rest_t, 3 * V_DIM, BF16)

        prep = _gla_prep(p_qkv, dlow, jnp.concatenate(_split_bf16(wup), axis=1), bdec)
        stf, stb = _gla_scan(prep[6], prep[7], prep[8], ctx_rows // ROW_BLOCK)
        a = _gla_out(prep[:6], p_qkv, stf, stb, p_ruv, gla_norm_g[l])
        s = _sgu(p_ruv, sg_ln_g[l], sg_ln_b[l], sg_w_all, l, b_s_bcast)
        mix = _merge(h, a, s, w_rest_t, 3 * V_DIM, w_branch_a, w_branch_b, l)
        xa = _matmul_gated_residual(mix, w_out, xa, mods, ctx_rows, gate_row=2, layer=l)

        h2, logits = _prenorm_router(xa, norm2_g[l], mods, ctx_rows, 3, 4, w_r)
        gates = _route(logits, ctx_rows)
        xs, gs, post, cnt = _dispatch(h2, gates)
        ys = _expert_ffn(xs, gs, w1_all, w3_all, w2_all, l)

        def combine(xa, l=l, ys=ys, post=post, mods=mods):
            if l + 1 < depth:
                return _combine(ys, post, xa, mods, ctx_rows, 5, norm1_g[l + 1], mods_all[l + 1])
            return _combine(ys, post, xa, mods, ctx_rows, 5, final_g)

        def combine_after_overflow(xa, l=l, h2=h2, gates=gates, mods=mods, combine=combine):
            late = jnp.where(_slots(gates) >= SLOTS_PER_BLOCK, gates, 0.0)
            hid = _moe_hidden(h2, late, w1_all, w3_all, l)
            return combine(_matmul_gated_residual(hid, w2_flat, xa, mods, ctx_rows, gate_row=5, layer=l))

        res = lax.cond(jnp.max(cnt) > SLOTS_PER_BLOCK, combine_after_overflow, combine, xa)
        if l + 1 < depth:
            xa, h = res
        else:
            out = res

    return out[None]
```

```python
import functools

import jax
import jax.numpy as jnp
from jax import lax
from jax.experimental import pallas as pl
from jax.experimental.pallas import tpu as pltpu

EPS = 1e-6
GLA_HEADS = 8
GLA_DK = 64
GLA_DV = 128
DECAY_RANK = 16
GLA_TAU = 16.0
SG_GROUPS = 8
SG_GROUP_CH = 128
SG_CHUNK = 128
N_EXPERTS = 16
EXPERT_FF = 256
EC_CAPACITY = 2

QK_DIM = GLA_HEADS * GLA_DK
V_DIM = GLA_HEADS * GLA_DV
SG_DIM = SG_GROUPS * SG_GROUP_CH
OFF_K = QK_DIM
OFF_V = 2 * QK_DIM
OFF_DF = OFF_V + V_DIM
OFF_DB = OFF_DF + DECAY_RANK
SCAN_COLS = OFF_DB + DECAY_RANK
OFF_U = SCAN_COLS + V_DIM
OFF_VS = OFF_U + SG_DIM
OFF_GA = OFF_VS + SG_DIM

RUV_COL_R, RUV_COL_U, RUV_COL_VS = 0, 1, 2

LANES = 128
ROW_BLOCK = 256
HEAD_PAIR = 2 * GLA_DK
EXP_CLAMP = 80.0
VMEM_LIMIT_BYTES = 52 * 1024 * 1024

F32 = jnp.float32
BF16 = jnp.bfloat16
NT_DIMS = (((1,), (1,)), ((), ()))
TN_DIMS = (((0,), (0,)), ((), ()))


def _pick(n, candidates):
    for c in candidates:
        if n % c == 0:
            return c
    raise ValueError(f"no tile of {candidates} divides {n}")


def _params(*semantics):
    return pltpu.CompilerParams(dimension_semantics=semantics,
                                vmem_limit_bytes=VMEM_LIMIT_BYTES)


def _stream_of_block(tm, ctx_rows):
    return lambda i: (i * tm >= ctx_rows).astype(jnp.int32)


def _adaln_kernel(cond_ref, w_ref, b_ref, out_ref, *, rows):
    d = cond_ref.shape[0]
    tn = out_ref.shape[-1]

    def body(i, acc):
        r0 = pl.multiple_of(i * rows, rows)
        cond = cond_ref[pl.ds(r0, rows), :]
        s = cond * jax.nn.sigmoid(cond)
        w = w_ref[0, pl.ds(r0, rows), :]
        a0 = jnp.sum(w * s[:, 0:1], axis=0, keepdims=True)
        a1 = jnp.sum(w * s[:, 1:2], axis=0, keepdims=True)
        return acc + jnp.concatenate([a0, a1], axis=0)

    acc = lax.fori_loop(0, d // rows, body, jnp.zeros((2, tn), F32))
    out_ref[0] = acc + b_ref[0]


def _adaln(cond2, ada_w, ada_b):
    nl, d, n6 = ada_w.shape
    tn = _pick(n6, (1024, 512, 256, 128))
    return pl.pallas_call(
        functools.partial(_adaln_kernel, rows=_pick(d, (256, 128, 8))),
        out_shape=jax.ShapeDtypeStruct((nl, 2, n6), F32),
        grid=(nl, n6 // tn),
        in_specs=[pl.BlockSpec((d, 2), lambda l, j: (0, 0)),
                  pl.BlockSpec((1, d, tn), lambda l, j: (l, 0, j)),
                  pl.BlockSpec((1, 1, tn), lambda l, j: (l, 0, j))],
        out_specs=pl.BlockSpec((1, 2, tn), lambda l, j: (l, 0, j)),
        compiler_params=_params("parallel", "parallel"),
        name="adaln",
    )(cond2, ada_w, ada_b.reshape(nl, 1, n6))


def _modulated_norm(x, g, mod, shift_row, scale_row):
    ms = jnp.mean(x * x, axis=-1, keepdims=True)
    y = x * lax.rsqrt(ms + EPS) * g
    return y * (1.0 + mod[scale_row:scale_row + 1, :]) + mod[shift_row:shift_row + 1, :]


def _stack_prenorm_kernel(ctx_ref, x_ref, g_ref, mod_ref, xa_ref, h_ref, *, ctx_blocks):
    xin = jnp.where(pl.program_id(0) < ctx_blocks, ctx_ref[...], x_ref[...])
    xa_ref[...] = xin
    h_ref[...] = _modulated_norm(xin, g_ref[...], mod_ref[0], 0, 1).astype(BF16)


def _stack_prenorm(ctx2d, x2d, g, mods):
    ctx_rows, d = ctx2d.shape
    m = ctx_rows + x2d.shape[0]
    tm = ROW_BLOCK
    nc = ctx_rows // tm
    row_spec = pl.BlockSpec((tm, d), lambda i: (i, 0))
    return pl.pallas_call(
        functools.partial(_stack_prenorm_kernel, ctx_blocks=nc),
        out_shape=(jax.ShapeDtypeStruct((m, d), F32), jax.ShapeDtypeStruct((m, d), BF16)),
        grid=(m // tm,),
        in_specs=[pl.BlockSpec((tm, d), lambda i: (jnp.minimum(i, nc - 1), 0)),
                  pl.BlockSpec((tm, d), lambda i: (jnp.maximum(i - nc, 0), 0)),
                  pl.BlockSpec((1, d), lambda i: (0, 0)),
                  pl.BlockSpec((1, 8, d), lambda i: (_stream_of_block(tm, ctx_rows)(i), 0, 0))],
        out_specs=(row_spec, row_spec),
        compiler_params=_params("parallel"), name="stack_prenorm",
    )(ctx2d, x2d, g.reshape(1, d), mods)


def _prenorm_router_kernel(x_ref, g_ref, mod_ref, wr_ref, h_ref, logit_ref, *, shift_row, scale_row):
    h = _modulated_norm(x_ref[...], g_ref[...], mod_ref[0], shift_row, scale_row)
    h_hi, h_lo = _split_bf16(h)
    h_ref[...] = h_hi
    hi = jnp.dot(h_hi, wr_ref[...], preferred_element_type=F32)
    lo = jnp.dot(h_lo, wr_ref[:, :LANES], preferred_element_type=F32)
    logit_ref[...] = hi[:, :LANES] + hi[:, LANES:] + lo


def _prenorm_router(x, g, mods, ctx_rows, shift_row, scale_row, w_router):
    m, d = x.shape
    tm = ROW_BLOCK
    stream = _stream_of_block(tm, ctx_rows)
    row_spec = pl.BlockSpec((tm, d), lambda i: (i, 0))
    return pl.pallas_call(
        functools.partial(_prenorm_router_kernel, shift_row=shift_row, scale_row=scale_row),
        out_shape=(jax.ShapeDtypeStruct((m, d), BF16), jax.ShapeDtypeStruct((m, LANES), F32)),
        grid=(m // tm,),
        in_specs=[row_spec,
                  pl.BlockSpec((1, d), lambda i: (0, 0)),
                  pl.BlockSpec((1, 8, d), lambda i: (stream(i), 0, 0)),
                  pl.BlockSpec((d, 2 * LANES), lambda i: (0, 0))],
        out_specs=(row_spec, pl.BlockSpec((tm, LANES), lambda i: (i, 0))),
        compiler_params=_params("parallel"), name="prenorm_router",
    )(x, g.reshape(1, d), mods, jnp.concatenate(_split_bf16(w_router), axis=1))


def _cast_kernel(w_ref, o_ref):
    o_ref[...] = w_ref[...].astype(BF16)


def _cast_rows(w_rows, row0, n_rows):
    k = w_rows.shape[1]
    tr = _pick(n_rows, (512, 544, 256, 128))
    assert row0 % 8 == 0
    return pl.pallas_call(
        _cast_kernel,
        out_shape=jax.ShapeDtypeStruct((n_rows, k), BF16),
        grid=(n_rows // tr,),
        in_specs=[pl.BlockSpec((pl.Element(tr), pl.Element(k)),
                               lambda j: (pl.multiple_of(row0 + j * tr, 8), 0))],
        out_specs=pl.BlockSpec((tr, k), lambda j: (j, 0)),
        compiler_params=_params("parallel"), name="cast_rows",
    )(w_rows)


def _weight_spec(w, layer, rows, tn, col_block):
    if w.ndim == 2:
        return pl.BlockSpec((rows, tn), lambda i, j: (0, col_block(j)))
    return pl.BlockSpec((None, rows, tn), lambda i, j: (layer, 0, col_block(j)))


def _mm_nt_kernel(a_ref, wt_ref, o_ref):
    o_ref[...] = lax.dot_general(a_ref[...], wt_ref[...], NT_DIMS,
                                 preferred_element_type=F32).astype(o_ref.dtype)


def _matmul_nt(a, w_t, n_rows, out_dtype):
    m, k = a.shape
    n = n_rows
    tm = _pick(m, (1280, 1024, 768, 512, 256))
    tn = _pick(n, (512, 256, 128))
    return pl.pallas_call(
        _mm_nt_kernel,
        out_shape=jax.ShapeDtypeStruct((m, n), out_dtype),
        grid=(m // tm, n // tn),
        in_specs=[pl.BlockSpec((tm, k), lambda i, j: (i, 0)),
                  pl.BlockSpec((tn, k), lambda i, j: (j, 0))],
        out_specs=pl.BlockSpec((tm, tn), lambda i, j: (i, j)),
        compiler_params=_params("parallel", "arbitrary"), name="matmul",
    )(a, w_t)


def _mm_nt_side_kernel(a_ref, wt_ref, ws_ref, o_ref, side_ref):
    a = a_ref[...]
    o_ref[...] = lax.dot_general(a, wt_ref[...], NT_DIMS, preferred_element_type=F32).astype(o_ref.dtype)

    @pl.when(pl.program_id(1) == 0)
    def _():
        side_ref[...] = lax.dot_general(a, ws_ref[...], NT_DIMS, preferred_element_type=F32)


def _matmul_nt_with_side(a, w_t, n_rows, out_dtype):
    m, k = a.shape
    tm = _pick(m, (1280, 1024, 768, 512, 256))
    tn = _pick(n_rows, (512, 256, 128))
    assert w_t.shape[0] - n_rows == LANES and n_rows % LANES == 0
    return pl.pallas_call(
        _mm_nt_side_kernel,
        out_shape=(jax.ShapeDtypeStruct((m, n_rows), out_dtype), jax.ShapeDtypeStruct((m, LANES), F32)),
        grid=(m // tm, n_rows // tn),
        in_specs=[pl.BlockSpec((tm, k), lambda i, j: (i, 0)),
                  pl.BlockSpec((tn, k), lambda i, j: (j, 0)),
                  pl.BlockSpec((LANES, k), lambda i, j: (n_rows // LANES, 0))],
        out_specs=(pl.BlockSpec((tm, tn), lambda i, j: (i, j)),
                   pl.BlockSpec((tm, LANES), lambda i, j: (i, 0))),
        compiler_params=_params("parallel", "arbitrary"), name="matmul_side",
    )(a, w_t, w_t)


def _mm_resid_kernel(a_ref, w_ref, x_ref, mod_ref, o_ref, *, gate_row, ctx_rows):
    tm = a_ref.shape[0]
    acc = jnp.dot(a_ref[...], w_ref[...].astype(BF16), preferred_element_type=F32)
    row = pl.program_id(0) * tm + lax.broadcasted_iota(jnp.int32, (tm, 1), 0)
    gate = jnp.where(row < ctx_rows, mod_ref[0, gate_row:gate_row + 1, :],
                     mod_ref[1, gate_row:gate_row + 1, :])
    o_ref[...] = x_ref[...] + gate * acc


def _matmul_gated_residual(a, w, x, mods, ctx_rows, gate_row, layer=None):
    m, k = a.shape
    n = w.shape[-1]
    tm = _pick(m, (1280, 1024, 768, 512, 256))
    tn = _pick(n, (512, 256, 128))
    return pl.pallas_call(
        functools.partial(_mm_resid_kernel, gate_row=gate_row, ctx_rows=ctx_rows),
        out_shape=jax.ShapeDtypeStruct((m, n), F32),
        grid=(m // tm, n // tn),
        in_specs=[pl.BlockSpec((tm, k), lambda i, j: (i, 0)),
                  _weight_spec(w, layer, k, tn, lambda j: j),
                  pl.BlockSpec((tm, tn), lambda i, j: (i, j)),
                  pl.BlockSpec((2, 8, tn), lambda i, j: (0, 0, j))],
        out_specs=pl.BlockSpec((tm, tn), lambda i, j: (i, j)),
        compiler_params=_params("parallel", "arbitrary"), name="out_proj",
    )(a, w, x, mods)


def _merge_kernel(h_ref, a_ref, s_ref, wga_ref, wgb_ref, wa_ref, wb_ref, m_ref):
    h = h_ref[...]
    ga = lax.dot_general(h, wga_ref[...], NT_DIMS, preferred_element_type=F32)
    gb = lax.dot_general(h, wgb_ref[...], NT_DIMS, preferred_element_type=F32)
    ya = jnp.dot(a_ref[...], wa_ref[...].astype(BF16), preferred_element_type=F32)
    yb = jnp.dot(s_ref[...], wb_ref[...].astype(BF16), preferred_element_type=F32)
    m_ref[...] = (jax.nn.sigmoid(ga) * ya + jax.nn.sigmoid(gb) * yb).astype(BF16)


def _merge(h, a, s, w_gates_t, row_ga, w_a, w_b, layer):
    m, d = h.shape
    ka = a.shape[1]
    kb = s.shape[1]
    tm = _pick(m, (1280, 1024, 768, 512, 256))
    tn = _pick(d, (256, 128))
    nb = d // tn
    j0 = row_ga // tn
    return pl.pallas_call(
        _merge_kernel,
        out_shape=jax.ShapeDtypeStruct((m, d), BF16),
        grid=(m // tm, nb),
        in_specs=[pl.BlockSpec((tm, d), lambda i, j: (i, 0)),
                  pl.BlockSpec((tm, ka), lambda i, j: (i, 0)),
                  pl.BlockSpec((tm, kb), lambda i, j: (i, 0)),
                  pl.BlockSpec((tn, d), lambda i, j: (j0 + j, 0)),
                  pl.BlockSpec((tn, d), lambda i, j: (j0 + nb + j, 0)),
                  _weight_spec(w_a, layer, ka, tn, lambda j: j),
                  _weight_spec(w_b, layer, kb, tn, lambda j: j)],
        out_specs=pl.BlockSpec((tm, tn), lambda i, j: (i, j)),
        compiler_params=_params("parallel", "arbitrary"), name="merge",
    )(h, a, s, w_gates_t, w_gates_t, w_a, w_b)


def _sgu_kernel(u_ref, vs_ref, lng_ref, lnb_ref, ws_ref, bs_ref, s_ref):
    tm = u_ref.shape[0]
    u = jax.nn.gelu(u_ref[...].astype(F32))
    v = jax.nn.gelu(vs_ref[...].astype(F32))
    vc = v - jnp.mean(v, axis=-1, keepdims=True)
    vn = vc * lax.rsqrt(jnp.mean(vc * vc, axis=-1, keepdims=True) + EPS)
    vn = (vn * lng_ref[...] + lnb_ref[...]).astype(BF16)
    for ch in range(tm // SG_CHUNK):
        rows = slice(ch * SG_CHUNK, (ch + 1) * SG_CHUNK)
        for g in range(SG_GROUPS):
            cols = slice(g * SG_GROUP_CH, (g + 1) * SG_GROUP_CH)
            mixed = jnp.dot(ws_ref[g], vn[rows, cols], preferred_element_type=F32) + bs_ref[g]
            s_ref[rows, cols] = (u[rows, cols] * mixed).astype(BF16)


def _sgu(p_ruv, ln_g, ln_b, w_s, layer, b_s_bcast):
    m = p_ruv.shape[0]
    tm = ROW_BLOCK
    return pl.pallas_call(
        _sgu_kernel,
        out_shape=jax.ShapeDtypeStruct((m, SG_DIM), BF16),
        grid=(m // tm,),
        in_specs=[pl.BlockSpec((tm, SG_DIM), lambda i: (i, RUV_COL_U)),
                  pl.BlockSpec((tm, SG_DIM), lambda i: (i, RUV_COL_VS)),
                  pl.BlockSpec((1, SG_DIM), lambda i: (0, 0)),
                  pl.BlockSpec((1, SG_DIM), lambda i: (0, 0)),
                  pl.BlockSpec((None, SG_GROUPS, SG_CHUNK, SG_CHUNK), lambda i: (layer, 0, 0, 0)),
                  pl.BlockSpec((SG_GROUPS, SG_CHUNK, SG_GROUP_CH), lambda i: (0, 0, 0))],
        out_specs=pl.BlockSpec((tm, SG_DIM), lambda i: (i, 0)),
        compiler_params=_params("parallel"), name="sgu",
    )(p_ruv, p_ruv, ln_g.reshape(1, SG_DIM), ln_b.reshape(1, SG_DIM), w_s, b_s_bcast)


def _split_bf16(x):
    hi = x.astype(BF16)
    lo = (x - hi.astype(F32)).astype(BF16)
    return hi, lo


def _gla_prep_kernel(q_ref, k_ref, v_ref, dl_ref, wup_ref, bdec_ref,
                     qmf_ref, kmf_ref, qef_ref, qmb_ref, kmb_ref, qeb_ref,
                     utf_ref, utb_ref, dtot_ref):
    c = q_ref.shape[0]
    half = c // 2
    n2 = 2 * QK_DIM
    dl_hi, dl_lo = _split_bf16(dl_ref[...])
    zz = jnp.dot(dl_hi, wup_ref[...], preferred_element_type=F32)
    z = (zz[:, :n2] + zz[:, n2:] + jnp.dot(dl_lo, wup_ref[:, :n2], preferred_element_type=F32)
         + bdec_ref[...])
    g = -(jnp.maximum(-z, 0.0) + jnp.log1p(jnp.exp(-jnp.abs(z)))) * (1.0 / GLA_TAU)
    ri = lax.broadcasted_iota(jnp.int32, (c, c), 0)
    ci = lax.broadcasted_iota(jnp.int32, (c, c), 1)
    lane = lax.broadcasted_iota(jnp.int32, (1, HEAD_PAIR), 1)
    q = q_ref[...].astype(F32) * (GLA_DK ** -0.5)
    k = k_ref[...].astype(F32)
    vb = v_ref[...]

    def direction(gd, tri, mid_row, end_row, qm_ref, km_ref, qe_ref, ut_ref):
        hi, lo = _split_bf16(gd)
        tri = tri.astype(BF16)
        bc = (jnp.dot(tri, hi, preferred_element_type=F32)
              + jnp.dot(tri, lo, preferred_element_type=F32))
        rho = bc[mid_row:mid_row + 1, :]
        tot = bc[end_row:end_row + 1, :]
        e1 = bc - rho
        qm_ref[...] = (q * jnp.exp(jnp.minimum(e1, EXP_CLAMP))).astype(BF16)
        km_ref[...] = (k * jnp.exp(jnp.minimum(-e1, EXP_CLAMP))).astype(BF16)
        qe_ref[...] = (q * jnp.exp(bc)).astype(BF16)
        k2 = (k * jnp.exp(tot - bc)).astype(BF16)
        for p in range(GLA_HEADS // 2):
            kp = k2[:, p * HEAD_PAIR:(p + 1) * HEAD_PAIR]
            ut = jnp.zeros((GLA_DV, HEAD_PAIR), F32)
            for hh in range(2):
                h = 2 * p + hh
                keep = (lane < GLA_DK) if hh == 0 else (lane >= GLA_DK)
                kh = jnp.where(keep, kp, jnp.zeros_like(kp))
                ut = ut + lax.dot_general(vb[:, h * GLA_DV:(h + 1) * GLA_DV], kh, TN_DIMS,
                                          preferred_element_type=F32)
            ut_ref[0, :, p * HEAD_PAIR:(p + 1) * HEAD_PAIR] = ut
        return jnp.exp(tot)

    df = direction(g[:, :QK_DIM], ci <= ri, half - 1, c - 1, qmf_ref, kmf_ref, qef_ref, utf_ref)
    db = direction(g[:, QK_DIM:], ci >= ri, half, 0, qmb_ref, kmb_ref, qeb_ref, utb_ref)
    dtot_ref[0] = jnp.concatenate([df, db, jnp.zeros((6, QK_DIM), F32)], axis=0)


def _gla_prep(qkv, dlow, wup, bdec):
    m = qkv.shape[0]
    c = ROW_BLOCK
    nb = m // c
    qk_spec = pl.BlockSpec((c, QK_DIM), lambda i: (i, 0))
    bf_qk = jax.ShapeDtypeStruct((m, QK_DIM), BF16)
    ut_shape = jax.ShapeDtypeStruct((nb, GLA_DV, QK_DIM), F32)
    ut_spec = pl.BlockSpec((1, GLA_DV, QK_DIM), lambda i: (i, 0, 0))
    return pl.pallas_call(
        _gla_prep_kernel,
        out_shape=(bf_qk,) * 6 + (ut_shape, ut_shape, jax.ShapeDtypeStruct((nb, 8, QK_DIM), F32)),
        grid=(nb,),
        in_specs=[pl.BlockSpec((c, QK_DIM), lambda i: (i, 0)),
                  pl.BlockSpec((c, QK_DIM), lambda i: (i, 1)),
                  pl.BlockSpec((c, V_DIM), lambda i: (i, 1)),
                  pl.BlockSpec((c, LANES), lambda i: (i, 0)),
                  pl.BlockSpec((LANES, 4 * QK_DIM), lambda i: (0, 0)),
                  pl.BlockSpec((1, 2 * QK_DIM), lambda i: (0, 0))],
        out_specs=(qk_spec,) * 6 + (ut_spec, ut_spec,
                                    pl.BlockSpec((1, 8, QK_DIM), lambda i: (i, 0, 0))),
        compiler_params=_params("parallel"), name="gla_prep",
    )(qkv, qkv, qkv, dlow, wup, bdec)


def _gla_scan_kernel(utf_ref, dtf_ref, utb_ref, dtb_ref, stf_ref, stb_ref, sf_ref, sb_ref):
    @pl.when(pl.program_id(0) == 0)
    def _():
        sf_ref[...] = jnp.zeros_like(sf_ref)
        sb_ref[...] = jnp.zeros_like(sb_ref)

    sf = sf_ref[...]
    stf_ref[0] = sf.astype(BF16)
    sf_ref[...] = dtf_ref[0, 0:1, :] * sf + utf_ref[0]
    sb = sb_ref[...]
    stb_ref[0] = sb.astype(BF16)
    sb_ref[...] = dtb_ref[0, 1:2, :] * sb + utb_ref[0]


def _gla_scan(utf, utb, dtot, ctx_blocks):
    nb = utf.shape[0]

    def bwd(t):
        return jnp.where(t < ctx_blocks, ctx_blocks - 1 - t, nb + ctx_blocks - 1 - t)

    st_shape = jax.ShapeDtypeStruct((nb, GLA_DV, QK_DIM), BF16)
    blk = (1, GLA_DV, QK_DIM)
    return pl.pallas_call(
        _gla_scan_kernel,
        out_shape=(st_shape, st_shape),
        grid=(nb,),
        in_specs=[pl.BlockSpec(blk, lambda t: (t, 0, 0)),
                  pl.BlockSpec((1, 8, QK_DIM), lambda t: (t, 0, 0)),
                  pl.BlockSpec(blk, lambda t: (bwd(t), 0, 0)),
                  pl.BlockSpec((1, 8, QK_DIM), lambda t: (bwd(t), 0, 0))],
        out_specs=(pl.BlockSpec(blk, lambda t: (t, 0, 0)),
                   pl.BlockSpec(blk, lambda t: (bwd(t), 0, 0))),
        scratch_shapes=[pltpu.VMEM((GLA_DV, QK_DIM), F32), pltpu.VMEM((GLA_DV, QK_DIM), F32)],
        compiler_params=_params("arbitrary"), name="gla_scan",
    )(utf, dtot, utb, dtot)


def _gla_out_kernel(qmf_ref, kmf_ref, qef_ref, qmb_ref, kmb_ref, qeb_ref, vb_ref,
                    stf_ref, stb_ref, r_ref, gain_ref, a_ref):
    c = qmf_ref.shape[0]
    ri = lax.broadcasted_iota(jnp.int32, (c, c), 0)
    ci = lax.broadcasted_iota(jnp.int32, (c, c), 1)
    lower = ci <= ri
    upper = ci >= ri
    lane = lax.broadcasted_iota(jnp.int32, (1, HEAD_PAIR), 1)
    for p in range(GLA_HEADS // 2):
        pair = slice(p * HEAD_PAIR, (p + 1) * HEAD_PAIR)
        qmf, kmf, qef = qmf_ref[:, pair], kmf_ref[:, pair], qef_ref[:, pair]
        qmb, kmb, qeb = qmb_ref[:, pair], kmb_ref[:, pair], qeb_ref[:, pair]
        stf = stf_ref[0, :, pair]
        stb = stb_ref[0, :, pair]
        for hh in range(2):
            h = 2 * p + hh
            keep = (lane < GLA_DK) if hh == 0 else (lane >= GLA_DK)
            zero = jnp.zeros_like(kmf)
            af = lax.dot_general(qmf, jnp.where(keep, kmf, zero), NT_DIMS, preferred_element_type=F32)
            ab = lax.dot_general(qmb, jnp.where(keep, kmb, zero), NT_DIMS, preferred_element_type=F32)
            att = (jnp.where(lower, af, 0.0) + jnp.where(upper, ab, 0.0)).astype(BF16)
            cols = slice(h * GLA_DV, (h + 1) * GLA_DV)
            o = jnp.dot(att, vb_ref[:, cols], preferred_element_type=F32)
            o = o + lax.dot_general(jnp.where(keep, qef, zero), stf, NT_DIMS, preferred_element_type=F32)
            o = o + lax.dot_general(jnp.where(keep, qeb, zero), stb, NT_DIMS, preferred_element_type=F32)
            o = o * lax.rsqrt(jnp.mean(o * o, axis=-1, keepdims=True) + EPS)
            r = r_ref[:, cols].astype(F32)
            a_ref[:, cols] = (o * gain_ref[:, cols] * (r * jax.nn.sigmoid(r))).astype(BF16)


def _gla_out(prep, qkv, stf, stb, p_ruv, gain):
    qmf, kmf, qef, qmb, kmb, qeb = prep
    m = qkv.shape[0]
    c = ROW_BLOCK
    qk_spec = pl.BlockSpec((c, QK_DIM), lambda i: (i, 0))
    st_spec = pl.BlockSpec((1, GLA_DV, QK_DIM), lambda i: (i, 0, 0))
    return pl.pallas_call(
        _gla_out_kernel,
        out_shape=jax.ShapeDtypeStruct((m, V_DIM), BF16),
        grid=(m // c,),
        in_specs=[qk_spec] * 6 + [pl.BlockSpec((c, V_DIM), lambda i: (i, 1)), st_spec, st_spec,
                                  pl.BlockSpec((c, V_DIM), lambda i: (i, RUV_COL_R)),
                                  pl.BlockSpec((1, V_DIM), lambda i: (0, 0))],
        out_specs=pl.BlockSpec((c, V_DIM), lambda i: (i, 0)),
        compiler_params=_params("parallel"), name="gla_out",
    )(qmf, kmf, qef, qmb, kmb, qeb, qkv, stf, stb, p_ruv, gain.reshape(1, V_DIM))


def _branches_kernel(*refs):
    _gla_out_kernel(*refs[:11], refs[17])
    _sgu_kernel(*refs[11:17], refs[18])


def _branch_outputs(prep, qkv, stf, stb, p_ruv, gain, ln_g, ln_b, w_s, layer, b_s_bcast):
    qmf, kmf, qef, qmb, kmb, qeb = prep
    m = qkv.shape[0]
    c = ROW_BLOCK
    qk_spec = pl.BlockSpec((c, QK_DIM), lambda i: (i, 0))
    st_spec = pl.BlockSpec((1, GLA_DV, QK_DIM), lambda i: (i, 0, 0))
    vec_spec = pl.BlockSpec((1, V_DIM), lambda i: (0, 0))
    out_spec = pl.BlockSpec((c, V_DIM), lambda i: (i, 0))
    out_shape = jax.ShapeDtypeStruct((m, V_DIM), BF16)
    assert SG_DIM == V_DIM
    return pl.pallas_call(
        _branches_kernel,
        out_shape=(out_shape, out_shape),
        grid=(m // c,),
        in_specs=[qk_spec] * 6 + [pl.BlockSpec((c, V_DIM), lambda i: (i, 1)), st_spec, st_spec,
                                  pl.BlockSpec((c, V_DIM), lambda i: (i, RUV_COL_R)), vec_spec,
                                  pl.BlockSpec((c, SG_DIM), lambda i: (i, RUV_COL_U)),
                                  pl.BlockSpec((c, SG_DIM), lambda i: (i, RUV_COL_VS)),
                                  vec_spec, vec_spec,
                                  pl.BlockSpec((None, SG_GROUPS, SG_CHUNK, SG_CHUNK),
                                               lambda i: (layer, 0, 0, 0)),
                                  pl.BlockSpec((SG_GROUPS, SG_CHUNK, SG_GROUP_CH), lambda i: (0, 0, 0))],
        out_specs=(out_spec, out_spec),
        compiler_params=_params("parallel"), name="branches",
    )(qmf, kmf, qef, qmb, kmb, qeb, qkv, stf, stb, p_ruv, gain.reshape(1, V_DIM),
      p_ruv, p_ruv, ln_g.reshape(1, SG_DIM), ln_b.reshape(1, SG_DIM), w_s, b_s_bcast)


ONE_F32_BITS = 0x3F800000
SEARCH_STEPS = 31


def _route_kernel(logit_ref, g_ref, *, ctx_rows):
    m = logit_ref.shape[0]
    rb = ROW_BLOCK
    lane = lax.broadcasted_iota(jnp.int32, (1, LANES), 1)
    valid = lane < N_EXPERTS

    def softmax_block(i, carry):
        r0 = pl.multiple_of(i * rb, rb)
        lg = jnp.where(valid, logit_ref[pl.ds(r0, rb), :], -jnp.inf)
        ex = jnp.exp(lg - jnp.max(lg, axis=-1, keepdims=True))
        g_ref[pl.ds(r0, rb), :] = ex / jnp.sum(ex, axis=-1, keepdims=True)
        return carry

    lax.fori_loop(0, m // rb, softmax_block, 0)

    def bits_of(r0):
        return lax.bitcast_convert_type(g_ref[pl.ds(r0, rb), :], jnp.int32)

    def count_ge(blk0, nblk, t):
        def body(i, acc):
            r0 = pl.multiple_of((blk0 + i) * rb, rb)
            hit = jnp.where(bits_of(r0) >= t, 1, 0)
            return acc + jnp.sum(hit.reshape(rb // 8, 8, LANES), axis=0)

        acc = lax.fori_loop(0, nblk, body, jnp.zeros((8, LANES), jnp.int32))
        return jnp.sum(acc, axis=0, keepdims=True)

    def route_set(blk0, nblk):
        cap = EC_CAPACITY * nblk * rb // N_EXPERTS

        def halve(_, lohi):
            lo, hi = lohi
            mid = lo + ((hi - lo) >> 1)
            ok = count_ge(blk0, nblk, mid) >= cap
            return jnp.where(ok, mid, lo), jnp.where(ok, hi, mid)

        thr, _ = lax.fori_loop(0, SEARCH_STEPS, halve,
                               (jnp.zeros((1, LANES), jnp.int32),
                                jnp.full((1, LANES), ONE_F32_BITS + 1, jnp.int32)))
        need = (cap - count_ge(blk0, nblk, thr + 1)).astype(F32)
        ri = lax.broadcasted_iota(jnp.int32, (rb, rb), 0)
        ci = lax.broadcasted_iota(jnp.int32, (rb, rb), 1)
        earlier = jnp.where(ci < ri, 1.0, 0.0).astype(BF16)

        def finalize(i, seen):
            r0 = pl.multiple_of((blk0 + i) * rb, rb)
            aff = g_ref[pl.ds(r0, rb), :]
            bits = lax.bitcast_convert_type(aff, jnp.int32)
            tie = bits == thr
            tie_f = jnp.where(tie, 1.0, 0.0)
            rank = seen + jnp.dot(earlier, tie_f.astype(BF16), preferred_element_type=F32)
            g_ref[pl.ds(r0, rb), :] = jnp.where(tie, jnp.where(rank < need, aff, 0.0),
                                                jnp.where(bits > thr, aff, 0.0))
            return seen + jnp.sum(tie_f, axis=0, keepdims=True)

        lax.fori_loop(0, nblk, finalize, jnp.zeros((1, LANES), F32))

    route_set(0, ctx_rows // rb)
    route_set(ctx_rows // rb, (m - ctx_rows) // rb)


def _route(logits, ctx_rows):
    m = logits.shape[0]
    return pl.pallas_call(
        functools.partial(_route_kernel, ctx_rows=ctx_rows),
        out_shape=jax.ShapeDtypeStruct((m, LANES), F32),
        grid=(1,),
        in_specs=[pl.BlockSpec((m, LANES), lambda i: (0, 0))],
        out_specs=pl.BlockSpec((m, LANES), lambda i: (0, 0)),
        compiler_params=_params("arbitrary"), name="route",
    )(logits)


SLOTS_PER_BLOCK = 64
GATE_PARTS = 3


def _slot_ranks(g):
    rb = g.shape[0]
    picked = jnp.where(g > 0.0, 1.0, 0.0)
    ri = lax.broadcasted_iota(jnp.int32, (rb, rb), 0)
    ci = lax.broadcasted_iota(jnp.int32, (rb, rb), 1)
    earlier = jnp.where(ci < ri, 1.0, 0.0).astype(BF16)
    rank = jnp.dot(earlier, picked.astype(BF16), preferred_element_type=F32)
    return jnp.where(picked > 0.0, rank, -1.0), jnp.sum(picked, axis=0, keepdims=True)


def _slots_kernel(g_ref, pos_ref):
    pos_ref[...] = _slot_ranks(g_ref[...])[0]


def _slots(gates):
    m = gates.shape[0]
    rb = ROW_BLOCK
    return pl.pallas_call(
        _slots_kernel,
        out_shape=jax.ShapeDtypeStruct((m, LANES), F32),
        grid=(m // rb,),
        in_specs=[pl.BlockSpec((rb, LANES), lambda i: (i, 0))],
        out_specs=pl.BlockSpec((rb, LANES), lambda i: (i, 0)),
        compiler_params=_params("parallel"), name="moe_slots",
    )(gates)


def _slot_onehot(post):
    rb = post.shape[1]
    slot = lax.broadcasted_iota(jnp.int32, (SLOTS_PER_BLOCK, rb), 0).astype(F32)
    parts = [jnp.where(slot == post[e:e + 1, :], 1.0, 0.0) for e in range(N_EXPERTS)]
    return jnp.concatenate(parts, axis=0).astype(BF16)


def _dispatch_kernel(h_ref, g_ref, xs_ref, gs_ref, post_ref, cnt_ref):
    pos, cnt = _slot_ranks(g_ref[...])
    post = pos.T[:N_EXPERTS, :]
    post_ref[...] = post
    cnt_ref[0] = jnp.broadcast_to(cnt, (8, LANES))
    onehot = _slot_onehot(post)
    d = h_ref.shape[1]
    xs = jnp.dot(onehot, h_ref[...], preferred_element_type=F32)
    xs_ref[...] = xs.astype(BF16).reshape(N_EXPERTS, SLOTS_PER_BLOCK, d)
    g = g_ref[...]
    pieces = jnp.zeros_like(g)
    for k in range(GATE_PARTS):
        piece = g.astype(BF16).astype(F32)
        pieces = pieces + (pltpu.roll(piece, k * N_EXPERTS, axis=1) if k else piece)
        g = g - piece
    gs = jnp.dot(onehot, pieces.astype(BF16), preferred_element_type=F32)
    gs_ref[...] = gs.reshape(N_EXPERTS, SLOTS_PER_BLOCK, LANES)


def _dispatch(h, gates):
    m, d = h.shape
    rb = ROW_BLOCK
    nb = m // rb
    s = SLOTS_PER_BLOCK
    return pl.pallas_call(
        _dispatch_kernel,
        out_shape=(jax.ShapeDtypeStruct((N_EXPERTS, nb * s, d), BF16),
                   jax.ShapeDtypeStruct((N_EXPERTS, nb * s, LANES), F32),
                   jax.ShapeDtypeStruct((N_EXPERTS, m), F32),
                   jax.ShapeDtypeStruct((nb, 8, LANES), F32)),
        grid=(nb,),
        in_specs=[pl.BlockSpec((rb, d), lambda i: (i, 0)),
                  pl.BlockSpec((rb, LANES), lambda i: (i, 0))],
        out_specs=(pl.BlockSpec((N_EXPERTS, s, d), lambda i: (0, i, 0)),
                   pl.BlockSpec((N_EXPERTS, s, LANES), lambda i: (0, i, 0)),
                   pl.BlockSpec((N_EXPERTS, rb), lambda i: (0, i)),
                   pl.BlockSpec((1, 8, LANES), lambda i: (i, 0, 0))),
        compiler_params=_params("parallel"), name="moe_dispatch",
    )(h, gates)


def _expert_kernel(xs_ref, w1_ref, w3_ref, w2_ref, gs_ref, y_ref):
    e = pl.program_id(0)
    lane = lax.broadcasted_iota(jnp.int32, (1, LANES), 1)
    mine = (lane % N_EXPERTS == e) & (lane < GATE_PARTS * N_EXPERTS)
    gate = jnp.sum(jnp.where(mine, gs_ref[0], 0.0), axis=-1, keepdims=True)
    x = xs_ref[0]
    h1 = jnp.dot(x, w1_ref[0], preferred_element_type=F32)
    h3 = jnp.dot(x, w3_ref[0], preferred_element_type=F32)
    hid = (h1 * jax.nn.sigmoid(h1) * h3 * gate).astype(BF16)
    y_ref[0] = jnp.dot(hid, w2_ref[0], preferred_element_type=F32).astype(BF16)


def _expert_ffn(xs, gs, w1, w3, w2, layer):
    e, rows, d = xs.shape
    f = w1.shape[-1]
    tr = _pick(rows, (832, 1024, 512, 256, 128, 64, 16))
    return pl.pallas_call(
        _expert_kernel,
        out_shape=jax.ShapeDtypeStruct((e, rows, d), BF16),
        grid=(e, rows // tr),
        in_specs=[pl.BlockSpec((1, tr, d), lambda i, j: (i, j, 0)),
                  pl.BlockSpec((None, 1, d, f), lambda i, j: (layer, i, 0, 0)),
                  pl.BlockSpec((None, 1, d, f), lambda i, j: (layer, i, 0, 0)),
                  pl.BlockSpec((None, 1, f, d), lambda i, j: (layer, i, 0, 0)),
                  pl.BlockSpec((1, tr, LANES), lambda i, j: (i, j, 0))],
        out_specs=pl.BlockSpec((1, tr, d), lambda i, j: (i, j, 0)),
        compiler_params=_params("parallel", "arbitrary"), name="expert_ffn",
    )(xs, w1, w3, w2, gs)


def _combined_rows(ys_ref, post_ref, x_ref, mod_ref, gate_row):
    onehot = _slot_onehot(post_ref[...])
    d = x_ref.shape[1]
    ys = ys_ref[...].reshape(N_EXPERTS * SLOTS_PER_BLOCK, d)
    moe = lax.dot_general(onehot, ys, TN_DIMS, preferred_element_type=F32)
    return x_ref[...] + mod_ref[0, gate_row:gate_row + 1, :] * moe


def _combine_prenorm_kernel(ys_ref, post_ref, x_ref, mod_ref, g_ref, nmod_ref, o_ref, h_ref, *,
                            gate_row):
    x = _combined_rows(ys_ref, post_ref, x_ref, mod_ref, gate_row)
    o_ref[...] = x
    h_ref[...] = _modulated_norm(x, g_ref[...], nmod_ref[0], 0, 1).astype(BF16)


def _combine_final_kernel(ys_ref, post_ref, x_ref, mod_ref, g_ref, o_ref, *, gate_row):
    x = _combined_rows(ys_ref, post_ref, x_ref, mod_ref, gate_row)
    ms = jnp.mean(x * x, axis=-1, keepdims=True)
    o_ref[...] = x * lax.rsqrt(ms + EPS) * g_ref[...]


def _combine(ys, post, x, mods, ctx_rows, gate_row, norm_g, next_mods=None):
    m, d = x.shape
    rb = ROW_BLOCK
    s = SLOTS_PER_BLOCK
    stream = _stream_of_block(rb, ctx_rows)
    row_spec = pl.BlockSpec((rb, d), lambda i: (i, 0))
    mod_spec = pl.BlockSpec((1, 8, d), lambda i: (stream(i), 0, 0))
    in_specs = [pl.BlockSpec((N_EXPERTS, s, d), lambda i: (0, i, 0)),
                pl.BlockSpec((N_EXPERTS, rb), lambda i: (0, i)),
                row_spec, mod_spec, pl.BlockSpec((1, d), lambda i: (0, 0))]
    if next_mods is not None:
        return pl.pallas_call(
            functools.partial(_combine_prenorm_kernel, gate_row=gate_row),
            out_shape=(jax.ShapeDtypeStruct((m, d), F32), jax.ShapeDtypeStruct((m, d), BF16)),
            grid=(m // rb,),
            in_specs=in_specs + [mod_spec],
            out_specs=(row_spec, row_spec),
            compiler_params=_params("parallel"), name="moe_combine",
        )(ys, post, x, mods, norm_g.reshape(1, d), next_mods)
    skip = ctx_rows // rb
    return pl.pallas_call(
        functools.partial(_combine_final_kernel, gate_row=gate_row),
        out_shape=jax.ShapeDtypeStruct((m - ctx_rows, d), F32),
        grid=(m // rb,),
        in_specs=in_specs,
        out_specs=pl.BlockSpec((rb, d), lambda i: (jnp.maximum(i - skip, 0), 0)),
        compiler_params=_params("arbitrary"), name="moe_combine_final",
    )(ys, post, x, mods, norm_g.reshape(1, d))


def _moe_hidden_kernel(h_ref, w1_ref, w3_ref, g_ref, o_ref):
    e = pl.program_id(1)
    lane = lax.broadcasted_iota(jnp.int32, (1, LANES), 1)
    gate = jnp.sum(jnp.where(lane == e, g_ref[...], 0.0), axis=-1, keepdims=True)
    h = h_ref[...]
    h1 = jnp.dot(h, w1_ref[0].astype(BF16), preferred_element_type=F32)
    h3 = jnp.dot(h, w3_ref[0].astype(BF16), preferred_element_type=F32)
    o_ref[...] = (h1 * jax.nn.sigmoid(h1) * h3 * gate).astype(BF16)


def _moe_hidden(h, gates, w1, w3, layer):
    m, d = h.shape
    _, e, _, f = w1.shape
    tm = _pick(m, (1280, 1024, 768, 512, 256))
    return pl.pallas_call(
        _moe_hidden_kernel,
        out_shape=jax.ShapeDtypeStruct((m, e * f), BF16),
        grid=(m // tm, e),
        in_specs=[pl.BlockSpec((tm, d), lambda i, j: (i, 0)),
                  pl.BlockSpec((None, 1, d, f), lambda i, j: (layer, j, 0, 0)),
                  pl.BlockSpec((None, 1, d, f), lambda i, j: (layer, j, 0, 0)),
                  pl.BlockSpec((tm, LANES), lambda i, j: (i, 0))],
        out_specs=pl.BlockSpec((tm, f), lambda i, j: (i, j)),
        compiler_params=_params("parallel", "arbitrary"), name="moe_hidden",
    )(h, w1, w3, gates)


def kernel(x, c, ctx, c_ctx, ada_w, ada_b, norm1_g, norm2_g, w_in, w_dec_up, b_dec, gla_norm_g,
           sg_ln_g, sg_ln_b, sg_w, sg_b, w_branch_a, w_branch_b, w_out, w_router, w_exp1, w_exp3,
           w_exp2, final_g):
    batch, seq, d = x.shape
    ctx_rows = ctx.shape[1]
    depth = ada_w.shape[0]
    assert batch == 1 and seq % ROW_BLOCK == 0 and ctx_rows % ROW_BLOCK == 0

    cond2 = jnp.stack([c_ctx, c[0]], axis=1)
    mods_all = _adaln(cond2, ada_w, ada_b).reshape(depth, 2, 6, d)
    mods_all = jnp.pad(mods_all, ((0, 0), (0, 0), (0, 2), (0, 0)))

    w1_all, w3_all, w2_all = (w.astype(BF16) for w in (w_exp1, w_exp3, w_exp2))
    w2_flat = w2_all.reshape(depth, N_EXPERTS * EXPERT_FF, d)
    sg_w_all = sg_w.astype(BF16)
    n_in = w_in.shape[2]
    w_in_rows = jnp.swapaxes(w_in, 1, 2).reshape(depth * n_in, d)

    for l in range(depth):
        mods = mods_all[l]
        w_head_t = _cast_rows(w_in_rows, l * n_in, OFF_DF + LANES)
        w_rest_t = _cast_rows(w_in_rows, l * n_in + SCAN_COLS, n_in - SCAN_COLS)
        wup = jnp.zeros((LANES, 2 * QK_DIM), F32)
        wup = wup.at[:DECAY_RANK, :QK_DIM].set(w_dec_up[l, 0])
        wup = wup.at[DECAY_RANK:2 * DECAY_RANK, QK_DIM:].set(w_dec_up[l, 1])
        bdec = b_dec[l].reshape(1, 2 * QK_DIM)
        b_s_bcast = jnp.broadcast_to(sg_b[l][:, :, None], (SG_GROUPS, SG_CHUNK, SG_GROUP_CH))
        w_r = jnp.pad(w_router[l], ((0, 0), (0, LANES - N_EXPERTS)))

        if l == 0:
            xa, h = _stack_prenorm(ctx[0], x[0], norm1_g[l], mods)
        p_qkv, dlow = _matmul_nt_with_side(h, w_head_t, OFF_DF, BF16)
        p_ruv = _matmul_nt(h, w_rest_t, 3 * V_DIM, BF16)

        prep = _gla_prep(p_qkv, dlow, jnp.concatenate(_split_bf16(wup), axis=1), bdec)
        stf, stb = _gla_scan(prep[6], prep[7], prep[8], ctx_rows // ROW_BLOCK)
        a, s = _branch_outputs(prep[:6], p_qkv, stf, stb, p_ruv, gla_norm_g[l],
                               sg_ln_g[l], sg_ln_b[l], sg_w_all, l, b_s_bcast)
        mix = _merge(h, a, s, w_rest_t, 3 * V_DIM, w_branch_a, w_branch_b, l)
        xa = _matmul_gated_residual(mix, w_out, xa, mods, ctx_rows, gate_row=2, layer=l)

        h2, logits = _prenorm_router(xa, norm2_g[l], mods, ctx_rows, 3, 4, w_r)
        gates = _route(logits, ctx_rows)
        xs, gs, post, cnt = _dispatch(h2, gates)
        ys = _expert_ffn(xs, gs, w1_all, w3_all, w2_all, l)

        def combine(xa, l=l, ys=ys, post=post, mods=mods):
            if l + 1 < depth:
                return _combine(ys, post, xa, mods, ctx_rows, 5, norm1_g[l + 1], mods_all[l + 1])
            return _combine(ys, post, xa, mods, ctx_rows, 5, final_g)

        def combine_after_overflow(xa, l=l, h2=h2, gates=gates, mods=mods, combine=combine):
            late = jnp.where(_slots(gates) >= SLOTS_PER_BLOCK, gates, 0.0)
            hid = _moe_hidden(h2, late, w1_all, w3_all, l)
            return combine(_matmul_gated_residual(hid, w2_flat, xa, mods, ctx_rows, gate_row=5, layer=l))

        res = lax.cond(jnp.max(cnt) > SLOTS_PER_BLOCK, combine_after_overflow, combine, xa)
        if l + 1 < depth:
            xa, h = res
        else:
            out = res

    return out[None]
```

```python
import functools

import jax
import jax.numpy as jnp
from jax import lax
from jax.experimental import pallas as pl
from jax.experimental.pallas import tpu as pltpu

EPS = 1e-6
GLA_HEADS = 8
GLA_DK = 64
GLA_DV = 128
DECAY_RANK = 16
GLA_TAU = 16.0
SG_GROUPS = 8
SG_GROUP_CH = 128
SG_CHUNK = 128
N_EXPERTS = 16
EXPERT_FF = 256
EC_CAPACITY = 2

QK_DIM = GLA_HEADS * GLA_DK
V_DIM = GLA_HEADS * GLA_DV
SG_DIM = SG_GROUPS * SG_GROUP_CH
OFF_K = QK_DIM
OFF_V = 2 * QK_DIM
OFF_DF = OFF_V + V_DIM
OFF_DB = OFF_DF + DECAY_RANK
SCAN_COLS = OFF_DB + DECAY_RANK
OFF_U = SCAN_COLS + V_DIM
OFF_VS = OFF_U + SG_DIM
OFF_GA = OFF_VS + SG_DIM

RUV_COL_R, RUV_COL_U, RUV_COL_VS = 0, 1, 2

LANES = 128
ROW_BLOCK = 256
HEAD_PAIR = 2 * GLA_DK
EXP_CLAMP = 80.0
VMEM_LIMIT_BYTES = 52 * 1024 * 1024

F32 = jnp.float32
BF16 = jnp.bfloat16
NT_DIMS = (((1,), (1,)), ((), ()))
TN_DIMS = (((0,), (0,)), ((), ()))


def _pick(n, candidates):
    for c in candidates:
        if n % c == 0:
            return c
    raise ValueError(f"no tile of {candidates} divides {n}")


def _params(*semantics):
    return pltpu.CompilerParams(dimension_semantics=semantics,
                                vmem_limit_bytes=VMEM_LIMIT_BYTES)


def _stream_of_block(tm, ctx_rows):
    return lambda i: (i * tm >= ctx_rows).astype(jnp.int32)


def _adaln_kernel(cond_ref, w_ref, b_ref, out_ref, *, rows):
    d = cond_ref.shape[0]
    tn = out_ref.shape[-1]

    def body(i, acc):
        r0 = pl.multiple_of(i * rows, rows)
        cond = cond_ref[pl.ds(r0, rows), :]
        s = cond * jax.nn.sigmoid(cond)
        w = w_ref[0, pl.ds(r0, rows), :]
        a0 = jnp.sum(w * s[:, 0:1], axis=0, keepdims=True)
        a1 = jnp.sum(w * s[:, 1:2], axis=0, keepdims=True)
        return acc + jnp.concatenate([a0, a1], axis=0)

    acc = lax.fori_loop(0, d // rows, body, jnp.zeros((2, tn), F32))
    out_ref[0] = acc + b_ref[0]


def _adaln(cond2, ada_w, ada_b):
    nl, d, n6 = ada_w.shape
    tn = _pick(n6, (1024, 512, 256, 128))
    return pl.pallas_call(
        functools.partial(_adaln_kernel, rows=_pick(d, (256, 128, 8))),
        out_shape=jax.ShapeDtypeStruct((nl, 2, n6), F32),
        grid=(nl, n6 // tn),
        in_specs=[pl.BlockSpec((d, 2), lambda l, j: (0, 0)),
                  pl.BlockSpec((1, d, tn), lambda l, j: (l, 0, j)),
                  pl.BlockSpec((1, 1, tn), lambda l, j: (l, 0, j))],
        out_specs=pl.BlockSpec((1, 2, tn), lambda l, j: (l, 0, j)),
        compiler_params=_params("parallel", "parallel"),
        name="adaln",
    )(cond2, ada_w, ada_b.reshape(nl, 1, n6))


def _modulated_norm(x, g, mod, shift_row, scale_row):
    ms = jnp.mean(x * x, axis=-1, keepdims=True)
    y = x * lax.rsqrt(ms + EPS) * g
    return y * (1.0 + mod[scale_row:scale_row + 1, :]) + mod[shift_row:shift_row + 1, :]


def _stack_prenorm_kernel(ctx_ref, x_ref, g_ref, mod_ref, xa_ref, h_ref, *, ctx_blocks):
    xin = jnp.where(pl.program_id(0) < ctx_blocks, ctx_ref[...], x_ref[...])
    xa_ref[...] = xin
    h_ref[...] = _modulated_norm(xin, g_ref[...], mod_ref[0], 0, 1).astype(BF16)


def _stack_prenorm(ctx2d, x2d, g, mods):
    ctx_rows, d = ctx2d.shape
    m = ctx_rows + x2d.shape[0]
    tm = ROW_BLOCK
    nc = ctx_rows // tm
    row_spec = pl.BlockSpec((tm, d), lambda i: (i, 0))
    return pl.pallas_call(
        functools.partial(_stack_prenorm_kernel, ctx_blocks=nc),
        out_shape=(jax.ShapeDtypeStruct((m, d), F32), jax.ShapeDtypeStruct((m, d), BF16)),
        grid=(m // tm,),
        in_specs=[pl.BlockSpec((tm, d), lambda i: (jnp.minimum(i, nc - 1), 0)),
                  pl.BlockSpec((tm, d), lambda i: (jnp.maximum(i - nc, 0), 0)),
                  pl.BlockSpec((1, d), lambda i: (0, 0)),
                  pl.BlockSpec((1, 8, d), lambda i: (_stream_of_block(tm, ctx_rows)(i), 0, 0))],
        out_specs=(row_spec, row_spec),
        compiler_params=_params("parallel"), name="stack_prenorm",
    )(ctx2d, x2d, g.reshape(1, d), mods)


def _prenorm_router_kernel(x_ref, g_ref, mod_ref, wr_ref, h_ref, logit_ref, *, shift_row, scale_row):
    h = _modulated_norm(x_ref[...], g_ref[...], mod_ref[0], shift_row, scale_row)
    h_hi, h_lo = _split_bf16(h)
    h_ref[...] = h_hi
    hi = jnp.dot(h_hi, wr_ref[...], preferred_element_type=F32)
    lo = jnp.dot(h_lo, wr_ref[:, :LANES], preferred_element_type=F32)
    logit_ref[...] = hi[:, :LANES] + hi[:, LANES:] + lo


def _prenorm_router(x, g, mods, ctx_rows, shift_row, scale_row, w_router):
    m, d = x.shape
    tm = ROW_BLOCK
    stream = _stream_of_block(tm, ctx_rows)
    row_spec = pl.BlockSpec((tm, d), lambda i: (i, 0))
    return pl.pallas_call(
        functools.partial(_prenorm_router_kernel, shift_row=shift_row, scale_row=scale_row),
        out_shape=(jax.ShapeDtypeStruct((m, d), BF16), jax.ShapeDtypeStruct((m, LANES), F32)),
        grid=(m // tm,),
        in_specs=[row_spec,
                  pl.BlockSpec((1, d), lambda i: (0, 0)),
                  pl.BlockSpec((1, 8, d), lambda i: (stream(i), 0, 0)),
                  pl.BlockSpec((d, 2 * LANES), lambda i: (0, 0))],
        out_specs=(row_spec, pl.BlockSpec((tm, LANES), lambda i: (i, 0))),
        compiler_params=_params("parallel"), name="prenorm_router",
    )(x, g.reshape(1, d), mods, jnp.concatenate(_split_bf16(w_router), axis=1))


def _cast_kernel(w_ref, o_ref):
    o_ref[...] = w_ref[...].astype(BF16)


def _cast_rows(w_rows, row0, n_rows):
    k = w_rows.shape[1]
    tr = _pick(n_rows, (512, 544, 256, 128))
    assert row0 % 8 == 0
    return pl.pallas_call(
        _cast_kernel,
        out_shape=jax.ShapeDtypeStruct((n_rows, k), BF16),
        grid=(n_rows // tr,),
        in_specs=[pl.BlockSpec((pl.Element(tr), pl.Element(k)),
                               lambda j: (pl.multiple_of(row0 + j * tr, 8), 0))],
        out_specs=pl.BlockSpec((tr, k), lambda j: (j, 0)),
        compiler_params=_params("parallel"), name="cast_rows",
    )(w_rows)


def _weight_spec(w, layer, rows, tn, col_block):
    if w.ndim == 2:
        return pl.BlockSpec((rows, tn), lambda i, j: (0, col_block(j)))
    return pl.BlockSpec((None, rows, tn), lambda i, j: (layer, 0, col_block(j)))


def _mm_nt_kernel(a_ref, wt_ref, o_ref):
    o_ref[...] = lax.dot_general(a_ref[...], wt_ref[...], NT_DIMS,
                                 preferred_element_type=F32).astype(o_ref.dtype)


def _matmul_nt(a, w_t, n_rows, out_dtype):
    m, k = a.shape
    n = n_rows
    tm = _pick(m, (1280, 1024, 768, 512, 256))
    tn = _pick(n, (512, 256, 128))
    return pl.pallas_call(
        _mm_nt_kernel,
        out_shape=jax.ShapeDtypeStruct((m, n), out_dtype),
        grid=(m // tm, n // tn),
        in_specs=[pl.BlockSpec((tm, k), lambda i, j: (i, 0)),
                  pl.BlockSpec((tn, k), lambda i, j: (j, 0))],
        out_specs=pl.BlockSpec((tm, tn), lambda i, j: (i, j)),
        compiler_params=_params("parallel", "arbitrary"), name="matmul",
    )(a, w_t)


def _mm_nt_side_kernel(a_ref, wt_ref, ws_ref, o_ref, side_ref):
    a = a_ref[...]
    o_ref[...] = lax.dot_general(a, wt_ref[...], NT_DIMS, preferred_element_type=F32).astype(o_ref.dtype)

    @pl.when(pl.program_id(1) == 0)
    def _():
        side_ref[...] = lax.dot_general(a, ws_ref[...], NT_DIMS, preferred_element_type=F32)


def _matmul_nt_with_side(a, w_t, n_rows, out_dtype):
    m, k = a.shape
    tm = _pick(m, (1280, 1024, 768, 512, 256))
    tn = _pick(n_rows, (512, 256, 128))
    assert w_t.shape[0] - n_rows == LANES and n_rows % LANES == 0
    return pl.pallas_call(
        _mm_nt_side_kernel,
        out_shape=(jax.ShapeDtypeStruct((m, n_rows), out_dtype), jax.ShapeDtypeStruct((m, LANES), F32)),
        grid=(m // tm, n_rows // tn),
        in_specs=[pl.BlockSpec((tm, k), lambda i, j: (i, 0)),
                  pl.BlockSpec((tn, k), lambda i, j: (j, 0)),
                  pl.BlockSpec((LANES, k), lambda i, j: (n_rows // LANES, 0))],
        out_specs=(pl.BlockSpec((tm, tn), lambda i, j: (i, j)),
                   pl.BlockSpec((tm, LANES), lambda i, j: (i, 0))),
        compiler_params=_params("parallel", "arbitrary"), name="matmul_side",
    )(a, w_t, w_t)


def _mm_resid_kernel(a_ref, w_ref, x_ref, mod_ref, o_ref, *, gate_row, ctx_rows):
    tm = a_ref.shape[0]
    acc = jnp.dot(a_ref[...], w_ref[...].astype(BF16), preferred_element_type=F32)
    row = pl.program_id(0) * tm + lax.broadcasted_iota(jnp.int32, (tm, 1), 0)
    gate = jnp.where(row < ctx_rows, mod_ref[0, gate_row:gate_row + 1, :],
                     mod_ref[1, gate_row:gate_row + 1, :])
    o_ref[...] = x_ref[...] + gate * acc


def _matmul_gated_residual(a, w, x, mods, ctx_rows, gate_row, layer=None):
    m, k = a.shape
    n = w.shape[-1]
    tm = _pick(m, (1280, 1024, 768, 512, 256))
    tn = _pick(n, (512, 256, 128))
    return pl.pallas_call(
        functools.partial(_mm_resid_kernel, gate_row=gate_row, ctx_rows=ctx_rows),
        out_shape=jax.ShapeDtypeStruct((m, n), F32),
        grid=(m // tm, n // tn),
        in_specs=[pl.BlockSpec((tm, k), lambda i, j: (i, 0)),
                  _weight_spec(w, layer, k, tn, lambda j: j),
                  pl.BlockSpec((tm, tn), lambda i, j: (i, j)),
                  pl.BlockSpec((2, 8, tn), lambda i, j: (0, 0, j))],
        out_specs=pl.BlockSpec((tm, tn), lambda i, j: (i, j)),
        compiler_params=_params("parallel", "arbitrary"), name="out_proj",
    )(a, w, x, mods)


def _merge_kernel(h_ref, a_ref, s_ref, wga_ref, wgb_ref, wa_ref, wb_ref, m_ref):
    h = h_ref[...]
    ga = lax.dot_general(h, wga_ref[...], NT_DIMS, preferred_element_type=F32)
    gb = lax.dot_general(h, wgb_ref[...], NT_DIMS, preferred_element_type=F32)
    ya = jnp.dot(a_ref[...], wa_ref[...].astype(BF16), preferred_element_type=F32)
    yb = jnp.dot(s_ref[...], wb_ref[...].astype(BF16), preferred_element_type=F32)
    m_ref[...] = (jax.nn.sigmoid(ga) * ya + jax.nn.sigmoid(gb) * yb).astype(BF16)


def _merge(h, a, s, w_gates_t, row_ga, w_a, w_b, layer):
    m, d = h.shape
    ka = a.shape[1]
    kb = s.shape[1]
    tm = _pick(m, (1280, 1024, 768, 512, 256))
    tn = _pick(d, (256, 128))
    nb = d // tn
    j0 = row_ga // tn
    return pl.pallas_call(
        _merge_kernel,
        out_shape=jax.ShapeDtypeStruct((m, d), BF16),
        grid=(m // tm, nb),
        in_specs=[pl.BlockSpec((tm, d), lambda i, j: (i, 0)),
                  pl.BlockSpec((tm, ka), lambda i, j: (i, 0)),
                  pl.BlockSpec((tm, kb), lambda i, j: (i, 0)),
                  pl.BlockSpec((tn, d), lambda i, j: (j0 + j, 0)),
                  pl.BlockSpec((tn, d), lambda i, j: (j0 + nb + j, 0)),
                  _weight_spec(w_a, layer, ka, tn, lambda j: j),
                  _weight_spec(w_b, layer, kb, tn, lambda j: j)],
        out_specs=pl.BlockSpec((tm, tn), lambda i, j: (i, j)),
        compiler_params=_params("parallel", "arbitrary"), name="merge",
    )(h, a, s, w_gates_t, w_gates_t, w_a, w_b)


def _sgu_kernel(u_ref, vs_ref, lng_ref, lnb_ref, ws_ref, bs_ref, s_ref):
    tm = u_ref.shape[0]
    u = jax.nn.gelu(u_ref[...].astype(F32))
    v = jax.nn.gelu(vs_ref[...].astype(F32))
    vc = v - jnp.mean(v, axis=-1, keepdims=True)
    vn = vc * lax.rsqrt(jnp.mean(vc * vc, axis=-1, keepdims=True) + EPS)
    vn = (vn * lng_ref[...] + lnb_ref[...]).astype(BF16)
    for ch in range(tm // SG_CHUNK):
        rows = slice(ch * SG_CHUNK, (ch + 1) * SG_CHUNK)
        for g in range(SG_GROUPS):
            cols = slice(g * SG_GROUP_CH, (g + 1) * SG_GROUP_CH)
            mixed = jnp.dot(ws_ref[g], vn[rows, cols], preferred_element_type=F32) + bs_ref[g]
            s_ref[rows, cols] = (u[rows, cols] * mixed).astype(BF16)


def _split_bf16(x):
    hi = x.astype(BF16)
    lo = (x - hi.astype(F32)).astype(BF16)
    return hi, lo


def _gla_prep_kernel(q_ref, k_ref, v_ref, dl_ref, wup_ref, bdec_ref,
                     qmf_ref, kmf_ref, qef_ref, qmb_ref, kmb_ref, qeb_ref,
                     utf_ref, utb_ref, dtot_ref):
    c = q_ref.shape[0]
    half = c // 2
    n2 = 2 * QK_DIM
    dl_hi, dl_lo = _split_bf16(dl_ref[...])
    zz = jnp.dot(dl_hi, wup_ref[...], preferred_element_type=F32)
    z = (zz[:, :n2] + zz[:, n2:] + jnp.dot(dl_lo, wup_ref[:, :n2], preferred_element_type=F32)
         + bdec_ref[...])
    g = -(jnp.maximum(-z, 0.0) + jnp.log1p(jnp.exp(-jnp.abs(z)))) * (1.0 / GLA_TAU)
    ri = lax.broadcasted_iota(jnp.int32, (c, c), 0)
    ci = lax.broadcasted_iota(jnp.int32, (c, c), 1)
    lane = lax.broadcasted_iota(jnp.int32, (1, HEAD_PAIR), 1)
    q = q_ref[...].astype(F32) * (GLA_DK ** -0.5)
    k = k_ref[...].astype(F32)
    vb = v_ref[...]

    def direction(gd, tri, mid_row, end_row, qm_ref, km_ref, qe_ref, ut_ref):
        hi, lo = _split_bf16(gd)
        tri = tri.astype(BF16)
        bc = (jnp.dot(tri, hi, preferred_element_type=F32)
              + jnp.dot(tri, lo, preferred_element_type=F32))
        rho = bc[mid_row:mid_row + 1, :]
        tot = bc[end_row:end_row + 1, :]
        e1 = bc - rho
        qm_ref[...] = (q * jnp.exp(jnp.minimum(e1, EXP_CLAMP))).astype(BF16)
        km_ref[...] = (k * jnp.exp(jnp.minimum(-e1, EXP_CLAMP))).astype(BF16)
        qe_ref[...] = (q * jnp.exp(bc)).astype(BF16)
        k2 = (k * jnp.exp(tot - bc)).astype(BF16)
        for p in range(GLA_HEADS // 2):
            kp = k2[:, p * HEAD_PAIR:(p + 1) * HEAD_PAIR]
            ut = jnp.zeros((GLA_DV, HEAD_PAIR), F32)
            for hh in range(2):
                h = 2 * p + hh
                keep = (lane < GLA_DK) if hh == 0 else (lane >= GLA_DK)
                kh = jnp.where(keep, kp, jnp.zeros_like(kp))
                ut = ut + lax.dot_general(vb[:, h * GLA_DV:(h + 1) * GLA_DV], kh, TN_DIMS,
                                          preferred_element_type=F32)
            ut_ref[0, :, p * HEAD_PAIR:(p + 1) * HEAD_PAIR] = ut
        return jnp.exp(tot)

    df = direction(g[:, :QK_DIM], ci <= ri, half - 1, c - 1, qmf_ref, kmf_ref, qef_ref, utf_ref)
    db = direction(g[:, QK_DIM:], ci >= ri, half, 0, qmb_ref, kmb_ref, qeb_ref, utb_ref)
    dtot_ref[0] = jnp.concatenate([df, db, jnp.zeros((6, QK_DIM), F32)], axis=0)


def _gla_prep(qkv, dlow, wup, bdec):
    m = qkv.shape[0]
    c = ROW_BLOCK
    nb = m // c
    qk_spec = pl.BlockSpec((c, QK_DIM), lambda i: (i, 0))
    bf_qk = jax.ShapeDtypeStruct((m, QK_DIM), BF16)
    ut_shape = jax.ShapeDtypeStruct((nb, GLA_DV, QK_DIM), F32)
    ut_spec = pl.BlockSpec((1, GLA_DV, QK_DIM), lambda i: (i, 0, 0))
    return pl.pallas_call(
        _gla_prep_kernel,
        out_shape=(bf_qk,) * 6 + (ut_shape, ut_shape, jax.ShapeDtypeStruct((nb, 8, QK_DIM), F32)),
        grid=(nb,),
        in_specs=[pl.BlockSpec((c, QK_DIM), lambda i: (i, 0)),
                  pl.BlockSpec((c, QK_DIM), lambda i: (i, 1)),
                  pl.BlockSpec((c, V_DIM), lambda i: (i, 1)),
                  pl.BlockSpec((c, LANES), lambda i: (i, 0)),
                  pl.BlockSpec((LANES, 4 * QK_DIM), lambda i: (0, 0)),
                  pl.BlockSpec((1, 2 * QK_DIM), lambda i: (0, 0))],
        out_specs=(qk_spec,) * 6 + (ut_spec, ut_spec,
                                    pl.BlockSpec((1, 8, QK_DIM), lambda i: (i, 0, 0))),
        compiler_params=_params("parallel"), name="gla_prep",
    )(qkv, qkv, qkv, dlow, wup, bdec)


def _gla_scan_kernel(utf_ref, dtf_ref, utb_ref, dtb_ref, stf_ref, stb_ref, sf_ref, sb_ref):
    @pl.when(pl.program_id(0) == 0)
    def _():
        sf_ref[...] = jnp.zeros_like(sf_ref)
        sb_ref[...] = jnp.zeros_like(sb_ref)

    sf = sf_ref[...]
    stf_ref[0] = sf.astype(BF16)
    sf_ref[...] = dtf_ref[0, 0:1, :] * sf + utf_ref[0]
    sb = sb_ref[...]
    stb_ref[0] = sb.astype(BF16)
    sb_ref[...] = dtb_ref[0, 1:2, :] * sb + utb_ref[0]


def _gla_scan(utf, utb, dtot, ctx_blocks):
    nb = utf.shape[0]

    def bwd(t):
        return jnp.where(t < ctx_blocks, ctx_blocks - 1 - t, nb + ctx_blocks - 1 - t)

    st_shape = jax.ShapeDtypeStruct((nb, GLA_DV, QK_DIM), BF16)
    blk = (1, GLA_DV, QK_DIM)
    return pl.pallas_call(
        _gla_scan_kernel,
        out_shape=(st_shape, st_shape),
        grid=(nb,),
        in_specs=[pl.BlockSpec(blk, lambda t: (t, 0, 0)),
                  pl.BlockSpec((1, 8, QK_DIM), lambda t: (t, 0, 0)),
                  pl.BlockSpec(blk, lambda t: (bwd(t), 0, 0)),
                  pl.BlockSpec((1, 8, QK_DIM), lambda t: (bwd(t), 0, 0))],
        out_specs=(pl.BlockSpec(blk, lambda t: (t, 0, 0)),
                   pl.BlockSpec(blk, lambda t: (bwd(t), 0, 0))),
        scratch_shapes=[pltpu.VMEM((GLA_DV, QK_DIM), F32), pltpu.VMEM((GLA_DV, QK_DIM), F32)],
        compiler_params=_params("arbitrary"), name="gla_scan",
    )(utf, dtot, utb, dtot)


def _gla_out_kernel(qmf_ref, kmf_ref, qef_ref, qmb_ref, kmb_ref, qeb_ref, vb_ref,
                    stf_ref, stb_ref, r_ref, gain_ref, a_ref):
    c = qmf_ref.shape[0]
    ri = lax.broadcasted_iota(jnp.int32, (c, c), 0)
    ci = lax.broadcasted_iota(jnp.int32, (c, c), 1)
    lower = ci <= ri
    upper = ci >= ri
    lane = lax.broadcasted_iota(jnp.int32, (1, HEAD_PAIR), 1)
    for p in range(GLA_HEADS // 2):
        pair = slice(p * HEAD_PAIR, (p + 1) * HEAD_PAIR)
        qmf, kmf, qef = qmf_ref[:, pair], kmf_ref[:, pair], qef_ref[:, pair]
        qmb, kmb, qeb = qmb_ref[:, pair], kmb_ref[:, pair], qeb_ref[:, pair]
        stf = stf_ref[0, :, pair]
        stb = stb_ref[0, :, pair]
        for hh in range(2):
            h = 2 * p + hh
            keep = (lane < GLA_DK) if hh == 0 else (lane >= GLA_DK)
            zero = jnp.zeros_like(kmf)
            af = lax.dot_general(qmf, jnp.where(keep, kmf, zero), NT_DIMS, preferred_element_type=F32)
            ab = lax.dot_general(qmb, jnp.where(keep, kmb, zero), NT_DIMS, preferred_element_type=F32)
            att = (jnp.where(lower, af, 0.0) + jnp.where(upper, ab, 0.0)).astype(BF16)
            cols = slice(h * GLA_DV, (h + 1) * GLA_DV)
            o = jnp.dot(att, vb_ref[:, cols], preferred_element_type=F32)
            o = o + lax.dot_general(jnp.where(keep, qef, zero), stf, NT_DIMS, preferred_element_type=F32)
            o = o + lax.dot_general(jnp.where(keep, qeb, zero), stb, NT_DIMS, preferred_element_type=F32)
            o = o * lax.rsqrt(jnp.mean(o * o, axis=-1, keepdims=True) + EPS)
            r = r_ref[:, cols].astype(F32)
            a_ref[:, cols] = (o * gain_ref[:, cols] * (r * jax.nn.sigmoid(r))).astype(BF16)


def _branches_kernel(*refs):
    _gla_out_kernel(*refs[:11], refs[17])
    _sgu_kernel(*refs[11:17], refs[18])


def _branch_outputs(prep, qkv, stf, stb, p_ruv, gain, ln_g, ln_b, w_s, layer, b_s_bcast):
    qmf, kmf, qef, qmb, kmb, qeb = prep
    m = qkv.shape[0]
    c = ROW_BLOCK
    qk_spec = pl.BlockSpec((c, QK_DIM), lambda i: (i, 0))
    st_spec = pl.BlockSpec((1, GLA_DV, QK_DIM), lambda i: (i, 0, 0))
    vec_spec = pl.BlockSpec((1, V_DIM), lambda i: (0, 0))
    out_spec = pl.BlockSpec((c, V_DIM), lambda i: (i, 0))
    out_shape = jax.ShapeDtypeStruct((m, V_DIM), BF16)
    assert SG_DIM == V_DIM
    return pl.pallas_call(
        _branches_kernel,
        out_shape=(out_shape, out_shape),
        grid=(m // c,),
        in_specs=[qk_spec] * 6 + [pl.BlockSpec((c, V_DIM), lambda i: (i, 1)), st_spec, st_spec,
                                  pl.BlockSpec((c, V_DIM), lambda i: (i, RUV_COL_R)), vec_spec,
                                  pl.BlockSpec((c, SG_DIM), lambda i: (i, RUV_COL_U)),
                                  pl.BlockSpec((c, SG_DIM), lambda i: (i, RUV_COL_VS)),
                                  vec_spec, vec_spec,
                                  pl.BlockSpec((None, SG_GROUPS, SG_CHUNK, SG_CHUNK),
                                               lambda i: (layer, 0, 0, 0)),
                                  pl.BlockSpec((SG_GROUPS, SG_CHUNK, SG_GROUP_CH), lambda i: (0, 0, 0))],
        out_specs=(out_spec, out_spec),
        compiler_params=_params("parallel"), name="branches",
    )(qmf, kmf, qef, qmb, kmb, qeb, qkv, stf, stb, p_ruv, gain.reshape(1, V_DIM),
      p_ruv, p_ruv, ln_g.reshape(1, SG_DIM), ln_b.reshape(1, SG_DIM), w_s, b_s_bcast)


ONE_F32_BITS = 0x3F800000
SEARCH_STEPS = 30


def _route_kernel(logit_ref, g_ref, *, ctx_rows):
    m = logit_ref.shape[0]
    rb = ROW_BLOCK
    lane = lax.broadcasted_iota(jnp.int32, (1, LANES), 1)
    valid = lane < N_EXPERTS

    def softmax_block(i, carry):
        r0 = pl.multiple_of(i * rb, rb)
        lg = jnp.where(valid, logit_ref[pl.ds(r0, rb), :], -jnp.inf)
        ex = jnp.exp(lg - jnp.max(lg, axis=-1, keepdims=True))
        g_ref[pl.ds(r0, rb), :] = ex / jnp.sum(ex, axis=-1, keepdims=True)
        return carry

    lax.fori_loop(0, m // rb, softmax_block, 0)

    def bits_of(r0):
        return lax.bitcast_convert_type(g_ref[pl.ds(r0, rb), :], jnp.int32)

    def count_ge(blk0, nblk, t):
        def body(i, acc):
            r0 = pl.multiple_of((blk0 + i) * rb, rb)
            hit = jnp.where(bits_of(r0) >= t, 1, 0)
            return acc + jnp.sum(hit.reshape(rb // 8, 8, LANES), axis=0)

        acc = lax.fori_loop(0, nblk, body, jnp.zeros((8, LANES), jnp.int32))
        return jnp.sum(acc, axis=0, keepdims=True)

    def route_set(blk0, nblk):
        cap = EC_CAPACITY * nblk * rb // N_EXPERTS

        def halve(_, lohi):
            lo, hi = lohi
            mid = lo + ((hi - lo) >> 1)
            ok = count_ge(blk0, nblk, mid) >= cap
            return jnp.where(ok, mid, lo), jnp.where(ok, hi, mid)

        thr, _ = lax.fori_loop(0, SEARCH_STEPS, halve,
                               (jnp.zeros((1, LANES), jnp.int32),
                                jnp.full((1, LANES), ONE_F32_BITS + 1, jnp.int32)))
        need = (cap - count_ge(blk0, nblk, thr + 1)).astype(F32)
        ri = lax.broadcasted_iota(jnp.int32, (rb, rb), 0)
        ci = lax.broadcasted_iota(jnp.int32, (rb, rb), 1)
        earlier = jnp.where(ci < ri, 1.0, 0.0).astype(BF16)

        def finalize(i, seen):
            r0 = pl.multiple_of((blk0 + i) * rb, rb)
            aff = g_ref[pl.ds(r0, rb), :]
            bits = lax.bitcast_convert_type(aff, jnp.int32)
            tie = bits == thr
            tie_f = jnp.where(tie, 1.0, 0.0)
            rank = seen + jnp.dot(earlier, tie_f.astype(BF16), preferred_element_type=F32)
            g_ref[pl.ds(r0, rb), :] = jnp.where(tie, jnp.where(rank < need, aff, 0.0),
                                                jnp.where(bits > thr, aff, 0.0))
            return seen + jnp.sum(tie_f, axis=0, keepdims=True)

        lax.fori_loop(0, nblk, finalize, jnp.zeros((1, LANES), F32))

    route_set(0, ctx_rows // rb)
    route_set(ctx_rows // rb, (m - ctx_rows) // rb)


def _route(logits, ctx_rows):
    m = logits.shape[0]
    return pl.pallas_call(
        functools.partial(_route_kernel, ctx_rows=ctx_rows),
        out_shape=jax.ShapeDtypeStruct((m, LANES), F32),
        grid=(1,),
        in_specs=[pl.BlockSpec((m, LANES), lambda i: (0, 0))],
        out_specs=pl.BlockSpec((m, LANES), lambda i: (0, 0)),
        compiler_params=_params("arbitrary"), name="route",
    )(logits)


SLOTS_PER_BLOCK = 64
GATE_PARTS = 3


def _slot_ranks(g):
    rb = g.shape[0]
    picked = jnp.where(g > 0.0, 1.0, 0.0)
    ri = lax.broadcasted_iota(jnp.int32, (rb, rb), 0)
    ci = lax.broadcasted_iota(jnp.int32, (rb, rb), 1)
    earlier = jnp.where(ci < ri, 1.0, 0.0).astype(BF16)
    rank = jnp.dot(earlier, picked.astype(BF16), preferred_element_type=F32)
    return jnp.where(picked > 0.0, rank, -1.0), jnp.sum(picked, axis=0, keepdims=True)


def _slots_kernel(g_ref, pos_ref):
    pos_ref[...] = _slot_ranks(g_ref[...])[0]


def _slots(gates):
    m = gates.shape[0]
    rb = ROW_BLOCK
    return pl.pallas_call(
        _slots_kernel,
        out_shape=jax.ShapeDtypeStruct((m, LANES), F32),
        grid=(m // rb,),
        in_specs=[pl.BlockSpec((rb, LANES), lambda i: (i, 0))],
        out_specs=pl.BlockSpec((rb, LANES), lambda i: (i, 0)),
        compiler_params=_params("parallel"), name="moe_slots",
    )(gates)


def _slot_onehot(post):
    rb = post.shape[1]
    slot = lax.broadcasted_iota(jnp.int32, (SLOTS_PER_BLOCK, rb), 0).astype(F32)
    parts = [jnp.where(slot == post[e:e + 1, :], 1.0, 0.0) for e in range(N_EXPERTS)]
    return jnp.concatenate(parts, axis=0).astype(BF16)


def _dispatch_kernel(h_ref, g_ref, xs_ref, gs_ref, post_ref, cnt_ref):
    pos, cnt = _slot_ranks(g_ref[...])
    post = pos.T[:N_EXPERTS, :]
    post_ref[...] = post
    cnt_ref[0] = jnp.broadcast_to(cnt, (8, LANES))
    onehot = _slot_onehot(post)
    d = h_ref.shape[1]
    xs = jnp.dot(onehot, h_ref[...], preferred_element_type=F32)
    xs_ref[...] = xs.astype(BF16).reshape(N_EXPERTS, SLOTS_PER_BLOCK, d)
    g = g_ref[...]
    pieces = jnp.zeros_like(g)
    for k in range(GATE_PARTS):
        piece = g.astype(BF16).astype(F32)
        pieces = pieces + (pltpu.roll(piece, k * N_EXPERTS, axis=1) if k else piece)
        g = g - piece
    gs = jnp.dot(onehot, pieces.astype(BF16), preferred_element_type=F32)
    gs_ref[...] = gs.reshape(N_EXPERTS, SLOTS_PER_BLOCK, LANES)


def _dispatch(h, gates):
    m, d = h.shape
    rb = ROW_BLOCK
    nb = m // rb
    s = SLOTS_PER_BLOCK
    return pl.pallas_call(
        _dispatch_kernel,
        out_shape=(jax.ShapeDtypeStruct((N_EXPERTS, nb * s, d), BF16),
                   jax.ShapeDtypeStruct((N_EXPERTS, nb * s, LANES), F32),
                   jax.ShapeDtypeStruct((N_EXPERTS, m), F32),
                   jax.ShapeDtypeStruct((nb, 8, LANES), F32)),
        grid=(nb,),
        in_specs=[pl.BlockSpec((rb, d), lambda i: (i, 0)),
                  pl.BlockSpec((rb, LANES), lambda i: (i, 0))],
        out_specs=(pl.BlockSpec((N_EXPERTS, s, d), lambda i: (0, i, 0)),
                   pl.BlockSpec((N_EXPERTS, s, LANES), lambda i: (0, i, 0)),
                   pl.BlockSpec((N_EXPERTS, rb), lambda i: (0, i)),
                   pl.BlockSpec((1, 8, LANES), lambda i: (i, 0, 0))),
        compiler_params=_params("parallel"), name="moe_dispatch",
    )(h, gates)


def _expert_kernel(xs_ref, w1_ref, w3_ref, w2_ref, gs_ref, y_ref):
    e = pl.program_id(0)
    lane = lax.broadcasted_iota(jnp.int32, (1, LANES), 1)
    mine = (lane % N_EXPERTS == e) & (lane < GATE_PARTS * N_EXPERTS)
    gate = jnp.sum(jnp.where(mine, gs_ref[0], 0.0), axis=-1, keepdims=True)
    x = xs_ref[0]
    h1 = jnp.dot(x, w1_ref[0], preferred_element_type=F32)
    h3 = jnp.dot(x, w3_ref[0], preferred_element_type=F32)
    hid = (h1 * jax.nn.sigmoid(h1) * h3 * gate).astype(BF16)
    y_ref[0] = jnp.dot(hid, w2_ref[0], preferred_element_type=F32).astype(BF16)


def _expert_ffn(xs, gs, w1, w3, w2, layer):
    e, rows, d = xs.shape
    f = w1.shape[-1]
    tr = _pick(rows, (832, 1024, 512, 256, 128, 64, 16))
    return pl.pallas_call(
        _expert_kernel,
        out_shape=jax.ShapeDtypeStruct((e, rows, d), BF16),
        grid=(e, rows // tr),
        in_specs=[pl.BlockSpec((1, tr, d), lambda i, j: (i, j, 0)),
                  pl.BlockSpec((None, 1, d, f), lambda i, j: (layer, i, 0, 0)),
                  pl.BlockSpec((None, 1, d, f), lambda i, j: (layer, i, 0, 0)),
                  pl.BlockSpec((None, 1, f, d), lambda i, j: (layer, i, 0, 0)),
                  pl.BlockSpec((1, tr, LANES), lambda i, j: (i, j, 0))],
        out_specs=pl.BlockSpec((1, tr, d), lambda i, j: (i, j, 0)),
        compiler_params=_params("parallel", "arbitrary"), name="expert_ffn",
    )(xs, w1, w3, w2, gs)


def _combined_rows(ys_ref, post_ref, x_ref, mod_ref, gate_row):
    onehot = _slot_onehot(post_ref[...])
    d = x_ref.shape[1]
    ys = ys_ref[...].reshape(N_EXPERTS * SLOTS_PER_BLOCK, d)
    moe = lax.dot_general(onehot, ys, TN_DIMS, preferred_element_type=F32)
    return x_ref[...] + mod_ref[0, gate_row:gate_row + 1, :] * moe


def _combine_prenorm_kernel(ys_ref, post_ref, x_ref, mod_ref, g_ref, nmod_ref, o_ref, h_ref, *,
                            gate_row):
    x = _combined_rows(ys_ref, post_ref, x_ref, mod_ref, gate_row)
    o_ref[...] = x
    h_ref[...] = _modulated_norm(x, g_ref[...], nmod_ref[0], 0, 1).astype(BF16)


def _combine_final_kernel(ys_ref, post_ref, x_ref, mod_ref, g_ref, o_ref, *, gate_row):
    x = _combined_rows(ys_ref, post_ref, x_ref, mod_ref, gate_row)
    ms = jnp.mean(x * x, axis=-1, keepdims=True)
    o_ref[...] = x * lax.rsqrt(ms + EPS) * g_ref[...]


def _combine(ys, post, x, mods, ctx_rows, gate_row, norm_g, next_mods=None):
    m, d = x.shape
    rb = ROW_BLOCK
    s = SLOTS_PER_BLOCK
    stream = _stream_of_block(rb, ctx_rows)
    row_spec = pl.BlockSpec((rb, d), lambda i: (i, 0))
    mod_spec = pl.BlockSpec((1, 8, d), lambda i: (stream(i), 0, 0))
    in_specs = [pl.BlockSpec((N_EXPERTS, s, d), lambda i: (0, i, 0)),
                pl.BlockSpec((N_EXPERTS, rb), lambda i: (0, i)),
                row_spec, mod_spec, pl.BlockSpec((1, d), lambda i: (0, 0))]
    if next_mods is not None:
        return pl.pallas_call(
            functools.partial(_combine_prenorm_kernel, gate_row=gate_row),
            out_shape=(jax.ShapeDtypeStruct((m, d), F32), jax.ShapeDtypeStruct((m, d), BF16)),
            grid=(m // rb,),
            in_specs=in_specs + [mod_spec],
            out_specs=(row_spec, row_spec),
            compiler_params=_params("parallel"), name="moe_combine",
        )(ys, post, x, mods, norm_g.reshape(1, d), next_mods)
    skip = ctx_rows // rb
    return pl.pallas_call(
        functools.partial(_combine_final_kernel, gate_row=gate_row),
        out_shape=jax.ShapeDtypeStruct((m - ctx_rows, d), F32),
        grid=(m // rb,),
        in_specs=in_specs,
        out_specs=pl.BlockSpec((rb, d), lambda i: (jnp.maximum(i - skip, 0), 0)),
        compiler_params=_params("arbitrary"), name="moe_combine_final",
    )(ys, post, x, mods, norm_g.reshape(1, d))


def _moe_hidden_kernel(h_ref, w1_ref, w3_ref, g_ref, o_ref):
    e = pl.program_id(1)
    lane = lax.broadcasted_iota(jnp.int32, (1, LANES), 1)
    gate = jnp.sum(jnp.where(lane == e, g_ref[...], 0.0), axis=-1, keepdims=True)
    h = h_ref[...]
    h1 = jnp.dot(h, w1_ref[0].astype(BF16), preferred_element_type=F32)
    h3 = jnp.dot(h, w3_ref[0].astype(BF16), preferred_element_type=F32)
    o_ref[...] = (h1 * jax.nn.sigmoid(h1) * h3 * gate).astype(BF16)


def _moe_hidden(h, gates, w1, w3, layer):
    m, d = h.shape
    _, e, _, f = w1.shape
    tm = _pick(m, (1280, 1024, 768, 512, 256))
    return pl.pallas_call(
        _moe_hidden_kernel,
        out_shape=jax.ShapeDtypeStruct((m, e * f), BF16),
        grid=(m // tm, e),
        in_specs=[pl.BlockSpec((tm, d), lambda i, j: (i, 0)),
                  pl.BlockSpec((None, 1, d, f), lambda i, j: (layer, j, 0, 0)),
                  pl.BlockSpec((None, 1, d, f), lambda i, j: (layer, j, 0, 0)),
                  pl.BlockSpec((tm, LANES), lambda i, j: (i, 0))],
        out_specs=pl.BlockSpec((tm, f), lambda i, j: (i, j)),
        compiler_params=_params("parallel", "arbitrary"), name="moe_hidden",
    )(h, w1, w3, gates)


def kernel(x, c, ctx, c_ctx, ada_w, ada_b, norm1_g, norm2_g, w_in, w_dec_up, b_dec, gla_norm_g,
           sg_ln_g, sg_ln_b, sg_w, sg_b, w_branch_a, w_branch_b, w_out, w_router, w_exp1, w_exp3,
           w_exp2, final_g):
    batch, seq, d = x.shape
    ctx_rows = ctx.shape[1]
    depth = ada_w.shape[0]
    assert batch == 1 and seq % ROW_BLOCK == 0 and ctx_rows % ROW_BLOCK == 0

    cond2 = jnp.stack([c_ctx, c[0]], axis=1)
    mods_all = _adaln(cond2, ada_w, ada_b).reshape(depth, 2, 6, d)
    mods_all = jnp.pad(mods_all, ((0, 0), (0, 0), (0, 2), (0, 0)))

    w1_all, w3_all, w2_all = (w.astype(BF16) for w in (w_exp1, w_exp3, w_exp2))
    w2_flat = w2_all.reshape(depth, N_EXPERTS * EXPERT_FF, d)
    sg_w_all = sg_w.astype(BF16)
    n_in = w_in.shape[2]
    w_in_rows = jnp.swapaxes(w_in, 1, 2).reshape(depth * n_in, d)

    for l in range(depth):
        mods = mods_all[l]
        w_head_t = _cast_rows(w_in_rows, l * n_in, OFF_DF + LANES)
        w_rest_t = _cast_rows(w_in_rows, l * n_in + SCAN_COLS, n_in - SCAN_COLS)
        wup = jnp.zeros((LANES, 2 * QK_DIM), F32)
        wup = wup.at[:DECAY_RANK, :QK_DIM].set(w_dec_up[l, 0])
        wup = wup.at[DECAY_RANK:2 * DECAY_RANK, QK_DIM:].set(w_dec_up[l, 1])
        bdec = b_dec[l].reshape(1, 2 * QK_DIM)
        b_s_bcast = jnp.broadcast_to(sg_b[l][:, :, None], (SG_GROUPS, SG_CHUNK, SG_GROUP_CH))
        w_r = jnp.pad(w_router[l], ((0, 0), (0, LANES - N_EXPERTS)))

        if l == 0:
            xa, h = _stack_prenorm(ctx[0], x[0], norm1_g[l], mods)
        p_qkv, dlow = _matmul_nt_with_side(h, w_head_t, OFF_DF, BF16)
        p_ruv = _matmul_nt(h, w_rest_t, 3 * V_DIM, BF16)

        prep = _gla_prep(p_qkv, dlow, jnp.concatenate(_split_bf16(wup), axis=1), bdec)
        stf, stb = _gla_scan(prep[6], prep[7], prep[8], ctx_rows // ROW_BLOCK)
        a, s = _branch_outputs(prep[:6], p_qkv, stf, stb, p_ruv, gla_norm_g[l],
                               sg_ln_g[l], sg_ln_b[l], sg_w_all, l, b_s_bcast)
        mix = _merge(h, a, s, w_rest_t, 3 * V_DIM, w_branch_a, w_branch_b, l)
        xa = _matmul_gated_residual(mix, w_out, xa, mods, ctx_rows, gate_row=2, layer=l)

        h2, logits = _prenorm_router(xa, norm2_g[l], mods, ctx_rows, 3, 4, w_r)
        gates = _route(logits, ctx_rows)
        xs, gs, post, cnt = _dispatch(h2, gates)
        ys = _expert_ffn(xs, gs, w1_all, w3_all, w2_all, l)

        def combine(xa, l=l, ys=ys, post=post, mods=mods):
            if l + 1 < depth:
                return _combine(ys, post, xa, mods, ctx_rows, 5, norm1_g[l + 1], mods_all[l + 1])
            return _combine(ys, post, xa, mods, ctx_rows, 5, final_g)

        def combine_after_overflow(xa, l=l, h2=h2, gates=gates, mods=mods, combine=combine):
            late = jnp.where(_slots(gates) >= SLOTS_PER_BLOCK, gates, 0.0)
            hid = _moe_hidden(h2, late, w1_all, w3_all, l)
            return combine(_matmul_gated_residual(hid, w2_flat, xa, mods, ctx_rows, gate_row=5, layer=l))

        res = lax.cond(jnp.max(cnt) > SLOTS_PER_BLOCK, combine_after_overflow, combine, xa)
        if l + 1 < depth:
            xa, h = res
        else:
            out = res

    return out[None]
```
